```python
import jax, jax.numpy as jnp
from jax import lax
import numpy as np

D_MODEL = 1024
BATCH = 16
SEQ = 2048
DEPTH = 1

HEAD_DIM = 64
N_Q_HEADS = 8
N_KV_HEADS = 2
Q_PER_KV = N_Q_HEADS // N_KV_HEADS
ATTN_WIDTH = N_Q_HEADS * HEAD_DIM
KV_WIDTH = N_KV_HEADS * HEAD_DIM
WINDOW = 128
ATTN_BLOCK = WINDOW
CONV_WIDTH = D_MODEL // 2
CONV_KERNEL = 31
MIX_WIDTH = ATTN_WIDTH + CONV_WIDTH
IN_WIDTH = ATTN_WIDTH + 2 * KV_WIDTH + 2 * CONV_WIDTH
N_EXPERTS = 32
TOP_K = 4
D_EXPERT = D_MODEL
SWIGLU_LIMIT = 7.0
SWIGLU_ALPHA = 1.702
EXPERT_BLOCK = 128
RMS_EPS = 1e-5
LN_EPS = 1e-5

kernel_name = "hymba_conformer_swa_sink_moe"


def rms_norm(x, g):
    xf = x.astype(jnp.float32)
    y = xf * lax.rsqrt(jnp.mean(xf * xf, axis=-1, keepdims=True) + RMS_EPS)
    return (y * g.astype(jnp.float32)).astype(x.dtype)


def layer_norm(x, g, b):
    xf = x.astype(jnp.float32)
    mu = jnp.mean(xf, axis=-1, keepdims=True)
    xc = xf - mu
    y = xc * lax.rsqrt(jnp.mean(xc * xc, axis=-1, keepdims=True) + LN_EPS)
    return (y * g.astype(jnp.float32) + b.astype(jnp.float32)).astype(x.dtype)


def sliding_window_attention_with_sinks(q, k, v, sinks):
    b, s, _ = q.shape
    L = ATTN_BLOCK
    nb = s // L
    q = q.reshape(b, nb, L, N_KV_HEADS, Q_PER_KV, HEAD_DIM)
    k = k.reshape(b, nb, L, N_KV_HEADS, HEAD_DIM)
    v = v.reshape(b, nb, L, N_KV_HEADS, HEAD_DIM)

    def band(t):
        prev = jnp.concatenate([jnp.zeros_like(t[:, :1]), t[:, :-1]], axis=1)
        return jnp.concatenate([prev, t], axis=2)

    kb, vb = band(k), band(v)
    scores = jnp.einsum('bnqhgd,bnkhd->bnhgqk', q, kb).astype(jnp.float32) * (HEAD_DIM ** -0.5)
    blk = jnp.arange(nb)[:, None, None]
    q_pos = blk * L + jnp.arange(L)[None, :, None]
    k_pos = (blk - 1) * L + jnp.arange(2 * L)[None, None, :]
    valid = (k_pos <= q_pos) & (q_pos - k_pos < WINDOW) & (k_pos >= 0)
    scores = jnp.where(valid[None, :, None, None], scores, -jnp.inf)
    sink = sinks.astype(jnp.float32).reshape(1, 1, N_KV_HEADS, Q_PER_KV, 1, 1)
    sink = jnp.broadcast_to(sink, scores.shape[:-1] + (1,))
    probs = jax.nn.softmax(jnp.concatenate([scores, sink], axis=-1), axis=-1)[..., :-1]
    out = jnp.einsum('bnhgqk,bnkhd->bnqhgd', probs.astype(v.dtype), vb)
    return out.reshape(b, s, ATTN_WIDTH)


def conformer_conv(a, gate, w_dw, b_dw, ln_g, ln_b):
    u = a * jax.nn.sigmoid(gate)
    u = lax.conv_general_dilated(
        u, w_dw[:, None, :].astype(u.dtype), window_strides=(1,),
        padding=[(CONV_KERNEL - 1, 0)],
        dimension_numbers=('NWC', 'WIO', 'NWC'),
        feature_group_count=CONV_WIDTH) + b_dw
    u = layer_norm(u, ln_g, ln_b)
    return jax.nn.silu(u)


def moe_ffn(h, w_router, b_router, w1, b1, w2, b2):
    bsz, s, d = h.shape
    t = bsz * s
    xt = h.reshape(t, d)
    logits = (xt @ w_router + b_router).astype(jnp.float32)
    top_val, top_idx = lax.top_k(logits, TOP_K)
    gates = jax.nn.softmax(top_val, axis=-1)
    n_assign = t * TOP_K
    e_flat = top_idx.reshape(n_assign).astype(jnp.int32)
    tok_flat = jnp.arange(n_assign, dtype=jnp.int32) // TOP_K
    g_flat = gates.reshape(n_assign)
    order = jnp.argsort(e_flat, stable=True)
    e_s, tok_s, g_s = e_flat[order], tok_flat[order], g_flat[order]
    counts = jnp.bincount(e_flat, length=N_EXPERTS).astype(jnp.int32)
    starts = jnp.cumsum(counts) - counts
    padded = (counts + EXPERT_BLOCK - 1) // EXPERT_BLOCK * EXPERT_BLOCK
    pad_end = jnp.cumsum(padded)
    pad_start = pad_end - padded
    dest = pad_start[e_s] + (jnp.arange(n_assign, dtype=jnp.int32) - starts[e_s])
    n_blocks = (n_assign + EXPERT_BLOCK - 1) // EXPERT_BLOCK + N_EXPERTS
    n_rows = n_blocks * EXPERT_BLOCK
    row_tok = jnp.full((n_rows,), t, jnp.int32).at[dest].set(tok_s)
    row_gate = jnp.zeros((n_rows,), jnp.float32).at[dest].set(g_s)
    block_start = jnp.arange(n_blocks, dtype=jnp.int32) * EXPERT_BLOCK
    block_expert = jnp.minimum(jnp.searchsorted(pad_end, block_start, side='right'),
                               N_EXPERTS - 1).astype(jnp.int32)
    x_pad = jnp.concatenate([xt, jnp.zeros((1, d), xt.dtype)], axis=0)
    x_rows = x_pad[row_tok].reshape(n_blocks, EXPERT_BLOCK, d)

    def expert_block(args):
        xb, e = args
        hh = (xb @ w1[e] + b1[e]).reshape(EXPERT_BLOCK, D_EXPERT, 2)
        x_glu = jnp.minimum(hh[..., 0], SWIGLU_LIMIT)
        x_lin = jnp.clip(hh[..., 1], -SWIGLU_LIMIT, SWIGLU_LIMIT)
        act = x_glu * jax.nn.sigmoid(SWIGLU_ALPHA * x_glu) * (x_lin + 1)
        return act @ w2[e] + b2[e]

    y_rows = lax.map(expert_block, (x_rows, block_expert)).reshape(n_rows, d)
    y_rows = y_rows * row_gate[:, None].astype(y_rows.dtype)
    y = jax.ops.segment_sum(y_rows, row_tok, num_segments=t + 1)[:t]
    return y.reshape(bsz, s, d)


def setup_inputs(seed: int = 0) -> dict:
    key = jax.random.key(seed)
    ks = jax.random.split(key, 24)
    f32 = jnp.float32

    def nrm(k, shape, scale):
        return jax.random.normal(k, shape, f32) * scale

    L = DEPTH
    return {
        "x": nrm(ks[0], (BATCH, SEQ, D_MODEL), 1.0),
        "g_mix": 1.0 + nrm(ks[1], (L, D_MODEL), 0.02),
        "w_in": nrm(ks[2], (L, D_MODEL, IN_WIDTH), D_MODEL ** -0.5),
        "b_in": nrm(ks[3], (L, IN_WIDTH), 0.02),
        "sinks": nrm(ks[4], (L, N_Q_HEADS), 0.5),
        "w_dw": nrm(ks[5], (L, CONV_KERNEL, CONV_WIDTH), CONV_KERNEL ** -0.5),
        "b_dw": nrm(ks[6], (L, CONV_WIDTH), 0.02),
        "ln_g": 1.0 + nrm(ks[7], (L, CONV_WIDTH), 0.02),
        "ln_b": nrm(ks[8], (L, CONV_WIDTH), 0.02),
        "g_attn_out": 1.0 + nrm(ks[9], (L, ATTN_WIDTH), 0.02),
        "g_conv_out": 1.0 + nrm(ks[10], (L, CONV_WIDTH), 0.02),
        "w_out": nrm(ks[11], (L, MIX_WIDTH, D_MODEL), MIX_WIDTH ** -0.5),
        "b_out": nrm(ks[12], (L, D_MODEL), 0.02),
        "g_ffn": 1.0 + nrm(ks[13], (L, D_MODEL), 0.02),
        "w_router": nrm(ks[14], (L, D_MODEL, N_EXPERTS), D_MODEL ** -0.5),
        "b_router": nrm(ks[15], (L, N_EXPERTS), 0.01),
        "w1": nrm(ks[16], (L, N_EXPERTS, D_MODEL, 2 * D_EXPERT), D_MODEL ** -0.5),
        "b1": nrm(ks[17], (L, N_EXPERTS, 2 * D_EXPERT), 0.02),
        "w2": nrm(ks[18], (L, N_EXPERTS, D_EXPERT, D_MODEL), D_EXPERT ** -0.5),
        "b2": nrm(ks[19], (L, N_EXPERTS, D_MODEL), 0.02),
        "g_final": 1.0 + nrm(ks[20], (D_MODEL,), 0.02),
    }


def reference(x, g_mix, w_in, b_in, sinks, w_dw, b_dw, ln_g, ln_b, g_attn_out, g_conv_out,
              w_out, b_out, g_ffn, w_router, b_router, w1, b1, w2, b2, g_final):
    split_at = [ATTN_WIDTH, ATTN_WIDTH + KV_WIDTH, ATTN_WIDTH + 2 * KV_WIDTH,
                ATTN_WIDTH + 2 * KV_WIDTH + CONV_WIDTH]
    for l in range(DEPTH):
        u = rms_norm(x, g_mix[l])
        proj = u @ w_in[l] + b_in[l]
        q, k, v, conv_a, conv_gate = jnp.split(proj, split_at, axis=-1)
        attn = sliding_window_attention_with_sinks(q, k, v, sinks[l])
        conv = conformer_conv(conv_a, conv_gate, w_dw[l], b_dw[l], ln_g[l], ln_b[l])
        mixed = jnp.concatenate([rms_norm(attn, g_attn_out[l]),
                                 rms_norm(conv, g_conv_out[l])], axis=-1)
        x = x + (mixed @ w_out[l] + b_out[l])
        x = x + moe_ffn(rms_norm(x, g_ffn[l]), w_router[l], b_router[l],
                        w1[l], b1[l], w2[l], b2[l])
    return rms_norm(x, g_final)
```

```python
import functools

import jax
import jax.numpy as jnp
from jax import lax
from jax.experimental import pallas as pl
from jax.experimental.pallas import tpu as pltpu

HEAD_DIM = 64
N_Q_HEADS = 8
N_KV_HEADS = 2
Q_PER_KV = N_Q_HEADS // N_KV_HEADS
ATTN_WIDTH = N_Q_HEADS * HEAD_DIM
KV_WIDTH = N_KV_HEADS * HEAD_DIM
WINDOW = 128
CONV_KERNEL = 31
N_EXPERTS = 32
TOP_K = 4
SWIGLU_LIMIT = 7.0
SWIGLU_ALPHA = 1.702
RMS_EPS = 1e-5
LN_EPS = 1e-5

LANES = 128
SUBLANES = 8

TS = 256
CHUNK = SUBLANES
R_CAP = -(-(TS * TOP_K + N_EXPERTS * (CHUNK - 1)) // LANES) * LANES
CPT = R_CAP // CHUNK
TAIL_MAX = CPT - TS * TOP_K // CHUNK
_TAIL_PIECES = tuple(1 << i for i in reversed(range(TAIL_MAX.bit_length())))
TM = 256
CPB = TM // CHUNK
CONV_HALO = 32
CONV_ROWS = 64
NEG_BIG = -1e30
VMEM_LIMIT = 56 * 1024 * 1024


def _rms(x, g):
    return x * lax.rsqrt(jnp.mean(x * x, axis=-1, keepdims=True) + RMS_EPS) * g


def _dot(a, b):
    return jnp.dot(a, b, preferred_element_type=jnp.float32)


def _dot_nt(a, b, precision=None):
    return lax.dot_general(a, b, (((1,), (1,)), ((), ())), precision=precision,
                           preferred_element_type=jnp.float32)


def _mix_kernel(x_ref, gmix_ref, win_ref, bin_ref, sink_ref, wdw_ref, bdw_ref, lng_ref, lnb_ref,
                ga_ref, gc_ref, wout_ref, bout_ref, o_ref, kv_prev, u_buf, conv_buf):
    s = pl.program_id(1)
    first = s == 0

    @pl.when(first)
    def _():
        kv_prev[...] = jnp.zeros_like(kv_prev)
        u_buf[0:CONV_HALO, :] = jnp.zeros((CONV_HALO, u_buf.shape[1]), jnp.float32)

    x = x_ref[0]
    u = _rms(x, gmix_ref[...])
    proj = _dot(u.astype(jnp.bfloat16), win_ref[...]) + bin_ref[...]
    o_q, o_k, o_v, o_a, o_g = 0, ATTN_WIDTH, ATTN_WIDTH + KV_WIDTH, ATTN_WIDTH + 2 * KV_WIDTH, \
        ATTN_WIDTH + 2 * KV_WIDTH + ATTN_WIDTH
    q = proj[:, o_q:o_k] * (HEAD_DIM ** -0.5)
    k = proj[:, o_k:o_v]
    v = proj[:, o_v:o_a]
    conv_a = proj[:, o_a:o_g]
    conv_gate = proj[:, o_g:]

    lane = lax.broadcasted_iota(jnp.int32, (WINDOW + TS, LANES), 1)
    low = lane < HEAD_DIM
    kfull = jnp.concatenate([kv_prev[:, 0:KV_WIDTH], k], axis=0)
    vfull = jnp.concatenate([kv_prev[:, KV_WIDTH:], v], axis=0)
    kroll = pltpu.roll(kfull, HEAD_DIM, axis=1)
    vroll = pltpu.roll(vfull, HEAD_DIM, axis=1)
    kk = [jnp.where(low, kfull, kroll).astype(jnp.bfloat16), jnp.where(low, kroll, kfull).astype(jnp.bfloat16)]
    vv = [jnp.where(low, vfull, vroll).astype(jnp.bfloat16), jnp.where(low, vroll, vfull).astype(jnp.bfloat16)]
    kv_prev[:, 0:KV_WIDTH] = k[TS - WINDOW:, :]
    kv_prev[:, KV_WIDTH:] = v[TS - WINDOW:, :]

    rows = Q_PER_KV * WINDOW
    qi = lax.broadcasted_iota(jnp.int32, (rows, 2 * WINDOW), 0) & (WINDOW - 1)
    kj = lax.broadcasted_iota(jnp.int32, (rows, 2 * WINDOW), 1)
    band = (kj > qi) & (kj <= qi + WINDOW)
    lane_q = lax.broadcasted_iota(jnp.int32, (WINDOW, LANES), 1)
    low_q = lane_q < HEAD_DIM
    ones_kv = jnp.ones((2 * WINDOW, LANES), jnp.bfloat16)

    attn_blocks = []
    for b in range(TS // WINDOW):
        qb = q[b * WINDOW:(b + 1) * WINDOW, :]
        valid = band & ((kj >= WINDOW) | jnp.logical_not(first & (b == 0)))
        head_out = []
        for g in range(N_KV_HEADS):
            parts = []
            for i in range(Q_PER_KV):
                h = g * Q_PER_KV + i
                q128 = qb[:, (h // 2) * LANES:(h // 2 + 1) * LANES]
                keep = low_q if h % 2 == 0 else jnp.logical_not(low_q)
                parts.append(jnp.where(keep, q128, 0.0))
            qs = jnp.concatenate(parts, axis=0).astype(jnp.bfloat16)
            kb = kk[g][b * WINDOW:b * WINDOW + 2 * WINDOW, :]
            vb = vv[g][b * WINDOW:b * WINDOW + 2 * WINDOW, :]
            sc = _dot_nt(qs, kb)
            sc = jnp.where(valid, sc, NEG_BIG)
            sink = sink_ref[g]
            m = jnp.maximum(jnp.max(sc, axis=-1, keepdims=True), sink)
            p = jnp.exp(sc - m).astype(jnp.bfloat16)
            den = _dot(p, ones_kv) + jnp.exp(sink - m)
            pv = _dot(p, vb)
            o = pv / den
            for i in range(Q_PER_KV):
                head_out.append(o[i * WINDOW:(i + 1) * WINDOW, :])
        cols = [jnp.where(low_q, head_out[2 * j], head_out[2 * j + 1]) for j in range(N_Q_HEADS // 2)]
        attn_blocks.append(jnp.concatenate(cols, axis=1))
    attn = jnp.concatenate(attn_blocks, axis=0)

    cw = u_buf.shape[1]
    u_buf[CONV_HALO:CONV_HALO + TS, :] = conv_a * jax.nn.sigmoid(conv_gate)
    shift = CONV_HALO - (CONV_KERNEL - 1)
    for c in range(cw // LANES):
        cs = slice(c * LANES, (c + 1) * LANES)
        for r0 in range(0, TS, CONV_ROWS):
            acc = jnp.zeros((CONV_ROWS, LANES), jnp.float32)
            for j in range(CONV_KERNEL):
                acc = acc + wdw_ref[j:j + 1, cs] * u_buf[r0 + shift + j:r0 + shift + j + CONV_ROWS, cs]
            conv_buf[r0:r0 + CONV_ROWS, cs] = acc
    u_buf[0:CONV_HALO, :] = u_buf[TS:TS + CONV_HALO, :]
    cv = conv_buf[...] + bdw_ref[...]
    mu = jnp.mean(cv, axis=-1, keepdims=True)
    xc = cv - mu
    cv = xc * lax.rsqrt(jnp.mean(xc * xc, axis=-1, keepdims=True) + LN_EPS) * lng_ref[...] + lnb_ref[...]
    cv = cv * jax.nn.sigmoid(cv)

    mixed = jnp.concatenate([_rms(attn, ga_ref[...]), _rms(cv, gc_ref[...])], axis=1).astype(jnp.bfloat16)
    o_ref[0] = x + _dot(mixed, wout_ref[...]) + bout_ref[...]


def _mix(x, g_mix, w_in, b_in, sink_cols, w_dw, b_dw, ln_g, ln_b, g_a, g_c, w_out, b_out):
    bsz, seq, d = x.shape
    assert seq % TS == 0 and TS % WINDOW == 0 and CONV_HALO >= CONV_KERNEL - 1
    cw = w_dw.shape[1]
    const = lambda shape: pl.BlockSpec(shape, lambda b, s: (0,) * len(shape))
    return pl.pallas_call(
        _mix_kernel,
        grid=(bsz, seq // TS),
        in_specs=[
            pl.BlockSpec((1, TS, d), lambda b, s: (b, s, 0)),
            const((1, d)), const(w_in.shape), const((1, w_in.shape[1])),
            const(sink_cols.shape), const(w_dw.shape), const((1, cw)), const((1, cw)), const((1, cw)),
            const((1, ATTN_WIDTH)), const((1, cw)), const(w_out.shape), const((1, d)),
        ],
        out_specs=pl.BlockSpec((1, TS, d), lambda b, s: (b, s, 0)),
        out_shape=jax.ShapeDtypeStruct(x.shape, jnp.float32),
        scratch_shapes=[
            pltpu.VMEM((WINDOW, 2 * KV_WIDTH), jnp.float32),
            pltpu.VMEM((CONV_HALO + TS, cw), jnp.float32),
            pltpu.VMEM((TS, cw), jnp.float32),
        ],
        compiler_params=pltpu.CompilerParams(
            dimension_semantics=("arbitrary", "arbitrary"), vmem_limit_bytes=VMEM_LIMIT),
        name="mix",
    )(x, g_mix, w_in, b_in, sink_cols, w_dw, b_dw, ln_g, ln_b, g_a, g_c, w_out, b_out)


def _route_kernel(x1_ref, gffn_ref, wrt_ref, brt_ref, xs_ref, slot_ref, gate_ref, pcc_ref):
    h = _rms(x1_ref[...], gffn_ref[...])
    lg = _dot_nt(wrt_ref[...], h, precision=lax.Precision.HIGHEST) + brt_ref[...]
    iota_e = lax.broadcasted_iota(jnp.int32, (N_EXPERTS, TS), 0)
    vals, hots = [], []
    member = jnp.zeros((N_EXPERTS, TS), jnp.float32)
    for _ in range(TOP_K):
        mx = jnp.max(lg, axis=0, keepdims=True)
        idx = jnp.min(jnp.where(lg == mx, iota_e, N_EXPERTS), axis=0, keepdims=True)
        hot = iota_e == idx
        lg = jnp.where(hot, -jnp.inf, lg)
        member = member + hot.astype(jnp.float32)
        vals.append(mx)
        hots.append(hot)
    ex = [jnp.exp(vk - vals[0]) for vk in vals]
    den = ex[0] + ex[1] + ex[2] + ex[3]
    gate_ref[...] = jnp.concatenate([e / den for e in ex], axis=0)

    ti = lax.broadcasted_iota(jnp.int32, (TS, TS), 0)
    tj = lax.broadcasted_iota(jnp.int32, (TS, TS), 1)
    upper = jnp.where(ti < tj, 1.0, 0.0).astype(jnp.bfloat16)
    cum = _dot(member.astype(jnp.bfloat16), upper)
    cnt = jnp.sum(member, axis=1, keepdims=True)
    pcc = jnp.floor((cnt + (CHUNK - 1)) * (1.0 / CHUNK))
    pcb = jnp.broadcast_to(pcc, (N_EXPERTS, LANES))
    row_e = lax.broadcasted_iota(jnp.int32, (N_EXPERTS, LANES), 0)
    inc = pcb
    sh = 1
    while sh < N_EXPERTS:
        inc = inc + jnp.where(row_e >= sh, pltpu.roll(inc, sh, axis=0), 0.0)
        sh *= 2
    run_start = (inc - pcb)[:, 0:1] * CHUNK
    pcc_ref[0] = pcb.astype(jnp.int32)

    pos = run_start + cum
    slots = [jnp.sum(jnp.where(hot, pos, 0.0), axis=0, keepdims=True).astype(jnp.int32) for hot in hots]
    slot_ref[...] = jnp.concatenate(slots, axis=0)

    iota_r = lax.broadcasted_iota(jnp.int32, (R_CAP, TS), 0)
    sel = (iota_r == slots[0]) | (iota_r == slots[1]) | (iota_r == slots[2]) | (iota_r == slots[3])
    perm = jnp.where(sel, 1.0, 0.0).astype(jnp.bfloat16)
    xs_ref[0] = _dot(perm, h.astype(jnp.bfloat16))


def _route(x1, g_ffn, wr_t, br_t):
    t, d = x1.shape
    nt = t // TS
    const = lambda shape: pl.BlockSpec(shape, lambda i: (0,) * len(shape))
    return pl.pallas_call(
        _route_kernel,
        grid=(nt,),
        in_specs=[pl.BlockSpec((TS, d), lambda i: (i, 0)), const((1, d)), const(wr_t.shape), const(br_t.shape)],
        out_specs=[
            pl.BlockSpec((1, R_CAP, d), lambda i: (i, 0, 0)),
            pl.BlockSpec((TOP_K, TS), lambda i: (0, i)),
            pl.BlockSpec((TOP_K, TS), lambda i: (0, i)),
            pl.BlockSpec((1, N_EXPERTS, LANES), lambda i: (i, 0, 0)),
        ],
        out_shape=[
            jax.ShapeDtypeStruct((nt, R_CAP, d), jnp.float32),
            jax.ShapeDtypeStruct((TOP_K, t), jnp.int32),
            jax.ShapeDtypeStruct((TOP_K, t), jnp.float32),
            jax.ShapeDtypeStruct((nt, N_EXPERTS, LANES), jnp.int32),
        ],
        compiler_params=pltpu.CompilerParams(
            dimension_semantics=("arbitrary",), vmem_limit_bytes=VMEM_LIMIT),
        name="route",
    )(x1, g_ffn, wr_t, br_t)


def _experts_kernel(src_ref, bexp_ref, nval_ref, nvc_ref,
                    xs_hbm, w1_ref, b1g_ref, b1l_ref, w2_ref, b2_ref, dei_ref,
                    y_hbm,
                    xbuf, ybuf, zbuf, w1g, w1l, w2b, sem_in, sem_out, sem_z):
    b = pl.program_id(0)
    nb = pl.num_programs(0)
    n_tiles = nvc_ref.shape[0]
    slot = b % 2

    def zero_tail(tile, start):
        n = CPT - nvc_ref[tile]
        off = tile * CPT + nvc_ref[tile]
        for pc in _TAIL_PIECES:
            take = (n & pc) != 0

            @pl.when(take)
            def _(off=off, pc=pc):
                cp = pltpu.make_async_copy(zbuf.at[pl.ds(0, pc)], y_hbm.at[pl.ds(off, pc)], sem_z)
                if start:
                    cp.start()
                else:
                    cp.wait()
            off = off + jnp.where(take, pc, 0)

    @pl.when(b == 0)
    def _():
        zbuf[...] = jnp.zeros_like(zbuf)

    @pl.when(b < n_tiles)
    def _():
        zero_tail(b, True)

    def gather(blk, sl, start):
        def body(c, carry):
            cp = pltpu.make_async_copy(xs_hbm.at[src_ref[blk * CPB + c]],
                                       xbuf.at[sl, pl.ds(c * CHUNK, CHUNK)], sem_in.at[sl])
            if start:
                cp.start()
            else:
                cp.wait()
            return carry
        lax.fori_loop(0, nval_ref[blk], body, 0)

    def scatter(blk, sl, start):
        def body(c, carry):
            cp = pltpu.make_async_copy(ybuf.at[sl, pl.ds(c * CHUNK, CHUNK)],
                                       y_hbm.at[src_ref[blk * CPB + c]], sem_out.at[sl])
            if start:
                cp.start()
            else:
                cp.wait()
            return carry
        lax.fori_loop(0, nval_ref[blk], body, 0)

    @pl.when(b == 0)
    def _():
        xbuf[...] = jnp.zeros_like(xbuf)
        gather(0, 0, True)

    @pl.when(b + 1 < nb)
    def _():
        gather(b + 1, 1 - slot, True)

    prev_e = bexp_ref[jnp.maximum(b - 1, 0)]

    @pl.when((b == 0) | (bexp_ref[b] != prev_e))
    def _():
        dei = dei_ref[...]
        for j in range(w1_ref.shape[2] // (2 * LANES)):
            wj = w1_ref[0, :, j * 2 * LANES:(j + 1) * 2 * LANES].astype(jnp.bfloat16)
            r = _dot(wj, dei)
            w1g[:, j * LANES:(j + 1) * LANES] = r[:, 0:LANES].astype(jnp.bfloat16)
            w1l[:, j * LANES:(j + 1) * LANES] = r[:, LANES:].astype(jnp.bfloat16)
        w2b[...] = w2_ref[0].astype(jnp.bfloat16)

    @pl.when(b >= 2)
    def _():
        scatter(b - 2, slot, False)

    @pl.when(nval_ref[b] > 0)
    def _():
        gather(b, slot, False)
        xb = xbuf[slot].astype(jnp.bfloat16)
        hg = _dot(xb, w1g[...]) + b1g_ref[0]
        hl = _dot(xb, w1l[...]) + b1l_ref[0]
        hg = jnp.minimum(hg, SWIGLU_LIMIT)
        hl = jnp.clip(hl, -SWIGLU_LIMIT, SWIGLU_LIMIT)
        act = hg * jax.nn.sigmoid(SWIGLU_ALPHA * hg) * (hl + 1.0)
        ybuf[slot] = _dot(act.astype(jnp.bfloat16), w2b[...]) + b2_ref[0]
        scatter(b, slot, True)

    @pl.when(b < n_tiles)
    def _():
        zero_tail(b, False)

    @pl.when(b == nb - 1)
    def _():
        scatter(b, slot, False)

        @pl.when(b >= 1)
        def _():
            scatter(b - 1, 1 - slot, False)


def _experts(chunk_src, block_expert, block_nvalid, nvc, xs_chunks, w1, b1g, b1l, w2, b2, dei):
    n_chunks, _, d = xs_chunks.shape
    nb = block_expert.shape[0]
    assert nb >= nvc.shape[0]
    de2 = w1.shape[2]
    grid_spec = pltpu.PrefetchScalarGridSpec(
        num_scalar_prefetch=4,
        grid=(nb,),
        in_specs=[
            pl.BlockSpec(memory_space=pl.ANY),
            pl.BlockSpec((1, d, de2), lambda i, src, be, nv, nc: (be[i], 0, 0)),
            pl.BlockSpec((1, 1, de2 // 2), lambda i, src, be, nv, nc: (be[i], 0, 0)),
            pl.BlockSpec((1, 1, de2 // 2), lambda i, src, be, nv, nc: (be[i], 0, 0)),
            pl.BlockSpec((1, de2 // 2, d), lambda i, src, be, nv, nc: (be[i], 0, 0)),
            pl.BlockSpec((1, 1, d), lambda i, src, be, nv, nc: (be[i], 0, 0)),
            pl.BlockSpec(dei.shape, lambda i, src, be, nv, nc: (0, 0)),
        ],
        out_specs=pl.BlockSpec(memory_space=pl.ANY),
        scratch_shapes=[
            pltpu.VMEM((2, TM, d), jnp.float32),
            pltpu.VMEM((2, TM, d), jnp.float32),
            pltpu.VMEM((TAIL_MAX, CHUNK, d), jnp.float32),
            pltpu.VMEM((d, de2 // 2), jnp.bfloat16),
            pltpu.VMEM((d, de2 // 2), jnp.bfloat16),
            pltpu.VMEM((de2 // 2, d), jnp.bfloat16),
            pltpu.SemaphoreType.DMA((2,)),
            pltpu.SemaphoreType.DMA((2,)),
            pltpu.SemaphoreType.DMA(()),
        ],
    )
    return pl.pallas_call(
        _experts_kernel,
        grid_spec=grid_spec,
        out_shape=jax.ShapeDtypeStruct(xs_chunks.shape, jnp.float32),
        compiler_params=pltpu.CompilerParams(
            dimension_semantics=("arbitrary",), vmem_limit_bytes=VMEM_LIMIT),
        name="experts",
    )(chunk_src, block_expert, block_nvalid, nvc, xs_chunks, w1, b1g, b1l, w2, b2, dei)


_PIECES = tuple(1 << i for i in reversed(range((CPT).bit_length())))


def _combine_kernel(nvc_ref,
                    y_hbm, x1_ref, slot_ref, gate_ref, gfin_ref, o_ref, ybuf, sem):
    i = pl.program_id(0)
    nt = pl.num_programs(0)
    sl = i % 2

    def fetch(tile, s_, start):
        n = nvc_ref[tile]
        off = jnp.int32(0)
        for pc in _PIECES:
            if pc * CHUNK > R_CAP:
                continue
            take = (n & pc) != 0

            @pl.when(take)
            def _(off=off, pc=pc):
                cp = pltpu.make_async_copy(
                    y_hbm.at[tile, pl.ds(pl.multiple_of(off * CHUNK, CHUNK), pc * CHUNK)],
                    ybuf.at[s_, pl.ds(pl.multiple_of(off * CHUNK, CHUNK), pc * CHUNK)], sem.at[s_])
                if start:
                    cp.start()
                else:
                    cp.wait()
            off = off + jnp.where(take, pc, 0)

    @pl.when(i == 0)
    def _():
        ybuf[...] = jnp.zeros_like(ybuf)
        fetch(0, 0, True)

    @pl.when(i + 1 < nt)
    def _():
        fetch(i + 1, 1 - sl, True)

    fetch(i, sl, False)
    iota_r = lax.broadcasted_iota(jnp.int32, (R_CAP, TS), 0)
    gt = jnp.zeros((R_CAP, TS), jnp.float32)
    for k in range(TOP_K):
        gt = jnp.where(iota_r == slot_ref[k:k + 1, :], gate_ref[k:k + 1, :], gt)
    moe = lax.dot_general(gt.astype(jnp.bfloat16), ybuf[sl].astype(jnp.bfloat16),
                          (((0,), (0,)), ((), ())), preferred_element_type=jnp.float32)
    o_ref[...] = _rms(x1_ref[...] + moe, gfin_ref[...])


def _combine(nvc, y_tiles, x1, slot_t, gate_t, g_final):
    t, d = x1.shape
    nt = t // TS
    grid_spec = pltpu.PrefetchScalarGridSpec(
        num_scalar_prefetch=1,
        grid=(nt,),
        in_specs=[
            pl.BlockSpec(memory_space=pl.ANY),
            pl.BlockSpec((TS, d), lambda i, nv: (i, 0)),
            pl.BlockSpec((TOP_K, TS), lambda i, nv: (0, i)),
            pl.BlockSpec((TOP_K, TS), lambda i, nv: (0, i)),
            pl.BlockSpec((1, d), lambda i, nv: (0, 0)),
        ],
        out_specs=pl.BlockSpec((TS, d), lambda i, nv: (i, 0)),
        scratch_shapes=[pltpu.VMEM((2, R_CAP, d), jnp.float32), pltpu.SemaphoreType.DMA((2,))],
    )
    return pl.pallas_call(
        _combine_kernel,
        grid_spec=grid_spec,
        out_shape=jax.ShapeDtypeStruct((t, d), jnp.float32),
        compiler_params=pltpu.CompilerParams(
            dimension_semantics=("arbitrary",), vmem_limit_bytes=VMEM_LIMIT),
        name="combine",
    )(nvc, y_tiles, x1, slot_t, gate_t, g_final)


def _chunk_plan(pcc, nb):
    nt = pcc.shape[0]
    run_start = jnp.cumsum(pcc, axis=1) - pcc
    tot = jnp.sum(pcc, axis=0)
    eblocks = (tot + CPB - 1) // CPB
    estart = (jnp.cumsum(eblocks) - eblocks) * CPB
    pcc_t = pcc.T
    gpos = (estart[:, None] + jnp.cumsum(pcc_t, axis=1) - pcc_t).reshape(-1)
    nc = pcc_t.reshape(-1)
    base = (jnp.arange(nt, dtype=jnp.int32)[None, :] * CPT + run_start.T).reshape(-1)
    g = jnp.arange(nb * CPB, dtype=jnp.int32)
    f = jnp.searchsorted(gpos, g, side="right").astype(jnp.int32) - 1
    j = g - gpos[f]
    valid = j < nc[f]
    src = jnp.where(valid, base[f] + j, 0).astype(jnp.int32)
    nvalid = jnp.sum(valid.reshape(nb, CPB), axis=1).astype(jnp.int32)
    bexp = (f.reshape(nb, CPB)[:, 0] // nt).astype(jnp.int32)
    return src, bexp, nvalid


def kernel(x, g_mix, w_in, b_in, sinks, w_dw, b_dw, ln_g, ln_b, g_attn_out, g_conv_out, w_out, b_out,
           g_ffn, w_router, b_router, w1, b1, w2, b2, g_final):
    bsz, seq, d = x.shape
    t = bsz * seq
    nt = t // TS
    depth = g_mix.shape[0]
    assert depth == 1, "combine fuses the final RMSNorm, so exactly one layer is supported"
    f32 = jnp.float32
    max_chunks = nt * ((TS * TOP_K + N_EXPERTS * (CHUNK - 1)) // CHUNK) + N_EXPERTS * (CPB - 1)
    nb = -(-max_chunks // CPB)
    ci = jnp.arange(2 * LANES)
    dei = (ci[:, None] == jnp.where(ci < LANES, 2 * ci, 2 * (ci - LANES) + 1)[None, :]).astype(jnp.bfloat16)

    for l in range(depth):
        sink_cols = jnp.repeat(sinks[l].astype(f32).reshape(N_KV_HEADS, Q_PER_KV), WINDOW, axis=1)[..., None]
        x1 = _mix(x, g_mix[l][None], w_in[l].astype(jnp.bfloat16), b_in[l][None], sink_cols,
                  w_dw[l], b_dw[l][None], ln_g[l][None], ln_b[l][None],
                  g_attn_out[l][None], g_conv_out[l][None], w_out[l].astype(jnp.bfloat16), b_out[l][None])
        x1 = x1.reshape(t, d)
        xs, slot_t, gate_t, pcc = _route(x1, g_ffn[l][None], w_router[l].T, b_router[l][:, None])
        pcc = pcc[:, :, 0]
        src, bexp, nvalid = _chunk_plan(pcc, nb)
        b1l = b1[l].reshape(N_EXPERTS, 1, -1, 2)
        nvc = jnp.sum(pcc, axis=1).astype(jnp.int32)
        y = _experts(src, bexp, nvalid, nvc, xs.reshape(nt * CPT, CHUNK, d), w1[l], b1l[..., 0], b1l[..., 1],
                     w2[l], b2[l][:, None, :], dei)
        x = _combine(nvc, y.reshape(nt, R_CAP, d), x1, slot_t, gate_t, g_final[None]).reshape(bsz, seq, d)
    return x
```

```python
import functools

import jax
import jax.numpy as jnp
from jax import lax
from jax.experimental import pallas as pl
from jax.experimental.pallas import tpu as pltpu

HEAD_DIM = 64
N_Q_HEADS = 8
N_KV_HEADS = 2
Q_PER_KV = N_Q_HEADS // N_KV_HEADS
ATTN_WIDTH = N_Q_HEADS * HEAD_DIM
KV_WIDTH = N_KV_HEADS * HEAD_DIM
WINDOW = 128
CONV_KERNEL = 31
N_EXPERTS = 32
TOP_K = 4
SWIGLU_LIMIT = 7.0
SWIGLU_ALPHA = 1.702
RMS_EPS = 1e-5
LN_EPS = 1e-5

LANES = 128
SUBLANES = 8

TS = 256
CHUNK = SUBLANES
R_CAP = -(-(TS * TOP_K + N_EXPERTS * (CHUNK - 1)) // LANES) * LANES
CPT = R_CAP // CHUNK
TAIL_MAX = CPT - TS * TOP_K // CHUNK
_TAIL_PIECES = tuple(1 << i for i in reversed(range(TAIL_MAX.bit_length())))
TM = 256
CPB = TM // CHUNK
CONV_HALO = 32
CONV_ROWS = 64
NEG_BIG = -1e30
VMEM_LIMIT = 56 * 1024 * 1024


def _rms(x, g):
    return x * lax.rsqrt(jnp.mean(x * x, axis=-1, keepdims=True) + RMS_EPS) * g


def _dot(a, b):
    return jnp.dot(a, b, preferred_element_type=jnp.float32)


def _dot_nt(a, b, precision=None):
    return lax.dot_general(a, b, (((1,), (1,)), ((), ())), precision=precision,
                           preferred_element_type=jnp.float32)


def _mix_kernel(x_ref, gmix_ref, win_ref, bin_ref, sink_ref, wdw_ref, bdw_ref, lng_ref, lnb_ref,
                ga_ref, gc_ref, wout_ref, bout_ref, o_ref, kv_prev, u_buf, conv_buf):
    s = pl.program_id(1)
    first = s == 0

    @pl.when(first)
    def _():
        kv_prev[...] = jnp.zeros_like(kv_prev)
        u_buf[0:CONV_HALO, :] = jnp.zeros((CONV_HALO, u_buf.shape[1]), jnp.float32)

    x = x_ref[0]
    u = _rms(x, gmix_ref[...])
    proj = _dot(u.astype(jnp.bfloat16), win_ref[...]) + bin_ref[...]
    o_q, o_k, o_v, o_a, o_g = 0, ATTN_WIDTH, ATTN_WIDTH + KV_WIDTH, ATTN_WIDTH + 2 * KV_WIDTH, \
        ATTN_WIDTH + 2 * KV_WIDTH + ATTN_WIDTH
    q = proj[:, o_q:o_k] * (HEAD_DIM ** -0.5)
    k = proj[:, o_k:o_v]
    v = proj[:, o_v:o_a]
    conv_a = proj[:, o_a:o_g]
    conv_gate = proj[:, o_g:]

    lane = lax.broadcasted_iota(jnp.int32, (WINDOW + TS, LANES), 1)
    low = lane < HEAD_DIM
    kfull = jnp.concatenate([kv_prev[:, 0:KV_WIDTH], k], axis=0)
    vfull = jnp.concatenate([kv_prev[:, KV_WIDTH:], v], axis=0)
    kroll = pltpu.roll(kfull, HEAD_DIM, axis=1)
    vroll = pltpu.roll(vfull, HEAD_DIM, axis=1)
    kk = [jnp.where(low, kfull, kroll).astype(jnp.bfloat16), jnp.where(low, kroll, kfull).astype(jnp.bfloat16)]
    vv = [jnp.where(low, vfull, vroll).astype(jnp.bfloat16), jnp.where(low, vroll, vfull).astype(jnp.bfloat16)]
    kv_prev[:, 0:KV_WIDTH] = k[TS - WINDOW:, :]
    kv_prev[:, KV_WIDTH:] = v[TS - WINDOW:, :]

    rows = Q_PER_KV * WINDOW
    qi = lax.broadcasted_iota(jnp.int32, (rows, 2 * WINDOW), 0) & (WINDOW - 1)
    kj = lax.broadcasted_iota(jnp.int32, (rows, 2 * WINDOW), 1)
    band = (kj > qi) & (kj <= qi + WINDOW)
    lane_q = lax.broadcasted_iota(jnp.int32, (WINDOW, LANES), 1)
    low_q = lane_q < HEAD_DIM
    ones_kv = jnp.ones((2 * WINDOW, LANES), jnp.bfloat16)

    attn_blocks = []
    for b in range(TS // WINDOW):
        qb = q[b * WINDOW:(b + 1) * WINDOW, :]
        valid = band & ((kj >= WINDOW) | jnp.logical_not(first & (b == 0)))
        head_out = []
        for g in range(N_KV_HEADS):
            parts = []
            for i in range(Q_PER_KV):
                h = g * Q_PER_KV + i
                q128 = qb[:, (h // 2) * LANES:(h // 2 + 1) * LANES]
                keep = low_q if h % 2 == 0 else jnp.logical_not(low_q)
                parts.append(jnp.where(keep, q128, 0.0))
            qs = jnp.concatenate(parts, axis=0).astype(jnp.bfloat16)
            kb = kk[g][b * WINDOW:b * WINDOW + 2 * WINDOW, :]
            vb = vv[g][b * WINDOW:b * WINDOW + 2 * WINDOW, :]
            sc = _dot_nt(qs, kb)
            sc = jnp.where(valid, sc, NEG_BIG)
            sink = sink_ref[g]
            m = jnp.maximum(jnp.max(sc, axis=-1, keepdims=True), sink)
            p = jnp.exp(sc - m).astype(jnp.bfloat16)
            den = _dot(p, ones_kv) + jnp.exp(sink - m)
            pv = _dot(p, vb)
            o = pv / den
            for i in range(Q_PER_KV):
                head_out.append(o[i * WINDOW:(i + 1) * WINDOW, :])
        cols = [jnp.where(low_q, head_out[2 * j], head_out[2 * j + 1]) for j in range(N_Q_HEADS // 2)]
        attn_blocks.append(jnp.concatenate(cols, axis=1))
    attn = jnp.concatenate(attn_blocks, axis=0)

    cw = u_buf.shape[1]
    u_buf[CONV_HALO:CONV_HALO + TS, :] = conv_a * jax.nn.sigmoid(conv_gate)
    shift = CONV_HALO - (CONV_KERNEL - 1)
    for c in range(cw // LANES):
        cs = slice(c * LANES, (c + 1) * LANES)
        for r0 in range(0, TS, CONV_ROWS):
            acc = jnp.zeros((CONV_ROWS, LANES), jnp.float32)
            for j in range(CONV_KERNEL):
                acc = acc + wdw_ref[j:j + 1, cs] * u_buf[r0 + shift + j:r0 + shift + j + CONV_ROWS, cs]
            conv_buf[r0:r0 + CONV_ROWS, cs] = acc
    u_buf[0:CONV_HALO, :] = u_buf[TS:TS + CONV_HALO, :]
    cv = conv_buf[...] + bdw_ref[...]
    mu = jnp.mean(cv, axis=-1, keepdims=True)
    xc = cv - mu
    cv = xc * lax.rsqrt(jnp.mean(xc * xc, axis=-1, keepdims=True) + LN_EPS) * lng_ref[...] + lnb_ref[...]
    cv = cv * jax.nn.sigmoid(cv)

    mixed = jnp.concatenate([_rms(attn, ga_ref[...]), _rms(cv, gc_ref[...])], axis=1).astype(jnp.bfloat16)
    o_ref[0] = x + _dot(mixed, wout_ref[...]) + bout_ref[...]


def _mix(x, g_mix, w_in, b_in, sink_cols, w_dw, b_dw, ln_g, ln_b, g_a, g_c, w_out, b_out):
    bsz, seq, d = x.shape
    assert seq % TS == 0 and TS % WINDOW == 0 and CONV_HALO >= CONV_KERNEL - 1
    cw = w_dw.shape[1]
    const = lambda shape: pl.BlockSpec(shape, lambda b, s: (0,) * len(shape))
    return pl.pallas_call(
        _mix_kernel,
        grid=(bsz, seq // TS),
        in_specs=[
            pl.BlockSpec((1, TS, d), lambda b, s: (b, s, 0)),
            const((1, d)), const(w_in.shape), const((1, w_in.shape[1])),
            const(sink_cols.shape), const(w_dw.shape), const((1, cw)), const((1, cw)), const((1, cw)),
            const((1, ATTN_WIDTH)), const((1, cw)), const(w_out.shape), const((1, d)),
        ],
        out_specs=pl.BlockSpec((1, TS, d), lambda b, s: (b, s, 0)),
        out_shape=jax.ShapeDtypeStruct(x.shape, jnp.float32),
        scratch_shapes=[
            pltpu.VMEM((WINDOW, 2 * KV_WIDTH), jnp.float32),
            pltpu.VMEM((CONV_HALO + TS, cw), jnp.float32),
            pltpu.VMEM((TS, cw), jnp.float32),
        ],
        compiler_params=pltpu.CompilerParams(
            dimension_semantics=("arbitrary", "arbitrary"), vmem_limit_bytes=VMEM_LIMIT),
        name="mix",
    )(x, g_mix, w_in, b_in, sink_cols, w_dw, b_dw, ln_g, ln_b, g_a, g_c, w_out, b_out)


def _route_kernel(x1_ref, gffn_ref, wrt_ref, brt_ref, xs_ref, slot_ref, gate_ref, pcc_ref):
    h = _rms(x1_ref[...], gffn_ref[...])
    lg = _dot_nt(wrt_ref[...], h, precision=lax.Precision.HIGHEST) + brt_ref[...]
    iota_e = lax.broadcasted_iota(jnp.int32, (N_EXPERTS, TS), 0)
    vals, hots = [], []
    member = jnp.zeros((N_EXPERTS, TS), jnp.float32)
    for _ in range(TOP_K):
        mx = jnp.max(lg, axis=0, keepdims=True)
        idx = jnp.min(jnp.where(lg == mx, iota_e, N_EXPERTS), axis=0, keepdims=True)
        hot = iota_e == idx
        lg = jnp.where(hot, -jnp.inf, lg)
        member = member + hot.astype(jnp.float32)
        vals.append(mx)
        hots.append(hot)
    ex = [jnp.exp(vk - vals[0]) for vk in vals]
    den = ex[0] + ex[1] + ex[2] + ex[3]
    gate_ref[...] = jnp.concatenate([e / den for e in ex], axis=0)

    ti = lax.broadcasted_iota(jnp.int32, (TS, TS), 0)
    tj = lax.broadcasted_iota(jnp.int32, (TS, TS), 1)
    upper = jnp.where(ti < tj, 1.0, 0.0).astype(jnp.bfloat16)
    cum = _dot(member.astype(jnp.bfloat16), upper)
    cnt = jnp.sum(member, axis=1, keepdims=True)
    pcc = jnp.floor((cnt + (CHUNK - 1)) * (1.0 / CHUNK))
    pcb = jnp.broadcast_to(pcc, (N_EXPERTS, LANES))
    row_e = lax.broadcasted_iota(jnp.int32, (N_EXPERTS, LANES), 0)
    inc = pcb
    sh = 1
    while sh < N_EXPERTS:
        inc = inc + jnp.where(row_e >= sh, pltpu.roll(inc, sh, axis=0), 0.0)
        sh *= 2
    run_start = (inc - pcb)[:, 0:1] * CHUNK
    pcc_ref[0] = pcb.astype(jnp.int32)

    pos = run_start + cum
    slots = [jnp.sum(jnp.where(hot, pos, 0.0), axis=0, keepdims=True).astype(jnp.int32) for hot in hots]
    slot_ref[...] = jnp.concatenate(slots, axis=0)

    iota_r = lax.broadcasted_iota(jnp.int32, (R_CAP, TS), 0)
    sel = (iota_r == slots[0]) | (iota_r == slots[1]) | (iota_r == slots[2]) | (iota_r == slots[3])
    perm = jnp.where(sel, 1.0, 0.0).astype(jnp.bfloat16)
    xs_ref[0] = _dot(perm, h.astype(jnp.bfloat16))


def _route(x1, g_ffn, wr_t, br_t):
    t, d = x1.shape
    nt = t // TS
    const = lambda shape: pl.BlockSpec(shape, lambda i: (0,) * len(shape))
    return pl.pallas_call(
        _route_kernel,
        grid=(nt,),
        in_specs=[pl.BlockSpec((TS, d), lambda i: (i, 0)), const((1, d)), const(wr_t.shape), const(br_t.shape)],
        out_specs=[
            pl.BlockSpec((1, R_CAP, d), lambda i: (i, 0, 0)),
            pl.BlockSpec((TOP_K, TS), lambda i: (0, i)),
            pl.BlockSpec((TOP_K, TS), lambda i: (0, i)),
            pl.BlockSpec((1, N_EXPERTS, LANES), lambda i: (i, 0, 0)),
        ],
        out_shape=[
            jax.ShapeDtypeStruct((nt, R_CAP, d), jnp.float32),
            jax.ShapeDtypeStruct((TOP_K, t), jnp.int32),
            jax.ShapeDtypeStruct((TOP_K, t), jnp.float32),
            jax.ShapeDtypeStruct((nt, N_EXPERTS, LANES), jnp.int32),
        ],
        compiler_params=pltpu.CompilerParams(
            dimension_semantics=("arbitrary",), vmem_limit_bytes=VMEM_LIMIT),
        name="route",
    )(x1, g_ffn, wr_t, br_t)


def _experts_kernel(src_ref, bexp_ref, nval_ref, nvc_ref,
                    xs_hbm, w1_ref, b1g_ref, b1l_ref, w2_ref, b2_ref, dei_ref,
                    y_hbm,
                    xbuf, ybuf, zbuf, w1g, w1l, w2b, sem_in, sem_out, sem_z):
    b = pl.program_id(0)
    nb = pl.num_programs(0)
    n_tiles = nvc_ref.shape[0]
    slot = b % 2

    def zero_tail(tile, start):
        n = CPT - nvc_ref[tile]
        off = tile * CPT + nvc_ref[tile]
        for pc in _TAIL_PIECES:
            take = (n & pc) != 0

            @pl.when(take)
            def _(off=off, pc=pc):
                cp = pltpu.make_async_copy(zbuf.at[pl.ds(0, pc)], y_hbm.at[pl.ds(off, pc)], sem_z)
                if start:
                    cp.start()
                else:
                    cp.wait()
            off = off + jnp.where(take, pc, 0)

    @pl.when(b == 0)
    def _():
        zbuf[...] = jnp.zeros_like(zbuf)

    @pl.when(b < n_tiles)
    def _():
        zero_tail(b, True)

    def gather(blk, sl, start):
        def body(c, carry):
            cp = pltpu.make_async_copy(xs_hbm.at[src_ref[blk * CPB + c]],
                                       xbuf.at[sl, pl.ds(c * CHUNK, CHUNK)], sem_in.at[sl])
            if start:
                cp.start()
            else:
                cp.wait()
            return carry
        lax.fori_loop(0, nval_ref[blk], body, 0)

    def scatter(blk, sl, start):
        def body(c, carry):
            cp = pltpu.make_async_copy(ybuf.at[sl, pl.ds(c * CHUNK, CHUNK)],
                                       y_hbm.at[src_ref[blk * CPB + c]], sem_out.at[sl])
            if start:
                cp.start()
            else:
                cp.wait()
            return carry
        lax.fori_loop(0, nval_ref[blk], body, 0)

    @pl.when(b == 0)
    def _():
        xbuf[...] = jnp.zeros_like(xbuf)
        gather(0, 0, True)

    @pl.when(b + 1 < nb)
    def _():
        gather(b + 1, 1 - slot, True)

    prev_e = bexp_ref[jnp.maximum(b - 1, 0)]

    @pl.when((b == 0) | (bexp_ref[b] != prev_e))
    def _():
        dei = dei_ref[...]
        for j in range(w1_ref.shape[2] // (2 * LANES)):
            wj = w1_ref[0, :, j * 2 * LANES:(j + 1) * 2 * LANES].astype(jnp.bfloat16)
            r = _dot(wj, dei)
            w1g[:, j * LANES:(j + 1) * LANES] = r[:, 0:LANES].astype(jnp.bfloat16)
            w1l[:, j * LANES:(j + 1) * LANES] = r[:, LANES:].astype(jnp.bfloat16)
        w2b[...] = w2_ref[0].astype(jnp.bfloat16)

    @pl.when(b >= 2)
    def _():
        scatter(b - 2, slot, False)

    @pl.when(nval_ref[b] > 0)
    def _():
        gather(b, slot, False)
        xb = xbuf[slot].astype(jnp.bfloat16)
        hg = _dot(xb, w1g[...]) + b1g_ref[0]
        hl = _dot(xb, w1l[...]) + b1l_ref[0]
        hg = jnp.minimum(hg, SWIGLU_LIMIT)
        hl = jnp.clip(hl, -SWIGLU_LIMIT, SWIGLU_LIMIT)
        act = hg * jax.nn.sigmoid(SWIGLU_ALPHA * hg) * (hl + 1.0)
        ybuf[slot] = _dot(act.astype(jnp.bfloat16), w2b[...]) + b2_ref[0]
        scatter(b, slot, True)

    @pl.when(b < n_tiles)
    def _():
        zero_tail(b, False)

    @pl.when(b == nb - 1)
    def _():
        scatter(b, slot, False)

        @pl.when(b >= 1)
        def _():
            scatter(b - 1, 1 - slot, False)


def _experts(chunk_src, block_expert, block_nvalid, nvc, xs_chunks, w1, b1g, b1l, w2, b2, dei):
    n_chunks, _, d = xs_chunks.shape
    nb = block_expert.shape[0]
    assert nb >= nvc.shape[0]
    de2 = w1.shape[2]
    grid_spec = pltpu.PrefetchScalarGridSpec(
        num_scalar_prefetch=4,
        grid=(nb,),
        in_specs=[
            pl.BlockSpec(memory_space=pl.ANY),
            pl.BlockSpec((1, d, de2), lambda i, src, be, nv, nc: (be[i], 0, 0)),
            pl.BlockSpec((1, 1, de2 // 2), lambda i, src, be, nv, nc: (be[i], 0, 0)),
            pl.BlockSpec((1, 1, de2 // 2), lambda i, src, be, nv, nc: (be[i], 0, 0)),
            pl.BlockSpec((1, de2 // 2, d), lambda i, src, be, nv, nc: (be[i], 0, 0)),
            pl.BlockSpec((1, 1, d), lambda i, src, be, nv, nc: (be[i], 0, 0)),
            pl.BlockSpec(dei.shape, lambda i, src, be, nv, nc: (0, 0)),
        ],
        out_specs=pl.BlockSpec(memory_space=pl.ANY),
        scratch_shapes=[
            pltpu.VMEM((2, TM, d), jnp.float32),
            pltpu.VMEM((2, TM, d), jnp.float32),
            pltpu.VMEM((TAIL_MAX, CHUNK, d), jnp.float32),
            pltpu.VMEM((d, de2 // 2), jnp.bfloat16),
            pltpu.VMEM((d, de2 // 2), jnp.bfloat16),
            pltpu.VMEM((de2 // 2, d), jnp.bfloat16),
            pltpu.SemaphoreType.DMA((2,)),
            pltpu.SemaphoreType.DMA((2,)),
            pltpu.SemaphoreType.DMA(()),
        ],
    )
    return pl.pallas_call(
        _experts_kernel,
        grid_spec=grid_spec,
        out_shape=jax.ShapeDtypeStruct(xs_chunks.shape, jnp.float32),
        compiler_params=pltpu.CompilerParams(
            dimension_semantics=("arbitrary",), vmem_limit_bytes=VMEM_LIMIT),
        name="experts",
    )(chunk_src, block_expert, block_nvalid, nvc, xs_chunks, w1, b1g, b1l, w2, b2, dei)


_PIECES = tuple(1 << i for i in reversed(range((CPT).bit_length())))


def _combine_kernel(nvc_ref,
                    y_hbm, x1_ref, slot_ref, gate_ref, gfin_ref, o_ref, ybuf, sem):
    i = pl.program_id(0)
    nt = pl.num_programs(0)
    sl = i % 2

    def fetch(tile, s_, start):
        n = nvc_ref[tile]
        off = jnp.int32(0)
        for pc in _PIECES:
            if pc * CHUNK > R_CAP:
                continue
            take = (n & pc) != 0

            @pl.when(take)
            def _(off=off, pc=pc):
                cp = pltpu.make_async_copy(
                    y_hbm.at[tile, pl.ds(pl.multiple_of(off * CHUNK, CHUNK), pc * CHUNK)],
                    ybuf.at[s_, pl.ds(pl.multiple_of(off * CHUNK, CHUNK), pc * CHUNK)], sem.at[s_])
                if start:
                    cp.start()
                else:
                    cp.wait()
            off = off + jnp.where(take, pc, 0)

    @pl.when(i == 0)
    def _():
        ybuf[...] = jnp.zeros_like(ybuf)
        fetch(0, 0, True)

    @pl.when(i + 1 < nt)
    def _():
        fetch(i + 1, 1 - sl, True)

    fetch(i, sl, False)
    iota_r = lax.broadcasted_iota(jnp.int32, (R_CAP, TS), 0)
    gt = jnp.zeros((R_CAP, TS), jnp.float32)
    for k in range(TOP_K):
        gt = jnp.where(iota_r == slot_ref[k:k + 1, :], gate_ref[k:k + 1, :], gt)
    moe = lax.dot_general(gt.astype(jnp.bfloat16), ybuf[sl].astype(jnp.bfloat16),
                          (((0,), (0,)), ((), ())), preferred_element_type=jnp.float32)
    o_ref[...] = _rms(x1_ref[...] + moe, gfin_ref[...])


def _combine(nvc, y_tiles, x1, slot_t, gate_t, g_final):
    t, d = x1.shape
    nt = t // TS
    grid_spec = pltpu.PrefetchScalarGridSpec(
        num_scalar_prefetch=1,
        grid=(nt,),
        in_specs=[
            pl.BlockSpec(memory_space=pl.ANY),
            pl.BlockSpec((TS, d), lambda i, nv: (i, 0)),
            pl.BlockSpec((TOP_K, TS), lambda i, nv: (0, i)),
            pl.BlockSpec((TOP_K, TS), lambda i, nv: (0, i)),
            pl.BlockSpec((1, d), lambda i, nv: (0, 0)),
        ],
        out_specs=pl.BlockSpec((TS, d), lambda i, nv: (i, 0)),
        scratch_shapes=[pltpu.VMEM((2, R_CAP, d), jnp.float32), pltpu.SemaphoreType.DMA((2,))],
    )
    return pl.pallas_call(
        _combine_kernel,
        grid_spec=grid_spec,
        out_shape=jax.ShapeDtypeStruct((t, d), jnp.float32),
        compiler_params=pltpu.CompilerParams(
            dimension_semantics=("arbitrary",), vmem_limit_bytes=VMEM_LIMIT),
        name="combine",
    )(nvc, y_tiles, x1, slot_t, gate_t, g_final)


def _chunk_plan(pcc, nb):
    nt = pcc.shape[0]
    i32 = jnp.int32
    pcc_t = pcc.T
    run_start_t = (jnp.cumsum(pcc, axis=1) - pcc).T
    cum_incl = jnp.cumsum(pcc_t, axis=1)
    eblocks = (cum_incl[:, -1] + CPB - 1) // CPB
    bstart = jnp.cumsum(eblocks) - eblocks
    blk = jnp.arange(nb, dtype=i32)
    bexp = jnp.sum(bstart[None, :] <= blk[:, None], axis=1).astype(i32) - 1
    oh = bexp[:, None] == jnp.arange(N_EXPERTS, dtype=i32)[None, :]
    pick = lambda tab: jnp.sum(jnp.where(oh[:, :, None], tab[None], 0), axis=1)
    ci, pc_row, rs_row = pick(cum_incl), pick(pcc_t), pick(run_start_t)
    b0 = jnp.sum(jnp.where(oh, bstart[None, :], 0), axis=1)
    p = (blk - b0)[:, None] * CPB + jnp.arange(CPB, dtype=i32)[None, :]
    before = ci[:, None, :] <= p[:, :, None]
    tile = jnp.sum(before, axis=2).astype(i32)
    cum_excl = jnp.sum(jnp.where(before, pc_row[:, None, :], 0), axis=2)
    at_tile = jnp.arange(nt, dtype=i32)[None, None, :] == tile[:, :, None]
    rs = jnp.sum(jnp.where(at_tile, rs_row[:, None, :], 0), axis=2)
    valid = tile < nt
    src = jnp.where(valid, tile * CPT + rs + (p - cum_excl), 0).astype(i32).reshape(-1)
    nvalid = jnp.sum(valid, axis=1).astype(i32)
    return src, bexp, nvalid


def kernel(x, g_mix, w_in, b_in, sinks, w_dw, b_dw, ln_g, ln_b, g_attn_out, g_conv_out, w_out, b_out,
           g_ffn, w_router, b_router, w1, b1, w2, b2, g_final):
    bsz, seq, d = x.shape
    t = bsz * seq
    nt = t // TS
    depth = g_mix.shape[0]
    assert depth == 1, "combine fuses the final RMSNorm, so exactly one layer is supported"
    f32 = jnp.float32
    max_chunks = nt * ((TS * TOP_K + N_EXPERTS * (CHUNK - 1)) // CHUNK) + N_EXPERTS * (CPB - 1)
    nb = -(-max_chunks // CPB)
    ci = jnp.arange(2 * LANES)
    dei = (ci[:, None] == jnp.where(ci < LANES, 2 * ci, 2 * (ci - LANES) + 1)[None, :]).astype(jnp.bfloat16)

    for l in range(depth):
        sink_cols = jnp.repeat(sinks[l].astype(f32).reshape(N_KV_HEADS, Q_PER_KV), WINDOW, axis=1)[..., None]
        x1 = _mix(x, g_mix[l][None], w_in[l].astype(jnp.bfloat16), b_in[l][None], sink_cols,
                  w_dw[l], b_dw[l][None], ln_g[l][None], ln_b[l][None],
                  g_attn_out[l][None], g_conv_out[l][None], w_out[l].astype(jnp.bfloat16), b_out[l][None])
        x1 = x1.reshape(t, d)
        xs, slot_t, gate_t, pcc = _route(x1, g_ffn[l][None], w_router[l].T, b_router[l][:, None])
        pcc = pcc[:, :, 0]
        src, bexp, nvalid = _chunk_plan(pcc, nb)
        b1l = b1[l].reshape(N_EXPERTS, 1, -1, 2)
        nvc = jnp.sum(pcc, axis=1).astype(jnp.int32)
        y = _experts(src, bexp, nvalid, nvc, xs.reshape(nt * CPT, CHUNK, d), w1[l], b1l[..., 0], b1l[..., 1],
                     w2[l], b2[l][:, None, :], dei)
        x = _combine(nvc, y.reshape(nt, R_CAP, d), x1, slot_t, gate_t, g_final[None]).reshape(bsz, seq, d)
    return x
```

```python
import functools

import jax
import jax.numpy as jnp
from jax import lax
from jax.experimental import pallas as pl
from jax.experimental.pallas import tpu as pltpu

HEAD_DIM = 64
N_Q_HEADS = 8
N_KV_HEADS = 2
Q_PER_KV = N_Q_HEADS // N_KV_HEADS
ATTN_WIDTH = N_Q_HEADS * HEAD_DIM
KV_WIDTH = N_KV_HEADS * HEAD_DIM
WINDOW = 128
CONV_KERNEL = 31
N_EXPERTS = 32
TOP_K = 4
SWIGLU_LIMIT = 7.0
SWIGLU_ALPHA = 1.702
RMS_EPS = 1e-5
LN_EPS = 1e-5

LANES = 128
SUBLANES = 8

TS = 256
CHUNK = SUBLANES
R_CAP = -(-(TS * TOP_K + N_EXPERTS * (CHUNK - 1)) // LANES) * LANES
CPT = R_CAP // CHUNK
TAIL_MAX = CPT - TS * TOP_K // CHUNK
_TAIL_PIECES = tuple(1 << i for i in reversed(range(TAIL_MAX.bit_length())))
TM = 256
CPB = TM // CHUNK
CONV_HALO = 32
CONV_ROWS = 64
NEG_BIG = -1e30
VMEM_LIMIT = 56 * 1024 * 1024


def _rms(x, g):
    return x * lax.rsqrt(jnp.mean(x * x, axis=-1, keepdims=True) + RMS_EPS) * g


def _dot(a, b):
    return jnp.dot(a, b, preferred_element_type=jnp.float32)


def _dot_nt(a, b, precision=None):
    return lax.dot_general(a, b, (((1,), (1,)), ((), ())), precision=precision,
                           preferred_element_type=jnp.float32)


def _mix_kernel(x_ref, gmix_ref, win_ref, bin_ref, sink_ref, wdw_ref, bdw_ref, lng_ref, lnb_ref,
                ga_ref, gc_ref, wout_ref, bout_ref, o_ref, kv_prev, u_buf, conv_buf):
    s = pl.program_id(1)
    first = s == 0

    @pl.when(first)
    def _():
        kv_prev[...] = jnp.zeros_like(kv_prev)
        u_buf[0:CONV_HALO, :] = jnp.zeros((CONV_HALO, u_buf.shape[1]), jnp.float32)

    x = x_ref[0]
    u = _rms(x, gmix_ref[...])
    proj = _dot(u.astype(jnp.bfloat16), win_ref[...]) + bin_ref[...]
    o_q, o_k, o_v, o_a, o_g = 0, ATTN_WIDTH, ATTN_WIDTH + KV_WIDTH, ATTN_WIDTH + 2 * KV_WIDTH, \
        ATTN_WIDTH + 2 * KV_WIDTH + ATTN_WIDTH
    q = proj[:, o_q:o_k] * (HEAD_DIM ** -0.5)
    k = proj[:, o_k:o_v]
    v = proj[:, o_v:o_a]
    conv_a = proj[:, o_a:o_g]
    conv_gate = proj[:, o_g:]

    lane = lax.broadcasted_iota(jnp.int32, (WINDOW + TS, LANES), 1)
    low = lane < HEAD_DIM
    kfull = jnp.concatenate([kv_prev[:, 0:KV_WIDTH], k], axis=0)
    vfull = jnp.concatenate([kv_prev[:, KV_WIDTH:], v], axis=0)
    kroll = pltpu.roll(kfull, HEAD_DIM, axis=1)
    vroll = pltpu.roll(vfull, HEAD_DIM, axis=1)
    kk = [jnp.where(low, kfull, kroll).astype(jnp.bfloat16), jnp.where(low, kroll, kfull).astype(jnp.bfloat16)]
    vv = [jnp.where(low, vfull, vroll).astype(jnp.bfloat16), jnp.where(low, vroll, vfull).astype(jnp.bfloat16)]
    kv_prev[:, 0:KV_WIDTH] = k[TS - WINDOW:, :]
    kv_prev[:, KV_WIDTH:] = v[TS - WINDOW:, :]

    rows = Q_PER_KV * WINDOW
    qi = lax.broadcasted_iota(jnp.int32, (rows, 2 * WINDOW), 0) & (WINDOW - 1)
    kj = lax.broadcasted_iota(jnp.int32, (rows, 2 * WINDOW), 1)
    band = (kj > qi) & (kj <= qi + WINDOW)
    lane_q = lax.broadcasted_iota(jnp.int32, (WINDOW, LANES), 1)
    low_q = lane_q < HEAD_DIM
    ones_kv = jnp.ones((2 * WINDOW, LANES), jnp.bfloat16)

    attn_blocks = []
    for b in range(TS // WINDOW):
        qb = q[b * WINDOW:(b + 1) * WINDOW, :]
        valid = band & ((kj >= WINDOW) | jnp.logical_not(first & (b == 0)))
        head_out = []
        for g in range(N_KV_HEADS):
            parts = []
            for i in range(Q_PER_KV):
                h = g * Q_PER_KV + i
                q128 = qb[:, (h // 2) * LANES:(h // 2 + 1) * LANES]
                keep = low_q if h % 2 == 0 else jnp.logical_not(low_q)
                parts.append(jnp.where(keep, q128, 0.0))
            qs = jnp.concatenate(parts, axis=0).astype(jnp.bfloat16)
            kb = kk[g][b * WINDOW:b * WINDOW + 2 * WINDOW, :]
            vb = vv[g][b * WINDOW:b * WINDOW + 2 * WINDOW, :]
            sc = _dot_nt(qs, kb)
            sc = jnp.where(valid, sc, NEG_BIG)
            sink = sink_ref[g]
            m = jnp.maximum(jnp.max(sc, axis=-1, keepdims=True), sink)
            p = jnp.exp(sc - m).astype(jnp.bfloat16)
            den = _dot(p, ones_kv) + jnp.exp(sink - m)
            pv = _dot(p, vb)
            o = pv / den
            for i in range(Q_PER_KV):
                head_out.append(o[i * WINDOW:(i + 1) * WINDOW, :])
        cols = [jnp.where(low_q, head_out[2 * j], head_out[2 * j + 1]) for j in range(N_Q_HEADS // 2)]
        attn_blocks.append(jnp.concatenate(cols, axis=1))
    attn = jnp.concatenate(attn_blocks, axis=0)

    cw = u_buf.shape[1]
    u_buf[CONV_HALO:CONV_HALO + TS, :] = conv_a * jax.nn.sigmoid(conv_gate)
    shift = CONV_HALO - (CONV_KERNEL - 1)
    for c in range(cw // LANES):
        cs = slice(c * LANES, (c + 1) * LANES)
        for r0 in range(0, TS, CONV_ROWS):
            acc = jnp.zeros((CONV_ROWS, LANES), jnp.float32)
            for j in range(CONV_KERNEL):
                acc = acc + wdw_ref[j:j + 1, cs] * u_buf[r0 + shift + j:r0 + shift + j + CONV_ROWS, cs]
            conv_buf[r0:r0 + CONV_ROWS, cs] = acc
    u_buf[0:CONV_HALO, :] = u_buf[TS:TS + CONV_HALO, :]
    cv = conv_buf[...] + bdw_ref[...]
    mu = jnp.mean(cv, axis=-1, keepdims=True)
    xc = cv - mu
    cv = xc * lax.rsqrt(jnp.mean(xc * xc, axis=-1, keepdims=True) + LN_EPS) * lng_ref[...] + lnb_ref[...]
    cv = cv * jax.nn.sigmoid(cv)

    mixed = jnp.concatenate([_rms(attn, ga_ref[...]), _rms(cv, gc_ref[...])], axis=1).astype(jnp.bfloat16)
    o_ref[0] = x + _dot(mixed, wout_ref[...]) + bout_ref[...]


def _mix(x, g_mix, w_in, b_in, sink_cols, w_dw, b_dw, ln_g, ln_b, g_a, g_c, w_out, b_out):
    bsz, seq, d = x.shape
    assert seq % TS == 0 and TS % WINDOW == 0 and CONV_HALO >= CONV_KERNEL - 1
    cw = w_dw.shape[1]
    const = lambda shape: pl.BlockSpec(shape, lambda b, s: (0,) * len(shape))
    return pl.pallas_call(
        _mix_kernel,
        grid=(bsz, seq // TS),
        in_specs=[
            pl.BlockSpec((1, TS, d), lambda b, s: (b, s, 0)),
            const((1, d)), const(w_in.shape), const((1, w_in.shape[1])),
            const(sink_cols.shape), const(w_dw.shape), const((1, cw)), const((1, cw)), const((1, cw)),
            const((1, ATTN_WIDTH)), const((1, cw)), const(w_out.shape), const((1, d)),
        ],
        out_specs=pl.BlockSpec((1, TS, d), lambda b, s: (b, s, 0)),
        out_shape=jax.ShapeDtypeStruct(x.shape, jnp.float32),
        scratch_shapes=[
            pltpu.VMEM((WINDOW, 2 * KV_WIDTH), jnp.float32),
            pltpu.VMEM((CONV_HALO + TS, cw), jnp.float32),
            pltpu.VMEM((TS, cw), jnp.float32),
        ],
        compiler_params=pltpu.CompilerParams(
            dimension_semantics=("arbitrary", "arbitrary"), vmem_limit_bytes=VMEM_LIMIT),
        name="mix",
    )(x, g_mix, w_in, b_in, sink_cols, w_dw, b_dw, ln_g, ln_b, g_a, g_c, w_out, b_out)


def _route_kernel(x1_ref, gffn_ref, wrt_ref, brt_ref, xs_ref, slot_ref, gate_ref, pcc_ref):
    h = _rms(x1_ref[...], gffn_ref[...])
    lg = _dot_nt(wrt_ref[...], h, precision=lax.Precision.HIGHEST) + brt_ref[...]
    iota_e = lax.broadcasted_iota(jnp.int32, (N_EXPERTS, TS), 0)
    vals, hots = [], []
    member = jnp.zeros((N_EXPERTS, TS), jnp.float32)
    for _ in range(TOP_K):
        mx = jnp.max(lg, axis=0, keepdims=True)
        idx = jnp.min(jnp.where(lg == mx, iota_e, N_EXPERTS), axis=0, keepdims=True)
        hot = iota_e == idx
        lg = jnp.where(hot, -jnp.inf, lg)
        member = member + hot.astype(jnp.float32)
        vals.append(mx)
        hots.append(hot)
    ex = [jnp.exp(vk - vals[0]) for vk in vals]
    den = ex[0] + ex[1] + ex[2] + ex[3]
    gate_ref[...] = jnp.concatenate([e / den for e in ex], axis=0)

    ti = lax.broadcasted_iota(jnp.int32, (TS, TS), 0)
    tj = lax.broadcasted_iota(jnp.int32, (TS, TS), 1)
    upper = jnp.where(ti < tj, 1.0, 0.0).astype(jnp.bfloat16)
    cum = _dot(member.astype(jnp.bfloat16), upper)
    cnt = jnp.sum(member, axis=1, keepdims=True)
    pcc = jnp.floor((cnt + (CHUNK - 1)) * (1.0 / CHUNK))
    pcb = jnp.broadcast_to(pcc, (N_EXPERTS, LANES))
    row_e = lax.broadcasted_iota(jnp.int32, (N_EXPERTS, LANES), 0)
    inc = pcb
    sh = 1
    while sh < N_EXPERTS:
        inc = inc + jnp.where(row_e >= sh, pltpu.roll(inc, sh, axis=0), 0.0)
        sh *= 2
    run_start = (inc - pcb)[:, 0:1] * CHUNK
    pcc_ref[0] = pcb.astype(jnp.int32)

    pos = run_start + cum
    slots = [jnp.sum(jnp.where(hot, pos, 0.0), axis=0, keepdims=True).astype(jnp.int32) for hot in hots]
    slot_ref[...] = jnp.concatenate(slots, axis=0)

    iota_r = lax.broadcasted_iota(jnp.int32, (R_CAP, TS), 0)
    sel = (iota_r == slots[0]) | (iota_r == slots[1]) | (iota_r == slots[2]) | (iota_r == slots[3])
    perm = jnp.where(sel, 1.0, 0.0).astype(jnp.bfloat16)
    xs_ref[0] = _dot(perm, h.astype(jnp.bfloat16))


def _route(x1, g_ffn, wr_t, br_t):
    t, d = x1.shape
    nt = t // TS
    const = lambda shape: pl.BlockSpec(shape, lambda i: (0,) * len(shape))
    return pl.pallas_call(
        _route_kernel,
        grid=(nt,),
        in_specs=[pl.BlockSpec((TS, d), lambda i: (i, 0)), const((1, d)), const(wr_t.shape), const(br_t.shape)],
        out_specs=[
            pl.BlockSpec((1, R_CAP, d), lambda i: (i, 0, 0)),
            pl.BlockSpec((TOP_K, TS), lambda i: (0, i)),
            pl.BlockSpec((TOP_K, TS), lambda i: (0, i)),
            pl.BlockSpec((1, N_EXPERTS, LANES), lambda i: (i, 0, 0)),
        ],
        out_shape=[
            jax.ShapeDtypeStruct((nt, R_CAP, d), jnp.float32),
            jax.ShapeDtypeStruct((TOP_K, t), jnp.int32),
            jax.ShapeDtypeStruct((TOP_K, t), jnp.float32),
            jax.ShapeDtypeStruct((nt, N_EXPERTS, LANES), jnp.int32),
        ],
        compiler_params=pltpu.CompilerParams(
            dimension_semantics=("arbitrary",), vmem_limit_bytes=VMEM_LIMIT),
        name="route",
    )(x1, g_ffn, wr_t, br_t)


def _experts_kernel(src_ref, bexp_ref, nval_ref, nvc_ref,
                    xs_hbm, w1_ref, b1g_ref, b1l_ref, w2_ref, b2_ref, dei_ref,
                    y_hbm,
                    xbuf, ybuf, zbuf, w1g, w1l, w2b, sem_in, sem_out, sem_z):
    b = pl.program_id(0)
    nb = pl.num_programs(0)
    n_tiles = nvc_ref.shape[0]
    slot = b % 2

    def zero_tail(tile, start):
        n = CPT - nvc_ref[tile]
        off = tile * CPT + nvc_ref[tile]
        for pc in _TAIL_PIECES:
            take = (n & pc) != 0

            @pl.when(take)
            def _(off=off, pc=pc):
                cp = pltpu.make_async_copy(zbuf.at[pl.ds(0, pc)], y_hbm.at[pl.ds(off, pc)], sem_z)
                if start:
                    cp.start()
                else:
                    cp.wait()
            off = off + jnp.where(take, pc, 0)

    @pl.when(b == 0)
    def _():
        zbuf[...] = jnp.zeros_like(zbuf)

    @pl.when(b < n_tiles)
    def _():
        zero_tail(b, True)

    def chunk_copy(blk, sl, c, inbound):
        if inbound:
            return pltpu.make_async_copy(xs_hbm.at[src_ref[blk * CPB + c]], xbuf.at[sl, c], sem_in.at[sl])
        return pltpu.make_async_copy(ybuf.at[sl, c], y_hbm.at[src_ref[blk * CPB + c]], sem_out.at[sl])

    def start_all(blk, sl, inbound, static):
        if static:
            for c in range(CPB):
                chunk_copy(blk, sl, c, inbound).start()
        else:
            def body(c, carry):
                chunk_copy(blk, sl, c, inbound).start()
                return carry
            lax.fori_loop(0, nval_ref[blk], body, 0)

    def wait_all(blk, sl, inbound):
        n = nval_ref[blk]

        @pl.when(n == CPB)
        def _():
            if inbound:
                pltpu.make_async_copy(xs_hbm.at[pl.ds(0, CPB)], xbuf.at[sl], sem_in.at[sl]).wait()
            else:
                pltpu.make_async_copy(ybuf.at[sl], y_hbm.at[pl.ds(0, CPB)], sem_out.at[sl]).wait()

        @pl.when(n != CPB)
        def _():
            def body(c, carry):
                chunk_copy(blk, sl, c, inbound).wait()
                return carry
            lax.fori_loop(0, n, body, 0)

    def compute():
        xb = xbuf[slot].reshape(TM, xbuf.shape[3]).astype(jnp.bfloat16)
        hg = _dot(xb, w1g[...]) + b1g_ref[0]
        hl = _dot(xb, w1l[...]) + b1l_ref[0]
        hg = jnp.minimum(hg, SWIGLU_LIMIT)
        hl = jnp.clip(hl, -SWIGLU_LIMIT, SWIGLU_LIMIT)
        act = hg * jax.nn.sigmoid(SWIGLU_ALPHA * hg) * (hl + 1.0)
        y = _dot(act.astype(jnp.bfloat16), w2b[...]) + b2_ref[0]
        ybuf[slot] = y.reshape(CPB, CHUNK, y.shape[1])

    @pl.when(b == 0)
    def _():
        xbuf[...] = jnp.zeros_like(xbuf)
        start_all(0, 0, True, False)

    prev_e = bexp_ref[jnp.maximum(b - 1, 0)]

    @pl.when((b == 0) | (bexp_ref[b] != prev_e))
    def _():
        dei = dei_ref[...]
        for j in range(w1_ref.shape[2] // (2 * LANES)):
            wj = w1_ref[0, :, j * 2 * LANES:(j + 1) * 2 * LANES].astype(jnp.bfloat16)
            r = _dot(wj, dei)
            w1g[:, j * LANES:(j + 1) * LANES] = r[:, 0:LANES].astype(jnp.bfloat16)
            w1l[:, j * LANES:(j + 1) * LANES] = r[:, LANES:].astype(jnp.bfloat16)
        w2b[...] = w2_ref[0].astype(jnp.bfloat16)

    @pl.when(b >= 2)
    def _():
        wait_all(b - 2, slot, False)

    wait_all(b, slot, True)

    prev_n = nval_ref[jnp.maximum(b - 1, 0)]
    next_n = nval_ref[jnp.minimum(b + 1, nb - 1)]
    fast = (b >= 1) & (prev_n == CPB) & (b + 1 < nb) & (next_n == CPB) & (nval_ref[b] > 0)

    @pl.when(fast)
    def _():
        start_all(b - 1, 1 - slot, False, True)
        start_all(b + 1, 1 - slot, True, True)
        compute()

    @pl.when(jnp.logical_not(fast))
    def _():
        @pl.when(b >= 1)
        def _():
            start_all(b - 1, 1 - slot, False, False)

        @pl.when(b + 1 < nb)
        def _():
            start_all(b + 1, 1 - slot, True, False)

        @pl.when(nval_ref[b] > 0)
        def _():
            compute()

    @pl.when(b < n_tiles)
    def _():
        zero_tail(b, False)

    @pl.when(b == nb - 1)
    def _():
        start_all(b, slot, False, False)

        @pl.when(b >= 1)
        def _():
            wait_all(b - 1, 1 - slot, False)
        wait_all(b, slot, False)


def _experts(chunk_src, block_expert, block_nvalid, nvc, xs_chunks, w1, b1g, b1l, w2, b2, dei):
    n_chunks, _, d = xs_chunks.shape
    nb = block_expert.shape[0]
    assert nb >= nvc.shape[0]
    de2 = w1.shape[2]
    grid_spec = pltpu.PrefetchScalarGridSpec(
        num_scalar_prefetch=4,
        grid=(nb,),
        in_specs=[
            pl.BlockSpec(memory_space=pl.ANY),
            pl.BlockSpec((1, d, de2), lambda i, src, be, nv, nc: (be[i], 0, 0)),
            pl.BlockSpec((1, 1, de2 // 2), lambda i, src, be, nv, nc: (be[i], 0, 0)),
            pl.BlockSpec((1, 1, de2 // 2), lambda i, src, be, nv, nc: (be[i], 0, 0)),
            pl.BlockSpec((1, de2 // 2, d), lambda i, src, be, nv, nc: (be[i], 0, 0)),
            pl.BlockSpec((1, 1, d), lambda i, src, be, nv, nc: (be[i], 0, 0)),
            pl.BlockSpec(dei.shape, lambda i, src, be, nv, nc: (0, 0)),
        ],
        out_specs=pl.BlockSpec(memory_space=pl.ANY),
        scratch_shapes=[
            pltpu.VMEM((2, CPB, CHUNK, d), jnp.float32),
            pltpu.VMEM((2, CPB, CHUNK, d), jnp.float32),
            pltpu.VMEM((TAIL_MAX, CHUNK, d), jnp.float32),
            pltpu.VMEM((d, de2 // 2), jnp.bfloat16),
            pltpu.VMEM((d, de2 // 2), jnp.bfloat16),
            pltpu.VMEM((de2 // 2, d), jnp.bfloat16),
            pltpu.SemaphoreType.DMA((2,)),
            pltpu.SemaphoreType.DMA((2,)),
            pltpu.SemaphoreType.DMA(()),
        ],
    )
    return pl.pallas_call(
        _experts_kernel,
        grid_spec=grid_spec,
        out_shape=jax.ShapeDtypeStruct(xs_chunks.shape, jnp.float32),
        compiler_params=pltpu.CompilerParams(
            dimension_semantics=("arbitrary",), vmem_limit_bytes=VMEM_LIMIT),
        name="experts",
    )(chunk_src, block_expert, block_nvalid, nvc, xs_chunks, w1, b1g, b1l, w2, b2, dei)


_PIECES = tuple(1 << i for i in reversed(range((CPT).bit_length())))


def _combine_kernel(nvc_ref,
                    y_hbm, x1_ref, slot_ref, gate_ref, gfin_ref, o_ref, ybuf, sem):
    i = pl.program_id(0)
    nt = pl.num_programs(0)
    sl = i % 2

    def fetch(tile, s_, start):
        n = nvc_ref[tile]
        off = jnp.int32(0)
        for pc in _PIECES:
            if pc * CHUNK > R_CAP:
                continue
            take = (n & pc) != 0

            @pl.when(take)
            def _(off=off, pc=pc):
                cp = pltpu.make_async_copy(
                    y_hbm.at[tile, pl.ds(pl.multiple_of(off * CHUNK, CHUNK), pc * CHUNK)],
                    ybuf.at[s_, pl.ds(pl.multiple_of(off * CHUNK, CHUNK), pc * CHUNK)], sem.at[s_])
                if start:
                    cp.start()
                else:
                    cp.wait()
            off = off + jnp.where(take, pc, 0)

    @pl.when(i == 0)
    def _():
        ybuf[...] = jnp.zeros_like(ybuf)
        fetch(0, 0, True)

    @pl.when(i + 1 < nt)
    def _():
        fetch(i + 1, 1 - sl, True)

    fetch(i, sl, False)
    iota_r = lax.broadcasted_iota(jnp.int32, (R_CAP, TS), 0)
    gt = jnp.zeros((R_CAP, TS), jnp.float32)
    for k in range(TOP_K):
        gt = jnp.where(iota_r == slot_ref[k:k + 1, :], gate_ref[k:k + 1, :], gt)
    moe = lax.dot_general(gt.astype(jnp.bfloat16), ybuf[sl].astype(jnp.bfloat16),
                          (((0,), (0,)), ((), ())), preferred_element_type=jnp.float32)
    o_ref[...] = _rms(x1_ref[...] + moe, gfin_ref[...])


def _combine(nvc, y_tiles, x1, slot_t, gate_t, g_final):
    t, d = x1.shape
    nt = t // TS
    grid_spec = pltpu.PrefetchScalarGridSpec(
        num_scalar_prefetch=1,
        grid=(nt,),
        in_specs=[
            pl.BlockSpec(memory_space=pl.ANY),
            pl.BlockSpec((TS, d), lambda i, nv: (i, 0)),
            pl.BlockSpec((TOP_K, TS), lambda i, nv: (0, i)),
            pl.BlockSpec((TOP_K, TS), lambda i, nv: (0, i)),
            pl.BlockSpec((1, d), lambda i, nv: (0, 0)),
        ],
        out_specs=pl.BlockSpec((TS, d), lambda i, nv: (i, 0)),
        scratch_shapes=[pltpu.VMEM((2, R_CAP, d), jnp.float32), pltpu.SemaphoreType.DMA((2,))],
    )
    return pl.pallas_call(
        _combine_kernel,
        grid_spec=grid_spec,
        out_shape=jax.ShapeDtypeStruct((t, d), jnp.float32),
        compiler_params=pltpu.CompilerParams(
            dimension_semantics=("arbitrary",), vmem_limit_bytes=VMEM_LIMIT),
        name="combine",
    )(nvc, y_tiles, x1, slot_t, gate_t, g_final)


def _chunk_plan(pcc, nb):
    nt = pcc.shape[0]
    i32 = jnp.int32
    pcc_t = pcc.T
    run_start_t = (jnp.cumsum(pcc, axis=1) - pcc).T
    cum_incl = jnp.cumsum(pcc_t, axis=1)
    eblocks = (cum_incl[:, -1] + CPB - 1) // CPB
    bstart = jnp.cumsum(eblocks) - eblocks
    blk = jnp.arange(nb, dtype=i32)
    bexp = jnp.sum(bstart[None, :] <= blk[:, None], axis=1).astype(i32) - 1
    oh = bexp[:, None] == jnp.arange(N_EXPERTS, dtype=i32)[None, :]
    pick = lambda tab: jnp.sum(jnp.where(oh[:, :, None], tab[None], 0), axis=1)
    ci, pc_row, rs_row = pick(cum_incl), pick(pcc_t), pick(run_start_t)
    b0 = jnp.sum(jnp.where(oh, bstart[None, :], 0), axis=1)
    p = (blk - b0)[:, None] * CPB + jnp.arange(CPB, dtype=i32)[None, :]
    before = ci[:, None, :] <= p[:, :, None]
    tile = jnp.sum(before, axis=2).astype(i32)
    cum_excl = jnp.sum(jnp.where(before, pc_row[:, None, :], 0), axis=2)
    at_tile = jnp.arange(nt, dtype=i32)[None, None, :] == tile[:, :, None]
    rs = jnp.sum(jnp.where(at_tile, rs_row[:, None, :], 0), axis=2)
    valid = tile < nt
    src = jnp.where(valid, tile * CPT + rs + (p - cum_excl), 0).astype(i32).reshape(-1)
    nvalid = jnp.sum(valid, axis=1).astype(i32)
    return src, bexp, nvalid


def kernel(x, g_mix, w_in, b_in, sinks, w_dw, b_dw, ln_g, ln_b, g_attn_out, g_conv_out, w_out, b_out,
           g_ffn, w_router, b_router, w1, b1, w2, b2, g_final):
    bsz, seq, d = x.shape
    t = bsz * seq
    nt = t // TS
    depth = g_mix.shape[0]
    assert depth == 1, "combine fuses the final RMSNorm, so exactly one layer is supported"
    f32 = jnp.float32
    max_chunks = nt * ((TS * TOP_K + N_EXPERTS * (CHUNK - 1)) // CHUNK) + N_EXPERTS * (CPB - 1)
    nb = -(-max_chunks // CPB)
    ci = jnp.arange(2 * LANES)
    dei = (ci[:, None] == jnp.where(ci < LANES, 2 * ci, 2 * (ci - LANES) + 1)[None, :]).astype(jnp.bfloat16)

    for l in range(depth):
        sink_cols = jnp.repeat(sinks[l].astype(f32).reshape(N_KV_HEADS, Q_PER_KV), WINDOW, axis=1)[..., None]
        x1 = _mix(x, g_mix[l][None], w_in[l].astype(jnp.bfloat16), b_in[l][None], sink_cols,
                  w_dw[l], b_dw[l][None], ln_g[l][None], ln_b[l][None],
                  g_attn_out[l][None], g_conv_out[l][None], w_out[l].astype(jnp.bfloat16), b_out[l][None])
        x1 = x1.reshape(t, d)
        xs, slot_t, gate_t, pcc = _route(x1, g_ffn[l][None], w_router[l].T, b_router[l][:, None])
        pcc = pcc[:, :, 0]
        src, bexp, nvalid = _chunk_plan(pcc, nb)
        b1l = b1[l].reshape(N_EXPERTS, 1, -1, 2)
        nvc = jnp.sum(pcc, axis=1).astype(jnp.int32)
        y = _experts(src, bexp, nvalid, nvc, xs.reshape(nt * CPT, CHUNK, d), w1[l], b1l[..., 0], b1l[..., 1],
                     w2[l], b2[l][:, None, :], dei)
        x = _combine(nvc, y.reshape(nt, R_CAP, d), x1, slot_t, gate_t, g_final[None]).reshape(bsz, seq, d)
    return x
```

```python
import functools

import jax
import jax.numpy as jnp
from jax import lax
from jax.experimental import pallas as pl
from jax.experimental.pallas import tpu as pltpu

HEAD_DIM = 64
N_Q_HEADS = 8
N_KV_HEADS = 2
Q_PER_KV = N_Q_HEADS // N_KV_HEADS
ATTN_WIDTH = N_Q_HEADS * HEAD_DIM
KV_WIDTH = N_KV_HEADS * HEAD_DIM
WINDOW = 128
CONV_KERNEL = 31
N_EXPERTS = 32
TOP_K = 4
SWIGLU_LIMIT = 7.0
SWIGLU_ALPHA = 1.702
RMS_EPS = 1e-5
LN_EPS = 1e-5

LANES = 128
SUBLANES = 8

TS = 256
CHUNK = SUBLANES
R_CAP = -(-(TS * TOP_K + N_EXPERTS * (CHUNK - 1)) // LANES) * LANES
CPT = R_CAP // CHUNK
TAIL_MAX = CPT - TS * TOP_K // CHUNK
_TAIL_PIECES = tuple(1 << i for i in reversed(range(TAIL_MAX.bit_length())))
TM = 256
CPB = TM // CHUNK
CONV_HALO = 32
CONV_ROWS = 64
NEG_BIG = -1e30
VMEM_LIMIT = 56 * 1024 * 1024


def _rms(x, g):
    return x * lax.rsqrt(jnp.mean(x * x, axis=-1, keepdims=True) + RMS_EPS) * g


def _dot(a, b):
    return jnp.dot(a, b, preferred_element_type=jnp.float32)


def _dot_nt(a, b, precision=None):
    return lax.dot_general(a, b, (((1,), (1,)), ((), ())), precision=precision,
                           preferred_element_type=jnp.float32)


def _mix_kernel(x_ref, gmix_ref, win_ref, bin_ref, sink_ref, wdw_ref, bdw_ref, lng_ref, lnb_ref,
                ga_ref, gc_ref, wout_ref, bout_ref, o_ref, kv_prev, u_buf, u_sh, conv_buf):
    s = pl.program_id(1)
    first = s == 0

    @pl.when(first)
    def _():
        kv_prev[...] = jnp.zeros_like(kv_prev)
        u_buf[0:CONV_HALO, :] = jnp.zeros((CONV_HALO, u_buf.shape[1]), jnp.float32)

    x = x_ref[0]
    u = _rms(x, gmix_ref[...])
    proj = _dot(u.astype(jnp.bfloat16), win_ref[...]) + bin_ref[...]
    o_q, o_k, o_v, o_a, o_g = 0, ATTN_WIDTH, ATTN_WIDTH + KV_WIDTH, ATTN_WIDTH + 2 * KV_WIDTH, \
        ATTN_WIDTH + 2 * KV_WIDTH + ATTN_WIDTH
    q = proj[:, o_q:o_k] * (HEAD_DIM ** -0.5)
    k = proj[:, o_k:o_v]
    v = proj[:, o_v:o_a]
    conv_a = proj[:, o_a:o_g]
    conv_gate = proj[:, o_g:]

    lane = lax.broadcasted_iota(jnp.int32, (WINDOW + TS, LANES), 1)
    low = lane < HEAD_DIM
    kfull = jnp.concatenate([kv_prev[:, 0:KV_WIDTH], k], axis=0)
    vfull = jnp.concatenate([kv_prev[:, KV_WIDTH:], v], axis=0)
    kroll = pltpu.roll(kfull, HEAD_DIM, axis=1)
    vroll = pltpu.roll(vfull, HEAD_DIM, axis=1)
    kk = [jnp.where(low, kfull, kroll).astype(jnp.bfloat16), jnp.where(low, kroll, kfull).astype(jnp.bfloat16)]
    vv = [jnp.where(low, vfull, vroll).astype(jnp.bfloat16), jnp.where(low, vroll, vfull).astype(jnp.bfloat16)]
    kv_prev[:, 0:KV_WIDTH] = k[TS - WINDOW:, :]
    kv_prev[:, KV_WIDTH:] = v[TS - WINDOW:, :]

    rows = Q_PER_KV * WINDOW
    qi = lax.broadcasted_iota(jnp.int32, (rows, 2 * WINDOW), 0) & (WINDOW - 1)
    kj = lax.broadcasted_iota(jnp.int32, (rows, 2 * WINDOW), 1)
    band = (kj > qi) & (kj <= qi + WINDOW)
    lane_q = lax.broadcasted_iota(jnp.int32, (WINDOW, LANES), 1)
    low_q = lane_q < HEAD_DIM
    ones_kv = jnp.ones((2 * WINDOW, LANES), jnp.bfloat16)

    attn_blocks = []
    for b in range(TS // WINDOW):
        qb = q[b * WINDOW:(b + 1) * WINDOW, :]
        valid = band & ((kj >= WINDOW) | jnp.logical_not(first & (b == 0)))
        head_out = []
        for g in range(N_KV_HEADS):
            parts = []
            for i in range(Q_PER_KV):
                h = g * Q_PER_KV + i
                q128 = qb[:, (h // 2) * LANES:(h // 2 + 1) * LANES]
                keep = low_q if h % 2 == 0 else jnp.logical_not(low_q)
                parts.append(jnp.where(keep, q128, 0.0))
            qs = jnp.concatenate(parts, axis=0).astype(jnp.bfloat16)
            kb = kk[g][b * WINDOW:b * WINDOW + 2 * WINDOW, :]
            vb = vv[g][b * WINDOW:b * WINDOW + 2 * WINDOW, :]
            sc = _dot_nt(qs, kb)
            sc = jnp.where(valid, sc, NEG_BIG)
            sink = sink_ref[g]
            m = jnp.maximum(jnp.max(sc, axis=-1, keepdims=True), sink)
            p = jnp.exp(sc - m).astype(jnp.bfloat16)
            den = _dot(p, ones_kv) + jnp.exp(sink - m)
            pv = _dot(p, vb)
            o = pv / den
            for i in range(Q_PER_KV):
                head_out.append(o[i * WINDOW:(i + 1) * WINDOW, :])
        cols = [jnp.where(low_q, head_out[2 * j], head_out[2 * j + 1]) for j in range(N_Q_HEADS // 2)]
        attn_blocks.append(jnp.concatenate(cols, axis=1))
    attn = jnp.concatenate(attn_blocks, axis=0)

    cw = u_buf.shape[1]
    u_buf[CONV_HALO:CONV_HALO + TS, :] = conv_a * jax.nn.sigmoid(conv_gate)
    n_sh = u_sh.shape[1]
    for s in range(1, SUBLANES):
        u_sh[s - 1] = u_buf[s:s + n_sh, :]
    shift = CONV_HALO - (CONV_KERNEL - 1)
    for c in range(cw // LANES):
        cs = slice(c * LANES, (c + 1) * LANES)
        for r0 in range(0, TS, CONV_ROWS):
            acc = jnp.zeros((CONV_ROWS, LANES), jnp.float32)
            for j in range(CONV_KERNEL):
                base, s = (shift + j) // SUBLANES * SUBLANES, (shift + j) % SUBLANES
                rs = slice(r0 + base, r0 + base + CONV_ROWS)
                win = u_buf[rs, cs] if s == 0 else u_sh[s - 1, rs, cs]
                acc = acc + wdw_ref[j:j + 1, cs] * win
            conv_buf[r0:r0 + CONV_ROWS, cs] = acc
    u_buf[0:CONV_HALO, :] = u_buf[TS:TS + CONV_HALO, :]
    cv = conv_buf[...] + bdw_ref[...]
    mu = jnp.mean(cv, axis=-1, keepdims=True)
    xc = cv - mu
    cv = xc * lax.rsqrt(jnp.mean(xc * xc, axis=-1, keepdims=True) + LN_EPS) * lng_ref[...] + lnb_ref[...]
    cv = cv * jax.nn.sigmoid(cv)

    mixed = jnp.concatenate([_rms(attn, ga_ref[...]), _rms(cv, gc_ref[...])], axis=1).astype(jnp.bfloat16)
    o_ref[0] = x + _dot(mixed, wout_ref[...]) + bout_ref[...]


def _mix(x, g_mix, w_in, b_in, sink_cols, w_dw, b_dw, ln_g, ln_b, g_a, g_c, w_out, b_out):
    bsz, seq, d = x.shape
    assert seq % TS == 0 and TS % WINDOW == 0 and CONV_HALO >= CONV_KERNEL - 1
    cw = w_dw.shape[1]
    const = lambda shape: pl.BlockSpec(shape, lambda b, s: (0,) * len(shape))
    return pl.pallas_call(
        _mix_kernel,
        grid=(bsz, seq // TS),
        in_specs=[
            pl.BlockSpec((1, TS, d), lambda b, s: (b, s, 0)),
            const((1, d)), const(w_in.shape), const((1, w_in.shape[1])),
            const(sink_cols.shape), const(w_dw.shape), const((1, cw)), const((1, cw)), const((1, cw)),
            const((1, ATTN_WIDTH)), const((1, cw)), const(w_out.shape), const((1, d)),
        ],
        out_specs=pl.BlockSpec((1, TS, d), lambda b, s: (b, s, 0)),
        out_shape=jax.ShapeDtypeStruct(x.shape, jnp.float32),
        scratch_shapes=[
            pltpu.VMEM((WINDOW, 2 * KV_WIDTH), jnp.float32),
            pltpu.VMEM((CONV_HALO + TS, cw), jnp.float32),
            pltpu.VMEM((SUBLANES - 1, CONV_HALO + TS - SUBLANES, cw), jnp.float32),
            pltpu.VMEM((TS, cw), jnp.float32),
        ],
        compiler_params=pltpu.CompilerParams(
            dimension_semantics=("arbitrary", "arbitrary"), vmem_limit_bytes=VMEM_LIMIT),
        name="mix",
    )(x, g_mix, w_in, b_in, sink_cols, w_dw, b_dw, ln_g, ln_b, g_a, g_c, w_out, b_out)


def _route_kernel(x1_ref, gffn_ref, wrt_ref, brt_ref, xs_ref, slot_ref, gate_ref, pcc_ref):
    h = _rms(x1_ref[...], gffn_ref[...])
    lg = _dot_nt(wrt_ref[...], h, precision=lax.Precision.HIGHEST) + brt_ref[...]
    iota_e = lax.broadcasted_iota(jnp.int32, (N_EXPERTS, TS), 0)
    vals, hots = [], []
    member = jnp.zeros((N_EXPERTS, TS), jnp.float32)
    for _ in range(TOP_K):
        mx = jnp.max(lg, axis=0, keepdims=True)
        idx = jnp.min(jnp.where(lg == mx, iota_e, N_EXPERTS), axis=0, keepdims=True)
        hot = iota_e == idx
        lg = jnp.where(hot, -jnp.inf, lg)
        member = member + hot.astype(jnp.float32)
        vals.append(mx)
        hots.append(hot)
    ex = [jnp.exp(vk - vals[0]) for vk in vals]
    den = ex[0] + ex[1] + ex[2] + ex[3]
    gate_ref[...] = jnp.concatenate([e / den for e in ex], axis=0)

    ti = lax.broadcasted_iota(jnp.int32, (TS, TS), 0)
    tj = lax.broadcasted_iota(jnp.int32, (TS, TS), 1)
    upper = jnp.where(ti < tj, 1.0, 0.0).astype(jnp.bfloat16)
    cum = _dot(member.astype(jnp.bfloat16), upper)
    cnt = jnp.sum(member, axis=1, keepdims=True)
    pcc = jnp.floor((cnt + (CHUNK - 1)) * (1.0 / CHUNK))
    pcb = jnp.broadcast_to(pcc, (N_EXPERTS, LANES))
    row_e = lax.broadcasted_iota(jnp.int32, (N_EXPERTS, LANES), 0)
    inc = pcb
    sh = 1
    while sh < N_EXPERTS:
        inc = inc + jnp.where(row_e >= sh, pltpu.roll(inc, sh, axis=0), 0.0)
        sh *= 2
    run_start = (inc - pcb)[:, 0:1] * CHUNK
    pcc_ref[0] = pcb.astype(jnp.int32)

    pos = run_start + cum
    slots = [jnp.sum(jnp.where(hot, pos, 0.0), axis=0, keepdims=True).astype(jnp.int32) for hot in hots]
    slot_ref[...] = jnp.concatenate(slots, axis=0)

    iota_r = lax.broadcasted_iota(jnp.int32, (R_CAP, TS), 0)
    sel = (iota_r == slots[0]) | (iota_r == slots[1]) | (iota_r == slots[2]) | (iota_r == slots[3])
    perm = jnp.where(sel, 1.0, 0.0).astype(jnp.bfloat16)
    xs_ref[0] = _dot(perm, h.astype(jnp.bfloat16))


def _route(x1, g_ffn, wr_t, br_t):
    t, d = x1.shape
    nt = t // TS
    const = lambda shape: pl.BlockSpec(shape, lambda i: (0,) * len(shape))
    return pl.pallas_call(
        _route_kernel,
        grid=(nt,),
        in_specs=[pl.BlockSpec((TS, d), lambda i: (i, 0)), const((1, d)), const(wr_t.shape), const(br_t.shape)],
        out_specs=[
            pl.BlockSpec((1, R_CAP, d), lambda i: (i, 0, 0)),
            pl.BlockSpec((TOP_K, TS), lambda i: (0, i)),
            pl.BlockSpec((TOP_K, TS), lambda i: (0, i)),
            pl.BlockSpec((1, N_EXPERTS, LANES), lambda i: (i, 0, 0)),
        ],
        out_shape=[
            jax.ShapeDtypeStruct((nt, R_CAP, d), jnp.float32),
            jax.ShapeDtypeStruct((TOP_K, t), jnp.int32),
            jax.ShapeDtypeStruct((TOP_K, t), jnp.float32),
            jax.ShapeDtypeStruct((nt, N_EXPERTS, LANES), jnp.int32),
        ],
        compiler_params=pltpu.CompilerParams(
            dimension_semantics=("arbitrary",), vmem_limit_bytes=VMEM_LIMIT),
        name="route",
    )(x1, g_ffn, wr_t, br_t)


def _experts_kernel(src_ref, bexp_ref, nval_ref, nvc_ref,
                    xs_hbm, w1_ref, b1g_ref, b1l_ref, w2_ref, b2_ref, dei_ref,
                    y_hbm,
                    xbuf, ybuf, zbuf, w1g, w1l, w2b, sem_in, sem_out, sem_z):
    b = pl.program_id(0)
    nb = pl.num_programs(0)
    n_tiles = nvc_ref.shape[0]
    slot = b % 2

    def zero_tail(tile, start):
        n = CPT - nvc_ref[tile]
        off = tile * CPT + nvc_ref[tile]
        for pc in _TAIL_PIECES:
            take = (n & pc) != 0

            @pl.when(take)
            def _(off=off, pc=pc):
                cp = pltpu.make_async_copy(zbuf.at[pl.ds(0, pc)], y_hbm.at[pl.ds(off, pc)], sem_z)
                if start:
                    cp.start()
                else:
                    cp.wait()
            off = off + jnp.where(take, pc, 0)

    @pl.when(b == 0)
    def _():
        zbuf[...] = jnp.zeros_like(zbuf)

    @pl.when(b < n_tiles)
    def _():
        zero_tail(b, True)

    def chunk_copy(blk, sl, c, inbound):
        if inbound:
            return pltpu.make_async_copy(xs_hbm.at[src_ref[blk * CPB + c]], xbuf.at[sl, c], sem_in.at[sl])
        return pltpu.make_async_copy(ybuf.at[sl, c], y_hbm.at[src_ref[blk * CPB + c]], sem_out.at[sl])

    def start_all(blk, sl, inbound, static):
        if static:
            for c in range(CPB):
                chunk_copy(blk, sl, c, inbound).start()
        else:
            def body(c, carry):
                chunk_copy(blk, sl, c, inbound).start()
                return carry
            lax.fori_loop(0, nval_ref[blk], body, 0)

    def wait_all(blk, sl, inbound):
        n = nval_ref[blk]

        @pl.when(n == CPB)
        def _():
            if inbound:
                pltpu.make_async_copy(xs_hbm.at[pl.ds(0, CPB)], xbuf.at[sl], sem_in.at[sl]).wait()
            else:
                pltpu.make_async_copy(ybuf.at[sl], y_hbm.at[pl.ds(0, CPB)], sem_out.at[sl]).wait()

        @pl.when(n != CPB)
        def _():
            def body(c, carry):
                chunk_copy(blk, sl, c, inbound).wait()
                return carry
            lax.fori_loop(0, n, body, 0)

    def compute():
        xb = xbuf[slot].reshape(TM, xbuf.shape[3]).astype(jnp.bfloat16)
        hg = _dot(xb, w1g[...]) + b1g_ref[0]
        hl = _dot(xb, w1l[...]) + b1l_ref[0]
        hg = jnp.minimum(hg, SWIGLU_LIMIT)
        hl = jnp.clip(hl, -SWIGLU_LIMIT, SWIGLU_LIMIT)
        act = hg * jax.nn.sigmoid(SWIGLU_ALPHA * hg) * (hl + 1.0)
        y = _dot(act.astype(jnp.bfloat16), w2b[...]) + b2_ref[0]
        ybuf[slot] = y.reshape(CPB, CHUNK, y.shape[1])

    @pl.when(b == 0)
    def _():
        xbuf[...] = jnp.zeros_like(xbuf)
        start_all(0, 0, True, False)

    prev_e = bexp_ref[jnp.maximum(b - 1, 0)]

    @pl.when((b == 0) | (bexp_ref[b] != prev_e))
    def _():
        dei = dei_ref[...]
        for j in range(w1_ref.shape[2] // (2 * LANES)):
            wj = w1_ref[0, :, j * 2 * LANES:(j + 1) * 2 * LANES].astype(jnp.bfloat16)
            r = _dot(wj, dei)
            w1g[:, j * LANES:(j + 1) * LANES] = r[:, 0:LANES].astype(jnp.bfloat16)
            w1l[:, j * LANES:(j + 1) * LANES] = r[:, LANES:].astype(jnp.bfloat16)
        w2b[...] = w2_ref[0].astype(jnp.bfloat16)

    @pl.when(b >= 2)
    def _():
        wait_all(b - 2, slot, False)

    wait_all(b, slot, True)

    prev_n = nval_ref[jnp.maximum(b - 1, 0)]
    next_n = nval_ref[jnp.minimum(b + 1, nb - 1)]
    fast = (b >= 1) & (prev_n == CPB) & (b + 1 < nb) & (next_n == CPB) & (nval_ref[b] > 0)

    @pl.when(fast)
    def _():
        start_all(b - 1, 1 - slot, False, True)
        start_all(b + 1, 1 - slot, True, True)
        compute()

    @pl.when(jnp.logical_not(fast))
    def _():
        @pl.when(b >= 1)
        def _():
            start_all(b - 1, 1 - slot, False, False)

        @pl.when(b + 1 < nb)
        def _():
            start_all(b + 1, 1 - slot, True, False)

        @pl.when(nval_ref[b] > 0)
        def _():
            compute()

    @pl.when(b < n_tiles)
    def _():
        zero_tail(b, False)

    @pl.when(b == nb - 1)
    def _():
        start_all(b, slot, False, False)

        @pl.when(b >= 1)
        def _():
            wait_all(b - 1, 1 - slot, False)
        wait_all(b, slot, False)


def _experts(chunk_src, block_expert, block_nvalid, nvc, xs_chunks, w1, b1g, b1l, w2, b2, dei):
    n_chunks, _, d = xs_chunks.shape
    nb = block_expert.shape[0]
    assert nb >= nvc.shape[0]
    de2 = w1.shape[2]
    grid_spec = pltpu.PrefetchScalarGridSpec(
        num_scalar_prefetch=4,
        grid=(nb,),
        in_specs=[
            pl.BlockSpec(memory_space=pl.ANY),
            pl.BlockSpec((1, d, de2), lambda i, src, be, nv, nc: (be[i], 0, 0)),
            pl.BlockSpec((1, 1, de2 // 2), lambda i, src, be, nv, nc: (be[i], 0, 0)),
            pl.BlockSpec((1, 1, de2 // 2), lambda i, src, be, nv, nc: (be[i], 0, 0)),
            pl.BlockSpec((1, de2 // 2, d), lambda i, src, be, nv, nc: (be[i], 0, 0)),
            pl.BlockSpec((1, 1, d), lambda i, src, be, nv, nc: (be[i], 0, 0)),
            pl.BlockSpec(dei.shape, lambda i, src, be, nv, nc: (0, 0)),
        ],
        out_specs=pl.BlockSpec(memory_space=pl.ANY),
        scratch_shapes=[
            pltpu.VMEM((2, CPB, CHUNK, d), jnp.float32),
            pltpu.VMEM((2, CPB, CHUNK, d), jnp.float32),
            pltpu.VMEM((TAIL_MAX, CHUNK, d), jnp.float32),
            pltpu.VMEM((d, de2 // 2), jnp.bfloat16),
            pltpu.VMEM((d, de2 // 2), jnp.bfloat16),
            pltpu.VMEM((de2 // 2, d), jnp.bfloat16),
            pltpu.SemaphoreType.DMA((2,)),
            pltpu.SemaphoreType.DMA((2,)),
            pltpu.SemaphoreType.DMA(()),
        ],
    )
    return pl.pallas_call(
        _experts_kernel,
        grid_spec=grid_spec,
        out_shape=jax.ShapeDtypeStruct(xs_chunks.shape, jnp.float32),
        compiler_params=pltpu.CompilerParams(
            dimension_semantics=("arbitrary",), vmem_limit_bytes=VMEM_LIMIT),
        name="experts",
    )(chunk_src, block_expert, block_nvalid, nvc, xs_chunks, w1, b1g, b1l, w2, b2, dei)


_PIECES = tuple(1 << i for i in reversed(range((CPT).bit_length())))


def _combine_kernel(nvc_ref,
                    y_hbm, x1_ref, slot_ref, gate_ref, gfin_ref, o_ref, ybuf, sem):
    i = pl.program_id(0)
    nt = pl.num_programs(0)
    sl = i % 2

    def fetch(tile, s_, start):
        n = nvc_ref[tile]
        off = jnp.int32(0)
        for pc in _PIECES:
            if pc * CHUNK > R_CAP:
                continue
            take = (n & pc) != 0

            @pl.when(take)
            def _(off=off, pc=pc):
                cp = pltpu.make_async_copy(
                    y_hbm.at[tile, pl.ds(pl.multiple_of(off * CHUNK, CHUNK), pc * CHUNK)],
                    ybuf.at[s_, pl.ds(pl.multiple_of(off * CHUNK, CHUNK), pc * CHUNK)], sem.at[s_])
                if start:
                    cp.start()
                else:
                    cp.wait()
            off = off + jnp.where(take, pc, 0)

    @pl.when(i == 0)
    def _():
        ybuf[...] = jnp.zeros_like(ybuf)
        fetch(0, 0, True)

    @pl.when(i + 1 < nt)
    def _():
        fetch(i + 1, 1 - sl, True)

    fetch(i, sl, False)
    iota_r = lax.broadcasted_iota(jnp.int32, (R_CAP, TS), 0)
    gt = jnp.zeros((R_CAP, TS), jnp.float32)
    for k in range(TOP_K):
        gt = jnp.where(iota_r == slot_ref[k:k + 1, :], gate_ref[k:k + 1, :], gt)
    moe = lax.dot_general(gt.astype(jnp.bfloat16), ybuf[sl].astype(jnp.bfloat16),
                          (((0,), (0,)), ((), ())), preferred_element_type=jnp.float32)
    o_ref[...] = _rms(x1_ref[...] + moe, gfin_ref[...])


def _combine(nvc, y_tiles, x1, slot_t, gate_t, g_final):
    t, d = x1.shape
    nt = t // TS
    grid_spec = pltpu.PrefetchScalarGridSpec(
        num_scalar_prefetch=1,
        grid=(nt,),
        in_specs=[
            pl.BlockSpec(memory_space=pl.ANY),
            pl.BlockSpec((TS, d), lambda i, nv: (i, 0)),
            pl.BlockSpec((TOP_K, TS), lambda i, nv: (0, i)),
            pl.BlockSpec((TOP_K, TS), lambda i, nv: (0, i)),
            pl.BlockSpec((1, d), lambda i, nv: (0, 0)),
        ],
        out_specs=pl.BlockSpec((TS, d), lambda i, nv: (i, 0)),
        scratch_shapes=[pltpu.VMEM((2, R_CAP, d), jnp.float32), pltpu.SemaphoreType.DMA((2,))],
    )
    return pl.pallas_call(
        _combine_kernel,
        grid_spec=grid_spec,
        out_shape=jax.ShapeDtypeStruct((t, d), jnp.float32),
        compiler_params=pltpu.CompilerParams(
            dimension_semantics=("arbitrary",), vmem_limit_bytes=VMEM_LIMIT),
        name="combine",
    )(nvc, y_tiles, x1, slot_t, gate_t, g_final)


def _chunk_plan(pcc, nb):
    nt = pcc.shape[0]
    i32 = jnp.int32
    pcc_t = pcc.T
    run_start_t = (jnp.cumsum(pcc, axis=1) - pcc).T
    cum_incl = jnp.cumsum(pcc_t, axis=1)
    eblocks = (cum_incl[:, -1] + CPB - 1) // CPB
    bstart = jnp.cumsum(eblocks) - eblocks
    blk = jnp.arange(nb, dtype=i32)
    bexp = jnp.sum(bstart[None, :] <= blk[:, None], axis=1).astype(i32) - 1
    oh = bexp[:, None] == jnp.arange(N_EXPERTS, dtype=i32)[None, :]
    pick = lambda tab: jnp.sum(jnp.where(oh[:, :, None], tab[None], 0), axis=1)
    ci, pc_row, rs_row = pick(cum_incl), pick(pcc_t), pick(run_start_t)
    b0 = jnp.sum(jnp.where(oh, bstart[None, :], 0), axis=1)
    p = (blk - b0)[:, None] * CPB + jnp.arange(CPB, dtype=i32)[None, :]
    before = ci[:, None, :] <= p[:, :, None]
    tile = jnp.sum(before, axis=2).astype(i32)
    cum_excl = jnp.sum(jnp.where(before, pc_row[:, None, :], 0), axis=2)
    at_tile = jnp.arange(nt, dtype=i32)[None, None, :] == tile[:, :, None]
    rs = jnp.sum(jnp.where(at_tile, rs_row[:, None, :], 0), axis=2)
    valid = tile < nt
    src = jnp.where(valid, tile * CPT + rs + (p - cum_excl), 0).astype(i32).reshape(-1)
    nvalid = jnp.sum(valid, axis=1).astype(i32)
    return src, bexp, nvalid


def kernel(x, g_mix, w_in, b_in, sinks, w_dw, b_dw, ln_g, ln_b, g_attn_out, g_conv_out, w_out, b_out,
           g_ffn, w_router, b_router, w1, b1, w2, b2, g_final):
    bsz, seq, d = x.shape
    t = bsz * seq
    nt = t // TS
    depth = g_mix.shape[0]
    assert depth == 1, "combine fuses the final RMSNorm, so exactly one layer is supported"
    f32 = jnp.float32
    max_chunks = nt * ((TS * TOP_K + N_EXPERTS * (CHUNK - 1)) // CHUNK) + N_EXPERTS * (CPB - 1)
    nb = -(-max_chunks // CPB)
    ci = jnp.arange(2 * LANES)
    dei = (ci[:, None] == jnp.where(ci < LANES, 2 * ci, 2 * (ci - LANES) + 1)[None, :]).astype(jnp.bfloat16)

    for l in range(depth):
        sink_cols = jnp.repeat(sinks[l].astype(f32).reshape(N_KV_HEADS, Q_PER_KV), WINDOW, axis=1)[..., None]
        x1 = _mix(x, g_mix[l][None], w_in[l].astype(jnp.bfloat16), b_in[l][None], sink_cols,
                  w_dw[l], b_dw[l][None], ln_g[l][None], ln_b[l][None],
                  g_attn_out[l][None], g_conv_out[l][None], w_out[l].astype(jnp.bfloat16), b_out[l][None])
        x1 = x1.reshape(t, d)
        xs, slot_t, gate_t, pcc = _route(x1, g_ffn[l][None], w_router[l].T, b_router[l][:, None])
        pcc = pcc[:, :, 0]
        src, bexp, nvalid = _chunk_plan(pcc, nb)
        b1l = b1[l].reshape(N_EXPERTS, 1, -1, 2)
        nvc = jnp.sum(pcc, axis=1).astype(jnp.int32)
        y = _experts(src, bexp, nvalid, nvc, xs.reshape(nt * CPT, CHUNK, d), w1[l], b1l[..., 0], b1l[..., 1],
                     w2[l], b2[l][:, None, :], dei)
        x = _combine(nvc, y.reshape(nt, R_CAP, d), x1, slot_t, gate_t, g_final[None]).reshape(bsz, seq, d)
    return x
```

```python
import functools

import jax
import jax.numpy as jnp
from jax import lax
from jax.experimental import pallas as pl
from jax.experimental.pallas import tpu as pltpu

HEAD_DIM = 64
N_Q_HEADS = 8
N_KV_HEADS = 2
Q_PER_KV = N_Q_HEADS // N_KV_HEADS
ATTN_WIDTH = N_Q_HEADS * HEAD_DIM
KV_WIDTH = N_KV_HEADS * HEAD_DIM
WINDOW = 128
CONV_KERNEL = 31
N_EXPERTS = 32
TOP_K = 4
SWIGLU_LIMIT = 7.0
SWIGLU_ALPHA = 1.702
RMS_EPS = 1e-5
LN_EPS = 1e-5

LANES = 128
SUBLANES = 8

TS = 256
CHUNK = SUBLANES
R_CAP = -(-(TS * TOP_K + N_EXPERTS * (CHUNK - 1)) // LANES) * LANES
CPT = R_CAP // CHUNK
TAIL_MAX = CPT - TS * TOP_K // CHUNK
_TAIL_PIECES = tuple(1 << i for i in reversed(range(TAIL_MAX.bit_length())))
TM = 256
CPB = TM // CHUNK
CONV_HALO = 32
CONV_ROWS = 64
NEG_BIG = -1e30
VMEM_LIMIT = 56 * 1024 * 1024


def _rms(x, g):
    return x * lax.rsqrt(jnp.mean(x * x, axis=-1, keepdims=True) + RMS_EPS) * g


def _dot(a, b):
    return jnp.dot(a, b, preferred_element_type=jnp.float32)


_HI16 = 0xFFFF0000


def _pack_halves(v):
    half = v.shape[1] // 2
    as_bits = lambda t: lax.bitcast_convert_type(t.astype(jnp.bfloat16).astype(jnp.float32), jnp.uint32)
    return (as_bits(v[:, half:]) & jnp.uint32(_HI16)) | (as_bits(v[:, :half]) >> 16)


def _unpack_halves(w):
    lo = lax.bitcast_convert_type(w << 16, jnp.float32).astype(jnp.bfloat16)
    hi = lax.bitcast_convert_type(w & jnp.uint32(_HI16), jnp.float32).astype(jnp.bfloat16)
    return jnp.concatenate([lo, hi], axis=1)


def _dot_nt(a, b, precision=None):
    return lax.dot_general(a, b, (((1,), (1,)), ((), ())), precision=precision,
                           preferred_element_type=jnp.float32)


def _mix_kernel(x_ref, gmix_ref, win_ref, bin_ref, sink_ref, wdw_ref, bdw_ref, lng_ref, lnb_ref,
                ga_ref, gc_ref, wout_ref, bout_ref, o_ref, kv_prev, u_buf, u_sh, conv_buf):
    s = pl.program_id(1)
    first = s == 0

    @pl.when(first)
    def _():
        kv_prev[...] = jnp.zeros_like(kv_prev)
        u_buf[0:CONV_HALO, :] = jnp.zeros((CONV_HALO, u_buf.shape[1]), jnp.float32)

    x = x_ref[0]
    u = _rms(x, gmix_ref[...])
    proj = _dot(u.astype(jnp.bfloat16), win_ref[...]) + bin_ref[...]
    o_q, o_k, o_v, o_a, o_g = 0, ATTN_WIDTH, ATTN_WIDTH + KV_WIDTH, ATTN_WIDTH + 2 * KV_WIDTH, \
        ATTN_WIDTH + 2 * KV_WIDTH + ATTN_WIDTH
    q = proj[:, o_q:o_k] * (HEAD_DIM ** -0.5)
    k = proj[:, o_k:o_v]
    v = proj[:, o_v:o_a]
    conv_a = proj[:, o_a:o_g]
    conv_gate = proj[:, o_g:]

    lane = lax.broadcasted_iota(jnp.int32, (WINDOW + TS, LANES), 1)
    low = lane < HEAD_DIM
    kfull = jnp.concatenate([kv_prev[:, 0:KV_WIDTH], k], axis=0)
    vfull = jnp.concatenate([kv_prev[:, KV_WIDTH:], v], axis=0)
    kroll = pltpu.roll(kfull, HEAD_DIM, axis=1)
    vroll = pltpu.roll(vfull, HEAD_DIM, axis=1)
    kk = [jnp.where(low, kfull, kroll).astype(jnp.bfloat16), jnp.where(low, kroll, kfull).astype(jnp.bfloat16)]
    vv = [jnp.where(low, vfull, vroll).astype(jnp.bfloat16), jnp.where(low, vroll, vfull).astype(jnp.bfloat16)]
    kv_prev[:, 0:KV_WIDTH] = k[TS - WINDOW:, :]
    kv_prev[:, KV_WIDTH:] = v[TS - WINDOW:, :]

    rows = Q_PER_KV * WINDOW
    qi = lax.broadcasted_iota(jnp.int32, (rows, 2 * WINDOW), 0) & (WINDOW - 1)
    kj = lax.broadcasted_iota(jnp.int32, (rows, 2 * WINDOW), 1)
    band = (kj > qi) & (kj <= qi + WINDOW)
    lane_q = lax.broadcasted_iota(jnp.int32, (WINDOW, LANES), 1)
    low_q = lane_q < HEAD_DIM
    ones_kv = jnp.ones((2 * WINDOW, LANES), jnp.bfloat16)

    attn_blocks = []
    for b in range(TS // WINDOW):
        qb = q[b * WINDOW:(b + 1) * WINDOW, :]
        valid = band & ((kj >= WINDOW) | jnp.logical_not(first & (b == 0)))
        head_out = []
        for g in range(N_KV_HEADS):
            parts = []
            for i in range(Q_PER_KV):
                h = g * Q_PER_KV + i
                q128 = qb[:, (h // 2) * LANES:(h // 2 + 1) * LANES]
                keep = low_q if h % 2 == 0 else jnp.logical_not(low_q)
                parts.append(jnp.where(keep, q128, 0.0))
            qs = jnp.concatenate(parts, axis=0).astype(jnp.bfloat16)
            kb = kk[g][b * WINDOW:b * WINDOW + 2 * WINDOW, :]
            vb = vv[g][b * WINDOW:b * WINDOW + 2 * WINDOW, :]
            sc = _dot_nt(qs, kb)
            sc = jnp.where(valid, sc, NEG_BIG)
            sink = sink_ref[g]
            m = jnp.maximum(jnp.max(sc, axis=-1, keepdims=True), sink)
            p = jnp.exp(sc - m).astype(jnp.bfloat16)
            den = _dot(p, ones_kv) + jnp.exp(sink - m)
            pv = _dot(p, vb)
            o = pv / den
            for i in range(Q_PER_KV):
                head_out.append(o[i * WINDOW:(i + 1) * WINDOW, :])
        cols = [jnp.where(low_q, head_out[2 * j], head_out[2 * j + 1]) for j in range(N_Q_HEADS // 2)]
        attn_blocks.append(jnp.concatenate(cols, axis=1))
    attn = jnp.concatenate(attn_blocks, axis=0)

    cw = u_buf.shape[1]
    u_buf[CONV_HALO:CONV_HALO + TS, :] = conv_a * jax.nn.sigmoid(conv_gate)
    n_sh = u_sh.shape[1]
    for s in range(1, SUBLANES):
        u_sh[s - 1] = u_buf[s:s + n_sh, :]
    shift = CONV_HALO - (CONV_KERNEL - 1)
    for c in range(cw // LANES):
        cs = slice(c * LANES, (c + 1) * LANES)
        for r0 in range(0, TS, CONV_ROWS):
            acc = jnp.zeros((CONV_ROWS, LANES), jnp.float32)
            for j in range(CONV_KERNEL):
                base, s = (shift + j) // SUBLANES * SUBLANES, (shift + j) % SUBLANES
                rs = slice(r0 + base, r0 + base + CONV_ROWS)
                win = u_buf[rs, cs] if s == 0 else u_sh[s - 1, rs, cs]
                acc = acc + wdw_ref[j:j + 1, cs] * win
            conv_buf[r0:r0 + CONV_ROWS, cs] = acc
    u_buf[0:CONV_HALO, :] = u_buf[TS:TS + CONV_HALO, :]
    cv = conv_buf[...] + bdw_ref[...]
    mu = jnp.mean(cv, axis=-1, keepdims=True)
    xc = cv - mu
    cv = xc * lax.rsqrt(jnp.mean(xc * xc, axis=-1, keepdims=True) + LN_EPS) * lng_ref[...] + lnb_ref[...]
    cv = cv * jax.nn.sigmoid(cv)

    mixed = jnp.concatenate([_rms(attn, ga_ref[...]), _rms(cv, gc_ref[...])], axis=1).astype(jnp.bfloat16)
    o_ref[0] = x + _dot(mixed, wout_ref[...]) + bout_ref[...]


def _mix(x, g_mix, w_in, b_in, sink_cols, w_dw, b_dw, ln_g, ln_b, g_a, g_c, w_out, b_out):
    bsz, seq, d = x.shape
    assert seq % TS == 0 and TS % WINDOW == 0 and CONV_HALO >= CONV_KERNEL - 1
    cw = w_dw.shape[1]
    const = lambda shape: pl.BlockSpec(shape, lambda b, s: (0,) * len(shape))
    return pl.pallas_call(
        _mix_kernel,
        grid=(bsz, seq // TS),
        in_specs=[
            pl.BlockSpec((1, TS, d), lambda b, s: (b, s, 0)),
            const((1, d)), const(w_in.shape), const((1, w_in.shape[1])),
            const(sink_cols.shape), const(w_dw.shape), const((1, cw)), const((1, cw)), const((1, cw)),
            const((1, ATTN_WIDTH)), const((1, cw)), const(w_out.shape), const((1, d)),
        ],
        out_specs=pl.BlockSpec((1, TS, d), lambda b, s: (b, s, 0)),
        out_shape=jax.ShapeDtypeStruct(x.shape, jnp.float32),
        scratch_shapes=[
            pltpu.VMEM((WINDOW, 2 * KV_WIDTH), jnp.float32),
            pltpu.VMEM((CONV_HALO + TS, cw), jnp.float32),
            pltpu.VMEM((SUBLANES - 1, CONV_HALO + TS - SUBLANES, cw), jnp.float32),
            pltpu.VMEM((TS, cw), jnp.float32),
        ],
        compiler_params=pltpu.CompilerParams(
            dimension_semantics=("arbitrary", "arbitrary"), vmem_limit_bytes=VMEM_LIMIT),
        name="mix",
    )(x, g_mix, w_in, b_in, sink_cols, w_dw, b_dw, ln_g, ln_b, g_a, g_c, w_out, b_out)


def _route_kernel(x1_ref, gffn_ref, wrt_ref, brt_ref, xs_ref, slot_ref, gate_ref, pcc_ref):
    h = _rms(x1_ref[...], gffn_ref[...])
    lg = _dot_nt(wrt_ref[...], h, precision=lax.Precision.HIGHEST) + brt_ref[...]
    iota_e = lax.broadcasted_iota(jnp.int32, (N_EXPERTS, TS), 0)
    vals, hots = [], []
    member = jnp.zeros((N_EXPERTS, TS), jnp.float32)
    for _ in range(TOP_K):
        mx = jnp.max(lg, axis=0, keepdims=True)
        idx = jnp.min(jnp.where(lg == mx, iota_e, N_EXPERTS), axis=0, keepdims=True)
        hot = iota_e == idx
        lg = jnp.where(hot, -jnp.inf, lg)
        member = member + hot.astype(jnp.float32)
        vals.append(mx)
        hots.append(hot)
    ex = [jnp.exp(vk - vals[0]) for vk in vals]
    den = ex[0] + ex[1] + ex[2] + ex[3]
    gate_ref[...] = jnp.concatenate([e / den for e in ex], axis=0)

    ti = lax.broadcasted_iota(jnp.int32, (TS, TS), 0)
    tj = lax.broadcasted_iota(jnp.int32, (TS, TS), 1)
    upper = jnp.where(ti < tj, 1.0, 0.0).astype(jnp.bfloat16)
    cum = _dot(member.astype(jnp.bfloat16), upper)
    cnt = jnp.sum(member, axis=1, keepdims=True)
    pcc = jnp.floor((cnt + (CHUNK - 1)) * (1.0 / CHUNK))
    pcb = jnp.broadcast_to(pcc, (N_EXPERTS, LANES))
    row_e = lax.broadcasted_iota(jnp.int32, (N_EXPERTS, LANES), 0)
    inc = pcb
    sh = 1
    while sh < N_EXPERTS:
        inc = inc + jnp.where(row_e >= sh, pltpu.roll(inc, sh, axis=0), 0.0)
        sh *= 2
    run_start = (inc - pcb)[:, 0:1] * CHUNK
    pcc_ref[0] = pcb.astype(jnp.int32)

    pos = run_start + cum
    slots = [jnp.sum(jnp.where(hot, pos, 0.0), axis=0, keepdims=True).astype(jnp.int32) for hot in hots]
    slot_ref[...] = jnp.concatenate(slots, axis=0)

    iota_r = lax.broadcasted_iota(jnp.int32, (R_CAP, TS), 0)
    sel = (iota_r == slots[0]) | (iota_r == slots[1]) | (iota_r == slots[2]) | (iota_r == slots[3])
    perm = jnp.where(sel, 1.0, 0.0).astype(jnp.bfloat16)
    xs_ref[0] = _pack_halves(_dot(perm, h.astype(jnp.bfloat16)))


def _route(x1, g_ffn, wr_t, br_t):
    t, d = x1.shape
    nt = t // TS
    const = lambda shape: pl.BlockSpec(shape, lambda i: (0,) * len(shape))
    return pl.pallas_call(
        _route_kernel,
        grid=(nt,),
        in_specs=[pl.BlockSpec((TS, d), lambda i: (i, 0)), const((1, d)), const(wr_t.shape), const(br_t.shape)],
        out_specs=[
            pl.BlockSpec((1, R_CAP, d // 2), lambda i: (i, 0, 0)),
            pl.BlockSpec((TOP_K, TS), lambda i: (0, i)),
            pl.BlockSpec((TOP_K, TS), lambda i: (0, i)),
            pl.BlockSpec((1, N_EXPERTS, LANES), lambda i: (i, 0, 0)),
        ],
        out_shape=[
            jax.ShapeDtypeStruct((nt, R_CAP, d // 2), jnp.uint32),
            jax.ShapeDtypeStruct((TOP_K, t), jnp.int32),
            jax.ShapeDtypeStruct((TOP_K, t), jnp.float32),
            jax.ShapeDtypeStruct((nt, N_EXPERTS, LANES), jnp.int32),
        ],
        compiler_params=pltpu.CompilerParams(
            dimension_semantics=("arbitrary",), vmem_limit_bytes=VMEM_LIMIT),
        name="route",
    )(x1, g_ffn, wr_t, br_t)


def _experts_kernel(src_ref, bexp_ref, nval_ref, nvc_ref,
                    xs_hbm, w1_ref, b1g_ref, b1l_ref, w2_ref, b2_ref, dei_ref,
                    y_hbm,
                    xbuf, ybuf, zbuf, w1g, w1l, w2b, sem_in, sem_out, sem_z):
    b = pl.program_id(0)
    nb = pl.num_programs(0)
    n_tiles = nvc_ref.shape[0]
    slot = b % 2

    def zero_tail(tile, start):
        n = CPT - nvc_ref[tile]
        off = tile * CPT + nvc_ref[tile]
        for pc in _TAIL_PIECES:
            take = (n & pc) != 0

            @pl.when(take)
            def _(off=off, pc=pc):
                cp = pltpu.make_async_copy(zbuf.at[pl.ds(0, pc)], y_hbm.at[pl.ds(off, pc)], sem_z)
                if start:
                    cp.start()
                else:
                    cp.wait()
            off = off + jnp.where(take, pc, 0)

    @pl.when(b == 0)
    def _():
        zbuf[...] = jnp.zeros_like(zbuf)

    @pl.when(b < n_tiles)
    def _():
        zero_tail(b, True)

    def chunk_copy(blk, sl, c, inbound):
        if inbound:
            return pltpu.make_async_copy(xs_hbm.at[src_ref[blk * CPB + c]], xbuf.at[sl, c], sem_in.at[sl])
        return pltpu.make_async_copy(ybuf.at[sl, c], y_hbm.at[src_ref[blk * CPB + c]], sem_out.at[sl])

    def start_all(blk, sl, inbound, static):
        if static:
            for c in range(CPB):
                chunk_copy(blk, sl, c, inbound).start()
        else:
            def body(c, carry):
                chunk_copy(blk, sl, c, inbound).start()
                return carry
            lax.fori_loop(0, nval_ref[blk], body, 0)

    def wait_all(blk, sl, inbound):
        n = nval_ref[blk]

        @pl.when(n == CPB)
        def _():
            if inbound:
                pltpu.make_async_copy(xs_hbm.at[pl.ds(0, CPB)], xbuf.at[sl], sem_in.at[sl]).wait()
            else:
                pltpu.make_async_copy(ybuf.at[sl], y_hbm.at[pl.ds(0, CPB)], sem_out.at[sl]).wait()

        @pl.when(n != CPB)
        def _():
            def body(c, carry):
                chunk_copy(blk, sl, c, inbound).wait()
                return carry
            lax.fori_loop(0, n, body, 0)

    def compute():
        xb = _unpack_halves(xbuf[slot].reshape(TM, xbuf.shape[3]))
        hg = _dot(xb, w1g[...]) + b1g_ref[0]
        hl = _dot(xb, w1l[...]) + b1l_ref[0]
        hg = jnp.minimum(hg, SWIGLU_LIMIT)
        hl = jnp.clip(hl, -SWIGLU_LIMIT, SWIGLU_LIMIT)
        act = hg * jax.nn.sigmoid(SWIGLU_ALPHA * hg) * (hl + 1.0)
        y = _dot(act.astype(jnp.bfloat16), w2b[...]) + b2_ref[0]
        ybuf[slot] = _pack_halves(y).reshape(CPB, CHUNK, y.shape[1] // 2)

    @pl.when(b == 0)
    def _():
        xbuf[...] = jnp.zeros_like(xbuf)
        start_all(0, 0, True, False)

    prev_e = bexp_ref[jnp.maximum(b - 1, 0)]

    @pl.when((b == 0) | (bexp_ref[b] != prev_e))
    def _():
        dei = dei_ref[...]
        for j in range(w1_ref.shape[2] // (2 * LANES)):
            wj = w1_ref[0, :, j * 2 * LANES:(j + 1) * 2 * LANES].astype(jnp.bfloat16)
            r = _dot(wj, dei)
            w1g[:, j * LANES:(j + 1) * LANES] = r[:, 0:LANES].astype(jnp.bfloat16)
            w1l[:, j * LANES:(j + 1) * LANES] = r[:, LANES:].astype(jnp.bfloat16)
        w2b[...] = w2_ref[0].astype(jnp.bfloat16)

    @pl.when(b >= 2)
    def _():
        wait_all(b - 2, slot, False)

    wait_all(b, slot, True)

    prev_n = nval_ref[jnp.maximum(b - 1, 0)]
    next_n = nval_ref[jnp.minimum(b + 1, nb - 1)]
    fast = (b >= 1) & (prev_n == CPB) & (b + 1 < nb) & (next_n == CPB) & (nval_ref[b] > 0)

    @pl.when(fast)
    def _():
        start_all(b - 1, 1 - slot, False, True)
        start_all(b + 1, 1 - slot, True, True)
        compute()

    @pl.when(jnp.logical_not(fast))
    def _():
        @pl.when(b >= 1)
        def _():
            start_all(b - 1, 1 - slot, False, False)

        @pl.when(b + 1 < nb)
        def _():
            start_all(b + 1, 1 - slot, True, False)

        @pl.when(nval_ref[b] > 0)
        def _():
            compute()

    @pl.when(b < n_tiles)
    def _():
        zero_tail(b, False)

    @pl.when(b == nb - 1)
    def _():
        start_all(b, slot, False, False)

        @pl.when(b >= 1)
        def _():
            wait_all(b - 1, 1 - slot, False)
        wait_all(b, slot, False)


def _experts(chunk_src, block_expert, block_nvalid, nvc, xs_chunks, w1, b1g, b1l, w2, b2, dei):
    n_chunks, _, dh = xs_chunks.shape
    nb = block_expert.shape[0]
    assert nb >= nvc.shape[0]
    d, de2 = w1.shape[1], w1.shape[2]
    grid_spec = pltpu.PrefetchScalarGridSpec(
        num_scalar_prefetch=4,
        grid=(nb,),
        in_specs=[
            pl.BlockSpec(memory_space=pl.ANY),
            pl.BlockSpec((1, d, de2), lambda i, src, be, nv, nc: (be[i], 0, 0)),
            pl.BlockSpec((1, 1, de2 // 2), lambda i, src, be, nv, nc: (be[i], 0, 0)),
            pl.BlockSpec((1, 1, de2 // 2), lambda i, src, be, nv, nc: (be[i], 0, 0)),
            pl.BlockSpec((1, de2 // 2, d), lambda i, src, be, nv, nc: (be[i], 0, 0)),
            pl.BlockSpec((1, 1, d), lambda i, src, be, nv, nc: (be[i], 0, 0)),
            pl.BlockSpec(dei.shape, lambda i, src, be, nv, nc: (0, 0)),
        ],
        out_specs=pl.BlockSpec(memory_space=pl.ANY),
        scratch_shapes=[
            pltpu.VMEM((2, CPB, CHUNK, dh), jnp.uint32),
            pltpu.VMEM((2, CPB, CHUNK, dh), jnp.uint32),
            pltpu.VMEM((TAIL_MAX, CHUNK, dh), jnp.uint32),
            pltpu.VMEM((d, de2 // 2), jnp.bfloat16),
            pltpu.VMEM((d, de2 // 2), jnp.bfloat16),
            pltpu.VMEM((de2 // 2, d), jnp.bfloat16),
            pltpu.SemaphoreType.DMA((2,)),
            pltpu.SemaphoreType.DMA((2,)),
            pltpu.SemaphoreType.DMA(()),
        ],
    )
    return pl.pallas_call(
        _experts_kernel,
        grid_spec=grid_spec,
        out_shape=jax.ShapeDtypeStruct(xs_chunks.shape, jnp.uint32),
        compiler_params=pltpu.CompilerParams(
            dimension_semantics=("arbitrary",), vmem_limit_bytes=VMEM_LIMIT),
        name="experts",
    )(chunk_src, block_expert, block_nvalid, nvc, xs_chunks, w1, b1g, b1l, w2, b2, dei)


_PIECES = tuple(1 << i for i in reversed(range((CPT).bit_length())))


def _combine_kernel(nvc_ref,
                    y_hbm, x1_ref, slot_ref, gate_ref, gfin_ref, o_ref, ybuf, sem):
    i = pl.program_id(0)
    nt = pl.num_programs(0)
    sl = i % 2

    def fetch(tile, s_, start):
        n = nvc_ref[tile]
        off = jnp.int32(0)
        for pc in _PIECES:
            if pc * CHUNK > R_CAP:
                continue
            take = (n & pc) != 0

            @pl.when(take)
            def _(off=off, pc=pc):
                cp = pltpu.make_async_copy(
                    y_hbm.at[tile, pl.ds(pl.multiple_of(off * CHUNK, CHUNK), pc * CHUNK)],
                    ybuf.at[s_, pl.ds(pl.multiple_of(off * CHUNK, CHUNK), pc * CHUNK)], sem.at[s_])
                if start:
                    cp.start()
                else:
                    cp.wait()
            off = off + jnp.where(take, pc, 0)

    @pl.when(i == 0)
    def _():
        ybuf[...] = jnp.zeros_like(ybuf)
        fetch(0, 0, True)

    @pl.when(i + 1 < nt)
    def _():
        fetch(i + 1, 1 - sl, True)

    fetch(i, sl, False)
    iota_r = lax.broadcasted_iota(jnp.int32, (R_CAP, TS), 0)
    gt = jnp.zeros((R_CAP, TS), jnp.float32)
    for k in range(TOP_K):
        gt = jnp.where(iota_r == slot_ref[k:k + 1, :], gate_ref[k:k + 1, :], gt)
    moe = lax.dot_general(gt.astype(jnp.bfloat16), _unpack_halves(ybuf[sl]),
                          (((0,), (0,)), ((), ())), preferred_element_type=jnp.float32)
    o_ref[...] = _rms(x1_ref[...] + moe, gfin_ref[...])


def _combine(nvc, y_tiles, x1, slot_t, gate_t, g_final):
    t, d = x1.shape
    nt = t // TS
    grid_spec = pltpu.PrefetchScalarGridSpec(
        num_scalar_prefetch=1,
        grid=(nt,),
        in_specs=[
            pl.BlockSpec(memory_space=pl.ANY),
            pl.BlockSpec((TS, d), lambda i, nv: (i, 0)),
            pl.BlockSpec((TOP_K, TS), lambda i, nv: (0, i)),
            pl.BlockSpec((TOP_K, TS), lambda i, nv: (0, i)),
            pl.BlockSpec((1, d), lambda i, nv: (0, 0)),
        ],
        out_specs=pl.BlockSpec((TS, d), lambda i, nv: (i, 0)),
        scratch_shapes=[pltpu.VMEM((2, R_CAP, d // 2), jnp.uint32), pltpu.SemaphoreType.DMA((2,))],
    )
    return pl.pallas_call(
        _combine_kernel,
        grid_spec=grid_spec,
        out_shape=jax.ShapeDtypeStruct((t, d), jnp.float32),
        compiler_params=pltpu.CompilerParams(
            dimension_semantics=("arbitrary",), vmem_limit_bytes=VMEM_LIMIT),
        name="combine",
    )(nvc, y_tiles, x1, slot_t, gate_t, g_final)


def _chunk_plan(pcc, nb):
    nt = pcc.shape[0]
    i32 = jnp.int32
    pcc_t = pcc.T
    run_start_t = (jnp.cumsum(pcc, axis=1) - pcc).T
    cum_incl = jnp.cumsum(pcc_t, axis=1)
    eblocks = (cum_incl[:, -1] + CPB - 1) // CPB
    bstart = jnp.cumsum(eblocks) - eblocks
    blk = jnp.arange(nb, dtype=i32)
    bexp = jnp.sum(bstart[None, :] <= blk[:, None], axis=1).astype(i32) - 1
    oh = bexp[:, None] == jnp.arange(N_EXPERTS, dtype=i32)[None, :]
    pick = lambda tab: jnp.sum(jnp.where(oh[:, :, None], tab[None], 0), axis=1)
    ci, pc_row, rs_row = pick(cum_incl), pick(pcc_t), pick(run_start_t)
    b0 = jnp.sum(jnp.where(oh, bstart[None, :], 0), axis=1)
    p = (blk - b0)[:, None] * CPB + jnp.arange(CPB, dtype=i32)[None, :]
    before = ci[:, None, :] <= p[:, :, None]
    tile = jnp.sum(before, axis=2).astype(i32)
    cum_excl = jnp.sum(jnp.where(before, pc_row[:, None, :], 0), axis=2)
    at_tile = jnp.arange(nt, dtype=i32)[None, None, :] == tile[:, :, None]
    rs = jnp.sum(jnp.where(at_tile, rs_row[:, None, :], 0), axis=2)
    valid = tile < nt
    src = jnp.where(valid, tile * CPT + rs + (p - cum_excl), 0).astype(i32).reshape(-1)
    nvalid = jnp.sum(valid, axis=1).astype(i32)
    return src, bexp, nvalid


def kernel(x, g_mix, w_in, b_in, sinks, w_dw, b_dw, ln_g, ln_b, g_attn_out, g_conv_out, w_out, b_out,
           g_ffn, w_router, b_router, w1, b1, w2, b2, g_final):
    bsz, seq, d = x.shape
    t = bsz * seq
    nt = t // TS
    depth = g_mix.shape[0]
    assert depth == 1, "combine fuses the final RMSNorm, so exactly one layer is supported"
    f32 = jnp.float32
    max_chunks = nt * ((TS * TOP_K + N_EXPERTS * (CHUNK - 1)) // CHUNK) + N_EXPERTS * (CPB - 1)
    nb = -(-max_chunks // CPB)
    ci = jnp.arange(2 * LANES)
    dei = (ci[:, None] == jnp.where(ci < LANES, 2 * ci, 2 * (ci - LANES) + 1)[None, :]).astype(jnp.bfloat16)

    for l in range(depth):
        sink_cols = jnp.repeat(sinks[l].astype(f32).reshape(N_KV_HEADS, Q_PER_KV), WINDOW, axis=1)[..., None]
        x1 = _mix(x, g_mix[l][None], w_in[l].astype(jnp.bfloat16), b_in[l][None], sink_cols,
                  w_dw[l], b_dw[l][None], ln_g[l][None], ln_b[l][None],
                  g_attn_out[l][None], g_conv_out[l][None], w_out[l].astype(jnp.bfloat16), b_out[l][None])
        x1 = x1.reshape(t, d)
        xs, slot_t, gate_t, pcc = _route(x1, g_ffn[l][None], w_router[l].T, b_router[l][:, None])
        pcc = pcc[:, :, 0]
        src, bexp, nvalid = _chunk_plan(pcc, nb)
        b1l = b1[l].reshape(N_EXPERTS, 1, -1, 2)
        nvc = jnp.sum(pcc, axis=1).astype(jnp.int32)
        y = _experts(src, bexp, nvalid, nvc, xs.reshape(nt * CPT, CHUNK, d // 2), w1[l], b1l[..., 0], b1l[..., 1],
                     w2[l], b2[l][:, None, :], dei)
        x = _combine(nvc, y.reshape(nt, R_CAP, d // 2), x1, slot_t, gate_t, g_final[None]).reshape(bsz, seq, d)
    return x
```

```python
import functools

import jax
import jax.numpy as jnp
from jax import lax
from jax.experimental import pallas as pl
from jax.experimental.pallas import tpu as pltpu

HEAD_DIM = 64
N_Q_HEADS = 8
N_KV_HEADS = 2
Q_PER_KV = N_Q_HEADS // N_KV_HEADS
ATTN_WIDTH = N_Q_HEADS * HEAD_DIM
KV_WIDTH = N_KV_HEADS * HEAD_DIM
WINDOW = 128
CONV_KERNEL = 31
N_EXPERTS = 32
TOP_K = 4
SWIGLU_LIMIT = 7.0
SWIGLU_ALPHA = 1.702
RMS_EPS = 1e-5
LN_EPS = 1e-5

LANES = 128
SUBLANES = 8

TS = 256
CHUNK = SUBLANES
R_CAP = -(-(TS * TOP_K + N_EXPERTS * (CHUNK - 1)) // LANES) * LANES
CPT = R_CAP // CHUNK
TAIL_MAX = CPT - TS * TOP_K // CHUNK
_TAIL_PIECES = tuple(1 << i for i in reversed(range(TAIL_MAX.bit_length())))
TM = 256
CPB = TM // CHUNK
CONV_HALO = 32
CONV_ROWS = 64
NEG_BIG = -1e30
VMEM_LIMIT = 56 * 1024 * 1024


def _rms(x, g):
    return x * lax.rsqrt(jnp.mean(x * x, axis=-1, keepdims=True) + RMS_EPS) * g


def _dot(a, b):
    return jnp.dot(a, b, preferred_element_type=jnp.float32)


_HI16 = 0xFFFF0000


def _pack_halves(v):
    half = v.shape[1] // 2
    as_bits = lambda t: lax.bitcast_convert_type(t.astype(jnp.bfloat16).astype(jnp.float32), jnp.uint32)
    return (as_bits(v[:, half:]) & jnp.uint32(_HI16)) | (as_bits(v[:, :half]) >> 16)


def _unpack_halves(w):
    lo = lax.bitcast_convert_type(w << 16, jnp.float32).astype(jnp.bfloat16)
    hi = lax.bitcast_convert_type(w & jnp.uint32(_HI16), jnp.float32).astype(jnp.bfloat16)
    return jnp.concatenate([lo, hi], axis=1)


def _dot_nt(a, b, precision=None):
    return lax.dot_general(a, b, (((1,), (1,)), ((), ())), precision=precision,
                           preferred_element_type=jnp.float32)


def _mix_kernel(x_ref, gmix_ref, win_ref, bin_ref, sink_ref, wdw_ref, bdw_ref, lng_ref, lnb_ref,
                ga_ref, gc_ref, wout_ref, bout_ref, o_ref, kv_prev, u_buf, u_sh, conv_buf):
    s = pl.program_id(1)
    first = s == 0

    @pl.when(first)
    def _():
        kv_prev[...] = jnp.zeros_like(kv_prev)
        u_buf[0:CONV_HALO, :] = jnp.zeros((CONV_HALO, u_buf.shape[1]), jnp.float32)

    x = x_ref[0]
    u = _rms(x, gmix_ref[...])
    proj = _dot(u.astype(jnp.bfloat16), win_ref[...]) + bin_ref[...]
    o_q, o_k, o_v, o_a, o_g = 0, ATTN_WIDTH, ATTN_WIDTH + KV_WIDTH, ATTN_WIDTH + 2 * KV_WIDTH, \
        ATTN_WIDTH + 2 * KV_WIDTH + ATTN_WIDTH
    q = proj[:, o_q:o_k] * (HEAD_DIM ** -0.5)
    k = proj[:, o_k:o_v]
    v = proj[:, o_v:o_a]
    conv_a = proj[:, o_a:o_g]
    conv_gate = proj[:, o_g:]

    lane = lax.broadcasted_iota(jnp.int32, (WINDOW + TS, LANES), 1)
    low = lane < HEAD_DIM
    kfull = jnp.concatenate([kv_prev[:, 0:KV_WIDTH], k], axis=0)
    vfull = jnp.concatenate([kv_prev[:, KV_WIDTH:], v], axis=0)
    kroll = pltpu.roll(kfull, HEAD_DIM, axis=1)
    vroll = pltpu.roll(vfull, HEAD_DIM, axis=1)
    kk = [jnp.where(low, kfull, kroll).astype(jnp.bfloat16), jnp.where(low, kroll, kfull).astype(jnp.bfloat16)]
    vv = [jnp.where(low, vfull, vroll).astype(jnp.bfloat16), jnp.where(low, vroll, vfull).astype(jnp.bfloat16)]
    kv_prev[:, 0:KV_WIDTH] = k[TS - WINDOW:, :]
    kv_prev[:, KV_WIDTH:] = v[TS - WINDOW:, :]

    rows = Q_PER_KV * WINDOW
    qi = lax.broadcasted_iota(jnp.int32, (rows, 2 * WINDOW), 0) & (WINDOW - 1)
    kj = lax.broadcasted_iota(jnp.int32, (rows, 2 * WINDOW), 1)
    band = (kj > qi) & (kj <= qi + WINDOW)
    lane_q = lax.broadcasted_iota(jnp.int32, (WINDOW, LANES), 1)
    low_q = lane_q < HEAD_DIM
    ones_kv = jnp.ones((2 * WINDOW, LANES), jnp.bfloat16)

    attn_blocks = []
    for b in range(TS // WINDOW):
        qb = q[b * WINDOW:(b + 1) * WINDOW, :]
        valid = band & ((kj >= WINDOW) | jnp.logical_not(first & (b == 0)))
        head_out = []
        for g in range(N_KV_HEADS):
            parts = []
            for i in range(Q_PER_KV):
                h = g * Q_PER_KV + i
                q128 = qb[:, (h // 2) * LANES:(h // 2 + 1) * LANES]
                keep = low_q if h % 2 == 0 else jnp.logical_not(low_q)
                parts.append(jnp.where(keep, q128, 0.0))
            qs = jnp.concatenate(parts, axis=0).astype(jnp.bfloat16)
            kb = kk[g][b * WINDOW:b * WINDOW + 2 * WINDOW, :]
            vb = vv[g][b * WINDOW:b * WINDOW + 2 * WINDOW, :]
            sc = _dot_nt(qs, kb)
            sc = jnp.where(valid, sc, NEG_BIG)
            sink = sink_ref[g]
            m = jnp.maximum(jnp.max(sc, axis=-1, keepdims=True), sink)
            p = jnp.exp(sc - m).astype(jnp.bfloat16)
            den = _dot(p, ones_kv) + jnp.exp(sink - m)
            pv = _dot(p, vb)
            o = pv / den
            for i in range(Q_PER_KV):
                head_out.append(o[i * WINDOW:(i + 1) * WINDOW, :])
        cols = [jnp.where(low_q, head_out[2 * j], head_out[2 * j + 1]) for j in range(N_Q_HEADS // 2)]
        attn_blocks.append(jnp.concatenate(cols, axis=1))
    attn = jnp.concatenate(attn_blocks, axis=0)

    cw = u_buf.shape[1]
    u_buf[CONV_HALO:CONV_HALO + TS, :] = conv_a * jax.nn.sigmoid(conv_gate)
    n_sh = u_sh.shape[1]
    for s in range(1, SUBLANES):
        u_sh[s - 1] = u_buf[s:s + n_sh, :]
    shift = CONV_HALO - (CONV_KERNEL - 1)
    for c in range(cw // LANES):
        cs = slice(c * LANES, (c + 1) * LANES)
        for r0 in range(0, TS, CONV_ROWS):
            acc = jnp.zeros((CONV_ROWS, LANES), jnp.float32)
            for j in range(CONV_KERNEL):
                base, s = (shift + j) // SUBLANES * SUBLANES, (shift + j) % SUBLANES
                rs = slice(r0 + base, r0 + base + CONV_ROWS)
                win = u_buf[rs, cs] if s == 0 else u_sh[s - 1, rs, cs]
                acc = acc + wdw_ref[j:j + 1, cs] * win
            conv_buf[r0:r0 + CONV_ROWS, cs] = acc
    u_buf[0:CONV_HALO, :] = u_buf[TS:TS + CONV_HALO, :]
    cv = conv_buf[...] + bdw_ref[...]
    mu = jnp.mean(cv, axis=-1, keepdims=True)
    xc = cv - mu
    cv = xc * lax.rsqrt(jnp.mean(xc * xc, axis=-1, keepdims=True) + LN_EPS) * lng_ref[...] + lnb_ref[...]
    cv = cv * jax.nn.sigmoid(cv)

    mixed = jnp.concatenate([_rms(attn, ga_ref[...]), _rms(cv, gc_ref[...])], axis=1).astype(jnp.bfloat16)
    o_ref[0] = x + _dot(mixed, wout_ref[...]) + bout_ref[...]


def _mix(x, g_mix, w_in, b_in, sink_cols, w_dw, b_dw, ln_g, ln_b, g_a, g_c, w_out, b_out):
    bsz, seq, d = x.shape
    assert seq % TS == 0 and TS % WINDOW == 0 and CONV_HALO >= CONV_KERNEL - 1
    cw = w_dw.shape[1]
    const = lambda shape: pl.BlockSpec(shape, lambda b, s: (0,) * len(shape))
    return pl.pallas_call(
        _mix_kernel,
        grid=(bsz, seq // TS),
        in_specs=[
            pl.BlockSpec((1, TS, d), lambda b, s: (b, s, 0)),
            const((1, d)), const(w_in.shape), const((1, w_in.shape[1])),
            const(sink_cols.shape), const(w_dw.shape), const((1, cw)), const((1, cw)), const((1, cw)),
            const((1, ATTN_WIDTH)), const((1, cw)), const(w_out.shape), const((1, d)),
        ],
        out_specs=pl.BlockSpec((1, TS, d), lambda b, s: (b, s, 0)),
        out_shape=jax.ShapeDtypeStruct(x.shape, jnp.float32),
        scratch_shapes=[
            pltpu.VMEM((WINDOW, 2 * KV_WIDTH), jnp.float32),
            pltpu.VMEM((CONV_HALO + TS, cw), jnp.float32),
            pltpu.VMEM((SUBLANES - 1, CONV_HALO + TS - SUBLANES, cw), jnp.float32),
            pltpu.VMEM((TS, cw), jnp.float32),
        ],
        compiler_params=pltpu.CompilerParams(
            dimension_semantics=("arbitrary", "arbitrary"), vmem_limit_bytes=VMEM_LIMIT),
        name="mix",
    )(x, g_mix, w_in, b_in, sink_cols, w_dw, b_dw, ln_g, ln_b, g_a, g_c, w_out, b_out)


def _route_kernel(x1_ref, gffn_ref, whi_ref, wlo_ref, brt_ref, xs_ref, slot_ref, gate_ref, pcc_ref):
    h = _rms(x1_ref[...], gffn_ref[...])
    hb = h.astype(jnp.bfloat16)
    h_lo = (h - hb.astype(jnp.float32)).astype(jnp.bfloat16)
    lg_t = _dot(hb, whi_ref[...]) + (_dot(hb, wlo_ref[...]) + _dot(h_lo, whi_ref[...]))
    lg = lg_t.T[0:N_EXPERTS, :] + brt_ref[...]
    iota_e = lax.broadcasted_iota(jnp.int32, (N_EXPERTS, TS), 0)
    vals, hots = [], []
    member = jnp.zeros((N_EXPERTS, TS), jnp.float32)
    for _ in range(TOP_K):
        mx = jnp.max(lg, axis=0, keepdims=True)
        idx = jnp.min(jnp.where(lg == mx, iota_e, N_EXPERTS), axis=0, keepdims=True)
        hot = iota_e == idx
        lg = jnp.where(hot, -jnp.inf, lg)
        member = member + hot.astype(jnp.float32)
        vals.append(mx)
        hots.append(hot)
    ex = [jnp.exp(vk - vals[0]) for vk in vals]
    den = ex[0] + ex[1] + ex[2] + ex[3]
    gate_ref[...] = jnp.concatenate([e / den for e in ex], axis=0)

    ti = lax.broadcasted_iota(jnp.int32, (TS, TS), 0)
    tj = lax.broadcasted_iota(jnp.int32, (TS, TS), 1)
    upper = jnp.where(ti < tj, 1.0, 0.0).astype(jnp.bfloat16)
    cum = _dot(member.astype(jnp.bfloat16), upper)
    cnt = jnp.sum(member, axis=1, keepdims=True)
    pcc = jnp.floor((cnt + (CHUNK - 1)) * (1.0 / CHUNK))
    pcb = jnp.broadcast_to(pcc, (N_EXPERTS, LANES))
    row_e = lax.broadcasted_iota(jnp.int32, (N_EXPERTS, LANES), 0)
    inc = pcb
    sh = 1
    while sh < N_EXPERTS:
        inc = inc + jnp.where(row_e >= sh, pltpu.roll(inc, sh, axis=0), 0.0)
        sh *= 2
    run_start = (inc - pcb)[:, 0:1] * CHUNK
    pcc_ref[0] = pcb.astype(jnp.int32)

    pos = run_start + cum
    slots = [jnp.sum(jnp.where(hot, pos, 0.0), axis=0, keepdims=True).astype(jnp.int32) for hot in hots]
    slot_ref[...] = jnp.concatenate(slots, axis=0)

    iota_r = lax.broadcasted_iota(jnp.int32, (R_CAP, TS), 0)
    sel = (iota_r == slots[0]) | (iota_r == slots[1]) | (iota_r == slots[2]) | (iota_r == slots[3])
    perm = jnp.where(sel, 1.0, 0.0).astype(jnp.bfloat16)
    xs_ref[0] = _pack_halves(_dot(perm, hb))


def _route(x1, g_ffn, w_router, b_router):
    t, d = x1.shape
    nt = t // TS
    w_pad = jnp.pad(w_router.astype(jnp.float32), ((0, 0), (0, LANES - N_EXPERTS)))
    w_hi = w_pad.astype(jnp.bfloat16)
    w_lo = (w_pad - w_hi.astype(jnp.float32)).astype(jnp.bfloat16)
    br_t = b_router.astype(jnp.float32)[:, None]
    const = lambda shape: pl.BlockSpec(shape, lambda i: (0,) * len(shape))
    return pl.pallas_call(
        _route_kernel,
        grid=(nt,),
        in_specs=[pl.BlockSpec((TS, d), lambda i: (i, 0)), const((1, d)), const(w_hi.shape), const(w_lo.shape),
                  const(br_t.shape)],
        out_specs=[
            pl.BlockSpec((1, R_CAP, d // 2), lambda i: (i, 0, 0)),
            pl.BlockSpec((TOP_K, TS), lambda i: (0, i)),
            pl.BlockSpec((TOP_K, TS), lambda i: (0, i)),
            pl.BlockSpec((1, N_EXPERTS, LANES), lambda i: (i, 0, 0)),
        ],
        out_shape=[
            jax.ShapeDtypeStruct((nt, R_CAP, d // 2), jnp.uint32),
            jax.ShapeDtypeStruct((TOP_K, t), jnp.int32),
            jax.ShapeDtypeStruct((TOP_K, t), jnp.float32),
            jax.ShapeDtypeStruct((nt, N_EXPERTS, LANES), jnp.int32),
        ],
        compiler_params=pltpu.CompilerParams(
            dimension_semantics=("arbitrary",), vmem_limit_bytes=VMEM_LIMIT),
        name="route",
    )(x1, g_ffn, w_hi, w_lo, br_t)


def _experts_kernel(src_ref, bexp_ref, nval_ref, nvc_ref,
                    xs_hbm, w1_ref, b1g_ref, b1l_ref, w2_ref, b2_ref, dei_ref,
                    y_hbm,
                    xbuf, ybuf, zbuf, w1g, w1l, w2b, sem_in, sem_out, sem_z):
    b = pl.program_id(0)
    nb = pl.num_programs(0)
    n_tiles = nvc_ref.shape[0]
    slot = b % 2

    def zero_tail(tile, start):
        n = CPT - nvc_ref[tile]
        off = tile * CPT + nvc_ref[tile]
        for pc in _TAIL_PIECES:
            take = (n & pc) != 0

            @pl.when(take)
            def _(off=off, pc=pc):
                cp = pltpu.make_async_copy(zbuf.at[pl.ds(0, pc)], y_hbm.at[pl.ds(off, pc)], sem_z)
                if start:
                    cp.start()
                else:
                    cp.wait()
            off = off + jnp.where(take, pc, 0)

    @pl.when(b == 0)
    def _():
        zbuf[...] = jnp.zeros_like(zbuf)

    @pl.when(b < n_tiles)
    def _():
        zero_tail(b, True)

    def chunk_copy(blk, sl, c, inbound):
        if inbound:
            return pltpu.make_async_copy(xs_hbm.at[src_ref[blk * CPB + c]], xbuf.at[sl, c], sem_in.at[sl])
        return pltpu.make_async_copy(ybuf.at[sl, c], y_hbm.at[src_ref[blk * CPB + c]], sem_out.at[sl])

    def start_all(blk, sl, inbound, static):
        if static:
            for c in range(CPB):
                chunk_copy(blk, sl, c, inbound).start()
        else:
            def body(c, carry):
                chunk_copy(blk, sl, c, inbound).start()
                return carry
            lax.fori_loop(0, nval_ref[blk], body, 0)

    def wait_all(blk, sl, inbound):
        n = nval_ref[blk]

        @pl.when(n == CPB)
        def _():
            if inbound:
                pltpu.make_async_copy(xs_hbm.at[pl.ds(0, CPB)], xbuf.at[sl], sem_in.at[sl]).wait()
            else:
                pltpu.make_async_copy(ybuf.at[sl], y_hbm.at[pl.ds(0, CPB)], sem_out.at[sl]).wait()

        @pl.when(n != CPB)
        def _():
            def body(c, carry):
                chunk_copy(blk, sl, c, inbound).wait()
                return carry
            lax.fori_loop(0, n, body, 0)

    def compute():
        xb = _unpack_halves(xbuf[slot].reshape(TM, xbuf.shape[3]))
        hg = _dot(xb, w1g[...]) + b1g_ref[0]
        hl = _dot(xb, w1l[...]) + b1l_ref[0]
        hg = jnp.minimum(hg, SWIGLU_LIMIT)
        hl = jnp.clip(hl, -SWIGLU_LIMIT, SWIGLU_LIMIT)
        act = hg * jax.nn.sigmoid(SWIGLU_ALPHA * hg) * (hl + 1.0)
        y = _dot(act.astype(jnp.bfloat16), w2b[...]) + b2_ref[0]
        ybuf[slot] = _pack_halves(y).reshape(CPB, CHUNK, y.shape[1] // 2)

    @pl.when(b == 0)
    def _():
        xbuf[...] = jnp.zeros_like(xbuf)
        start_all(0, 0, True, False)

    prev_e = bexp_ref[jnp.maximum(b - 1, 0)]

    @pl.when((b == 0) | (bexp_ref[b] != prev_e))
    def _():
        dei = dei_ref[...]
        for j in range(w1_ref.shape[2] // (2 * LANES)):
            wj = w1_ref[0, :, j * 2 * LANES:(j + 1) * 2 * LANES].astype(jnp.bfloat16)
            r = _dot(wj, dei)
            w1g[:, j * LANES:(j + 1) * LANES] = r[:, 0:LANES].astype(jnp.bfloat16)
            w1l[:, j * LANES:(j + 1) * LANES] = r[:, LANES:].astype(jnp.bfloat16)
        w2b[...] = w2_ref[0].astype(jnp.bfloat16)

    @pl.when(b >= 2)
    def _():
        wait_all(b - 2, slot, False)

    wait_all(b, slot, True)

    prev_n = nval_ref[jnp.maximum(b - 1, 0)]
    next_n = nval_ref[jnp.minimum(b + 1, nb - 1)]
    fast = (b >= 1) & (prev_n == CPB) & (b + 1 < nb) & (next_n == CPB) & (nval_ref[b] > 0)

    @pl.when(fast)
    def _():
        start_all(b - 1, 1 - slot, False, True)
        start_all(b + 1, 1 - slot, True, True)
        compute()

    @pl.when(jnp.logical_not(fast))
    def _():
        @pl.when(b >= 1)
        def _():
            start_all(b - 1, 1 - slot, False, False)

        @pl.when(b + 1 < nb)
        def _():
            start_all(b + 1, 1 - slot, True, False)

        @pl.when(nval_ref[b] > 0)
        def _():
            compute()

    @pl.when(b < n_tiles)
    def _():
        zero_tail(b, False)

    @pl.when(b == nb - 1)
    def _():
        start_all(b, slot, False, False)

        @pl.when(b >= 1)
        def _():
            wait_all(b - 1, 1 - slot, False)
        wait_all(b, slot, False)


def _experts(chunk_src, block_expert, block_nvalid, nvc, xs_chunks, w1, b1g, b1l, w2, b2, dei):
    n_chunks, _, dh = xs_chunks.shape
    nb = block_expert.shape[0]
    assert nb >= nvc.shape[0]
    d, de2 = w1.shape[1], w1.shape[2]
    grid_spec = pltpu.PrefetchScalarGridSpec(
        num_scalar_prefetch=4,
        grid=(nb,),
        in_specs=[
            pl.BlockSpec(memory_space=pl.ANY),
            pl.BlockSpec((1, d, de2), lambda i, src, be, nv, nc: (be[i], 0, 0)),
            pl.BlockSpec((1, 1, de2 // 2), lambda i, src, be, nv, nc: (be[i], 0, 0)),
            pl.BlockSpec((1, 1, de2 // 2), lambda i, src, be, nv, nc: (be[i], 0, 0)),
            pl.BlockSpec((1, de2 // 2, d), lambda i, src, be, nv, nc: (be[i], 0, 0)),
            pl.BlockSpec((1, 1, d), lambda i, src, be, nv, nc: (be[i], 0, 0)),
            pl.BlockSpec(dei.shape, lambda i, src, be, nv, nc: (0, 0)),
        ],
        out_specs=pl.BlockSpec(memory_space=pl.ANY),
        scratch_shapes=[
            pltpu.VMEM((2, CPB, CHUNK, dh), jnp.uint32),
            pltpu.VMEM((2, CPB, CHUNK, dh), jnp.uint32),
            pltpu.VMEM((TAIL_MAX, CHUNK, dh), jnp.uint32),
            pltpu.VMEM((d, de2 // 2), jnp.bfloat16),
            pltpu.VMEM((d, de2 // 2), jnp.bfloat16),
            pltpu.VMEM((de2 // 2, d), jnp.bfloat16),
            pltpu.SemaphoreType.DMA((2,)),
            pltpu.SemaphoreType.DMA((2,)),
            pltpu.SemaphoreType.DMA(()),
        ],
    )
    return pl.pallas_call(
        _experts_kernel,
        grid_spec=grid_spec,
        out_shape=jax.ShapeDtypeStruct(xs_chunks.shape, jnp.uint32),
        compiler_params=pltpu.CompilerParams(
            dimension_semantics=("arbitrary",), vmem_limit_bytes=VMEM_LIMIT),
        name="experts",
    )(chunk_src, block_expert, block_nvalid, nvc, xs_chunks, w1, b1g, b1l, w2, b2, dei)


_PIECES = tuple(1 << i for i in reversed(range((CPT).bit_length())))


def _combine_kernel(nvc_ref,
                    y_hbm, x1_ref, slot_ref, gate_ref, gfin_ref, o_ref, ybuf, sem):
    i = pl.program_id(0)
    nt = pl.num_programs(0)
    sl = i % 2

    def fetch(tile, s_, start):
        n = nvc_ref[tile]
        off = jnp.int32(0)
        for pc in _PIECES:
            if pc * CHUNK > R_CAP:
                continue
            take = (n & pc) != 0

            @pl.when(take)
            def _(off=off, pc=pc):
                cp = pltpu.make_async_copy(
                    y_hbm.at[tile, pl.ds(pl.multiple_of(off * CHUNK, CHUNK), pc * CHUNK)],
                    ybuf.at[s_, pl.ds(pl.multiple_of(off * CHUNK, CHUNK), pc * CHUNK)], sem.at[s_])
                if start:
                    cp.start()
                else:
                    cp.wait()
            off = off + jnp.where(take, pc, 0)

    @pl.when(i == 0)
    def _():
        ybuf[...] = jnp.zeros_like(ybuf)
        fetch(0, 0, True)

    @pl.when(i + 1 < nt)
    def _():
        fetch(i + 1, 1 - sl, True)

    fetch(i, sl, False)
    iota_r = lax.broadcasted_iota(jnp.int32, (R_CAP, TS), 0)
    gt = jnp.zeros((R_CAP, TS), jnp.float32)
    for k in range(TOP_K):
        gt = jnp.where(iota_r == slot_ref[k:k + 1, :], gate_ref[k:k + 1, :], gt)
    moe = lax.dot_general(gt.astype(jnp.bfloat16), _unpack_halves(ybuf[sl]),
                          (((0,), (0,)), ((), ())), preferred_element_type=jnp.float32)
    o_ref[...] = _rms(x1_ref[...] + moe, gfin_ref[...])


def _combine(nvc, y_tiles, x1, slot_t, gate_t, g_final):
    t, d = x1.shape
    nt = t // TS
    grid_spec = pltpu.PrefetchScalarGridSpec(
        num_scalar_prefetch=1,
        grid=(nt,),
        in_specs=[
            pl.BlockSpec(memory_space=pl.ANY),
            pl.BlockSpec((TS, d), lambda i, nv: (i, 0)),
            pl.BlockSpec((TOP_K, TS), lambda i, nv: (0, i)),
            pl.BlockSpec((TOP_K, TS), lambda i, nv: (0, i)),
            pl.BlockSpec((1, d), lambda i, nv: (0, 0)),
        ],
        out_specs=pl.BlockSpec((TS, d), lambda i, nv: (i, 0)),
        scratch_shapes=[pltpu.VMEM((2, R_CAP, d // 2), jnp.uint32), pltpu.SemaphoreType.DMA((2,))],
    )
    return pl.pallas_call(
        _combine_kernel,
        grid_spec=grid_spec,
        out_shape=jax.ShapeDtypeStruct((t, d), jnp.float32),
        compiler_params=pltpu.CompilerParams(
            dimension_semantics=("arbitrary",), vmem_limit_bytes=VMEM_LIMIT),
        name="combine",
    )(nvc, y_tiles, x1, slot_t, gate_t, g_final)


def _chunk_plan(pcc, nb):
    nt = pcc.shape[0]
    i32 = jnp.int32
    pcc_t = pcc.T
    run_start_t = (jnp.cumsum(pcc, axis=1) - pcc).T
    cum_incl = jnp.cumsum(pcc_t, axis=1)
    eblocks = (cum_incl[:, -1] + CPB - 1) // CPB
    bstart = jnp.cumsum(eblocks) - eblocks
    blk = jnp.arange(nb, dtype=i32)
    bexp = jnp.sum(bstart[None, :] <= blk[:, None], axis=1).astype(i32) - 1
    oh = bexp[:, None] == jnp.arange(N_EXPERTS, dtype=i32)[None, :]
    pick = lambda tab: jnp.sum(jnp.where(oh[:, :, None], tab[None], 0), axis=1)
    ci, pc_row, rs_row = pick(cum_incl), pick(pcc_t), pick(run_start_t)
    b0 = jnp.sum(jnp.where(oh, bstart[None, :], 0), axis=1)
    p = (blk - b0)[:, None] * CPB + jnp.arange(CPB, dtype=i32)[None, :]
    before = ci[:, None, :] <= p[:, :, None]
    tile = jnp.sum(before, axis=2).astype(i32)
    cum_excl = jnp.sum(jnp.where(before, pc_row[:, None, :], 0), axis=2)
    at_tile = jnp.arange(nt, dtype=i32)[None, None, :] == tile[:, :, None]
    rs = jnp.sum(jnp.where(at_tile, rs_row[:, None, :], 0), axis=2)
    valid = tile < nt
    src = jnp.where(valid, tile * CPT + rs + (p - cum_excl), 0).astype(i32).reshape(-1)
    nvalid = jnp.sum(valid, axis=1).astype(i32)
    return src, bexp, nvalid


def kernel(x, g_mix, w_in, b_in, sinks, w_dw, b_dw, ln_g, ln_b, g_attn_out, g_conv_out, w_out, b_out,
           g_ffn, w_router, b_router, w1, b1, w2, b2, g_final):
    bsz, seq, d = x.shape
    t = bsz * seq
    nt = t // TS
    depth = g_mix.shape[0]
    assert depth == 1, "combine fuses the final RMSNorm, so exactly one layer is supported"
    f32 = jnp.float32
    max_chunks = nt * ((TS * TOP_K + N_EXPERTS * (CHUNK - 1)) // CHUNK) + N_EXPERTS * (CPB - 1)
    nb = -(-max_chunks // CPB)
    ci = jnp.arange(2 * LANES)
    dei = (ci[:, None] == jnp.where(ci < LANES, 2 * ci, 2 * (ci - LANES) + 1)[None, :]).astype(jnp.bfloat16)

    for l in range(depth):
        sink_cols = jnp.repeat(sinks[l].astype(f32).reshape(N_KV_HEADS, Q_PER_KV), WINDOW, axis=1)[..., None]
        x1 = _mix(x, g_mix[l][None], w_in[l].astype(jnp.bfloat16), b_in[l][None], sink_cols,
                  w_dw[l], b_dw[l][None], ln_g[l][None], ln_b[l][None],
                  g_attn_out[l][None], g_conv_out[l][None], w_out[l].astype(jnp.bfloat16), b_out[l][None])
        x1 = x1.reshape(t, d)
        xs, slot_t, gate_t, pcc = _route(x1, g_ffn[l][None], w_router[l], b_router[l])
        pcc = pcc[:, :, 0]
        src, bexp, nvalid = _chunk_plan(pcc, nb)
        b1l = b1[l].reshape(N_EXPERTS, 1, -1, 2)
        nvc = jnp.sum(pcc, axis=1).astype(jnp.int32)
        y = _experts(src, bexp, nvalid, nvc, xs.reshape(nt * CPT, CHUNK, d // 2), w1[l], b1l[..., 0], b1l[..., 1],
                     w2[l], b2[l][:, None, :], dei)
        x = _combine(nvc, y.reshape(nt, R_CAP, d // 2), x1, slot_t, gate_t, g_final[None]).reshape(bsz, seq, d)
    return x
```

```python
import functools

import jax
import jax.numpy as jnp
from jax import lax
from jax.experimental import pallas as pl
from jax.experimental.pallas import tpu as pltpu

HEAD_DIM = 64
N_Q_HEADS = 8
N_KV_HEADS = 2
Q_PER_KV = N_Q_HEADS // N_KV_HEADS
ATTN_WIDTH = N_Q_HEADS * HEAD_DIM
KV_WIDTH = N_KV_HEADS * HEAD_DIM
WINDOW = 128
CONV_KERNEL = 31
N_EXPERTS = 32
TOP_K = 4
SWIGLU_LIMIT = 7.0
SWIGLU_ALPHA = 1.702
RMS_EPS = 1e-5
LN_EPS = 1e-5

LANES = 128
SUBLANES = 8

TS = 256
CHUNK = SUBLANES
R_CAP = -(-(TS * TOP_K + N_EXPERTS * (CHUNK - 1)) // LANES) * LANES
CPT = R_CAP // CHUNK
TAIL_MAX = CPT - TS * TOP_K // CHUNK
_TAIL_PIECES = tuple(1 << i for i in reversed(range(TAIL_MAX.bit_length())))
MIX_SEQS = 2
MIX_COLS = 256
MIX_FILL_EVERY = 2
TM = 256
CPB = TM // CHUNK
CONV_HALO = 32
CONV_ROWS = 64
NEG_BIG = -1e30
VMEM_LIMIT = 56 * 1024 * 1024


def _rms(x, g):
    return x * lax.rsqrt(jnp.mean(x * x, axis=-1, keepdims=True) + RMS_EPS) * g


def _dot(a, b):
    return jnp.dot(a, b, preferred_element_type=jnp.float32)


_HI16 = 0xFFFF0000


def _pack_halves(v):
    half = v.shape[1] // 2
    as_bits = lambda t: lax.bitcast_convert_type(t.astype(jnp.bfloat16).astype(jnp.float32), jnp.uint32)
    return (as_bits(v[:, half:]) & jnp.uint32(_HI16)) | (as_bits(v[:, :half]) >> 16)


def _unpack_halves(w):
    lo = lax.bitcast_convert_type(w << 16, jnp.float32).astype(jnp.bfloat16)
    hi = lax.bitcast_convert_type(w & jnp.uint32(_HI16), jnp.float32).astype(jnp.bfloat16)
    return jnp.concatenate([lo, hi], axis=1)


def _dot_nt(a, b, precision=None):
    return lax.dot_general(a, b, (((1,), (1,)), ((), ())), precision=precision,
                           preferred_element_type=jnp.float32)


def _mix_kernel(x_ref, gmix_ref, win_ref, bin_ref, sink_ref, wdw_ref, bdw_ref, lng_ref, lnb_ref,
                ga_ref, gc_ref, wout_ref, bout_ref, o_ref, *scratch):
    params = (gmix_ref, win_ref, bin_ref, sink_ref, wdw_ref, bdw_ref, lng_ref, lnb_ref,
              ga_ref, gc_ref, wout_ref, bout_ref)
    per_seq = len(scratch) // MIX_SEQS
    seqs = [scratch[i * per_seq:(i + 1) * per_seq] for i in range(MIX_SEQS)]
    first = pl.program_id(1) == 0

    @pl.when(first)
    def _():
        for kv_prev, u_buf, *_ in seqs:
            kv_prev[...] = jnp.zeros_like(kv_prev)
            u_buf[0:CONV_HALO, :] = jnp.zeros((CONV_HALO, u_buf.shape[1]), jnp.float32)

    _emit_pipelined([_Stream(_mix_tile(x_ref.at[i], o_ref.at[i], first, params, *seqs[i]))
                     for i in range(MIX_SEQS)])


class _Stream:
    def __init__(self, gen):
        self.gen = gen
        self.tag = next(gen, None)

    def step(self):
        self.tag = next(self.gen, None)

    def run_while(self, tag):
        while self.tag == tag:
            self.step()


def _emit_pipelined(streams):
    streams[0].run_while("head")
    for i, cur in enumerate(streams):
        prv = streams[i - 1] if i >= 1 else None
        nxt = streams[i + 1] if i + 1 < len(streams) else None
        n = 0
        while cur.tag == "body":
            cur.step()
            n += 1
            if n % MIX_FILL_EVERY == 0:
                if prv is not None and prv.tag == "tail":
                    prv.step()
                elif nxt is not None and nxt.tag == "head":
                    nxt.step()
        if prv is not None:
            prv.run_while("tail")
        if nxt is not None:
            nxt.run_while("head")
    streams[-1].run_while("tail")


def _mix_tile(x_ref, o_ref, first, params, kv_prev, u_buf, u_sh, conv_buf, pbuf, mbuf):
    gmix_ref, win_ref, bin_ref, sink_ref, wdw_ref, bdw_ref, lng_ref, lnb_ref, ga_ref, gc_ref, wout_ref, bout_ref = params
    ub = None
    for c in range(win_ref.shape[1] // MIX_COLS):
        yield "head"
        if ub is None:
            ub = _rms(x_ref[0], gmix_ref[...]).astype(jnp.bfloat16)
        cs = slice(c * MIX_COLS, (c + 1) * MIX_COLS)
        pbuf[:, cs] = _dot(ub, win_ref[:, cs]) + bin_ref[:, cs]

    yield "body"
    o_q, o_k, o_v, o_a, o_g = 0, ATTN_WIDTH, ATTN_WIDTH + KV_WIDTH, ATTN_WIDTH + 2 * KV_WIDTH, \
        ATTN_WIDTH + 2 * KV_WIDTH + ATTN_WIDTH
    q = pbuf[:, o_q:o_k] * (HEAD_DIM ** -0.5)
    k = pbuf[:, o_k:o_v]
    v = pbuf[:, o_v:o_a]

    lane = lax.broadcasted_iota(jnp.int32, (WINDOW + TS, LANES), 1)
    low = lane < HEAD_DIM
    kfull = jnp.concatenate([kv_prev[:, 0:KV_WIDTH], k], axis=0)
    vfull = jnp.concatenate([kv_prev[:, KV_WIDTH:], v], axis=0)
    kroll = pltpu.roll(kfull, HEAD_DIM, axis=1)
    vroll = pltpu.roll(vfull, HEAD_DIM, axis=1)
    kk = [jnp.where(low, kfull, kroll).astype(jnp.bfloat16), jnp.where(low, kroll, kfull).astype(jnp.bfloat16)]
    vv = [jnp.where(low, vfull, vroll).astype(jnp.bfloat16), jnp.where(low, vroll, vfull).astype(jnp.bfloat16)]
    kv_prev[:, 0:KV_WIDTH] = k[TS - WINDOW:, :]
    kv_prev[:, KV_WIDTH:] = v[TS - WINDOW:, :]

    rows = Q_PER_KV * WINDOW
    qi = lax.broadcasted_iota(jnp.int32, (rows, 2 * WINDOW), 0) & (WINDOW - 1)
    kj = lax.broadcasted_iota(jnp.int32, (rows, 2 * WINDOW), 1)
    band = (kj > qi) & (kj <= qi + WINDOW)
    lane_q = lax.broadcasted_iota(jnp.int32, (WINDOW, LANES), 1)
    low_q = lane_q < HEAD_DIM
    ones_kv = jnp.ones((2 * WINDOW, LANES), jnp.bfloat16)

    def attn_unit(b, g):
        qb = q[b * WINDOW:(b + 1) * WINDOW, :]
        valid = band & ((kj >= WINDOW) | jnp.logical_not(first & (b == 0)))
        parts = []
        for i in range(Q_PER_KV):
            h = g * Q_PER_KV + i
            q128 = qb[:, (h // 2) * LANES:(h // 2 + 1) * LANES]
            keep = low_q if h % 2 == 0 else jnp.logical_not(low_q)
            parts.append(jnp.where(keep, q128, 0.0))
        qs = jnp.concatenate(parts, axis=0).astype(jnp.bfloat16)
        kb = kk[g][b * WINDOW:b * WINDOW + 2 * WINDOW, :]
        vb = vv[g][b * WINDOW:b * WINDOW + 2 * WINDOW, :]
        sc = _dot_nt(qs, kb)
        sc = jnp.where(valid, sc, NEG_BIG)
        sink = sink_ref[g]
        m = jnp.maximum(jnp.max(sc, axis=-1, keepdims=True), sink)
        p = jnp.exp(sc - m).astype(jnp.bfloat16)
        den = _dot(p, ones_kv) + jnp.exp(sink - m)
        pv = _dot(p, vb)
        o = pv / den
        return [o[i * WINDOW:(i + 1) * WINDOW, :] for i in range(Q_PER_KV)]

    cw = u_buf.shape[1]
    u_buf[CONV_HALO:CONV_HALO + TS, :] = pbuf[:, o_a:o_g] * jax.nn.sigmoid(pbuf[:, o_g:])
    n_sh = u_sh.shape[1]
    for s in range(1, SUBLANES):
        u_sh[s - 1] = u_buf[s:s + n_sh, :]
    shift = CONV_HALO - (CONV_KERNEL - 1)

    def conv_strip(c, r0):
        cs = slice(c * LANES, (c + 1) * LANES)
        acc = jnp.zeros((CONV_ROWS, LANES), jnp.float32)
        for j in range(CONV_KERNEL):
            base, s = (shift + j) // SUBLANES * SUBLANES, (shift + j) % SUBLANES
            rs = slice(r0 + base, r0 + base + CONV_ROWS)
            win = u_buf[rs, cs] if s == 0 else u_sh[s - 1, rs, cs]
            acc = acc + wdw_ref[j:j + 1, cs] * win
        conv_buf[r0:r0 + CONV_ROWS, cs] = acc

    units = [(b, g) for b in range(TS // WINDOW) for g in range(N_KV_HEADS)]
    strips = [(c, r0) for c in range(cw // LANES) for r0 in range(0, TS, CONV_ROWS)]
    per_unit = -(-len(strips) // len(units))
    head_out = {}
    for n, (b, g) in enumerate(units):
        yield "body"
        head_out[b, g] = attn_unit(b, g)
        for c, r0 in strips[n * per_unit:(n + 1) * per_unit]:
            yield "body"
            conv_strip(c, r0)

    yield "body"
    attn_blocks = []
    for b in range(TS // WINDOW):
        heads = [o for g in range(N_KV_HEADS) for o in head_out[b, g]]
        cols = [jnp.where(low_q, heads[2 * j], heads[2 * j + 1]) for j in range(N_Q_HEADS // 2)]
        attn_blocks.append(jnp.concatenate(cols, axis=1))
    attn = jnp.concatenate(attn_blocks, axis=0)

    u_buf[0:CONV_HALO, :] = u_buf[TS:TS + CONV_HALO, :]
    cv = conv_buf[...] + bdw_ref[...]
    mu = jnp.mean(cv, axis=-1, keepdims=True)
    xc = cv - mu
    cv = xc * lax.rsqrt(jnp.mean(xc * xc, axis=-1, keepdims=True) + LN_EPS) * lng_ref[...] + lnb_ref[...]
    cv = cv * jax.nn.sigmoid(cv)

    mbuf[...] = jnp.concatenate([_rms(attn, ga_ref[...]), _rms(cv, gc_ref[...])], axis=1).astype(jnp.bfloat16)

    for c in range(wout_ref.shape[1] // MIX_COLS):
        yield "tail"
        cs = slice(c * MIX_COLS, (c + 1) * MIX_COLS)
        o_ref[0, :, cs] = x_ref[0, :, cs] + _dot(mbuf[...], wout_ref[:, cs]) + bout_ref[:, cs]


def _mix(x, g_mix, w_in, b_in, sink_cols, w_dw, b_dw, ln_g, ln_b, g_a, g_c, w_out, b_out):
    bsz, seq, d = x.shape
    assert seq % TS == 0 and TS % WINDOW == 0 and CONV_HALO >= CONV_KERNEL - 1 and bsz % MIX_SEQS == 0
    cw = w_dw.shape[1]
    const = lambda shape: pl.BlockSpec(shape, lambda b, s: (0,) * len(shape))
    x = x.reshape(MIX_SEQS, bsz // MIX_SEQS, seq, d)
    seq_scratch = [
        pltpu.VMEM((WINDOW, 2 * KV_WIDTH), jnp.float32),
        pltpu.VMEM((CONV_HALO + TS, cw), jnp.float32),
        pltpu.VMEM((SUBLANES - 1, CONV_HALO + TS - SUBLANES, cw), jnp.float32),
        pltpu.VMEM((TS, cw), jnp.float32),
        pltpu.VMEM((TS, w_in.shape[1]), jnp.float32),
        pltpu.VMEM((TS, w_out.shape[0]), jnp.bfloat16),
    ]
    return pl.pallas_call(
        _mix_kernel,
        grid=(bsz // MIX_SEQS, seq // TS),
        in_specs=[
            pl.BlockSpec((MIX_SEQS, 1, TS, d), lambda b, s: (0, b, s, 0)),
            const((1, d)), const(w_in.shape), const((1, w_in.shape[1])),
            const(sink_cols.shape), const(w_dw.shape), const((1, cw)), const((1, cw)), const((1, cw)),
            const((1, ATTN_WIDTH)), const((1, cw)), const(w_out.shape), const((1, d)),
        ],
        out_specs=pl.BlockSpec((MIX_SEQS, 1, TS, d), lambda b, s: (0, b, s, 0)),
        out_shape=jax.ShapeDtypeStruct(x.shape, jnp.float32),
        scratch_shapes=seq_scratch * MIX_SEQS,
        compiler_params=pltpu.CompilerParams(
            dimension_semantics=("arbitrary", "arbitrary"), vmem_limit_bytes=VMEM_LIMIT),
        name="mix",
    )(x, g_mix, w_in, b_in, sink_cols, w_dw, b_dw, ln_g, ln_b, g_a, g_c, w_out, b_out).reshape(bsz, seq, d)


def _route_kernel(x1_ref, gffn_ref, whi_ref, wlo_ref, brt_ref, xs_ref, slot_ref, gate_ref, pcc_ref):
    h = _rms(x1_ref[...], gffn_ref[...])
    hb = h.astype(jnp.bfloat16)
    h_lo = (h - hb.astype(jnp.float32)).astype(jnp.bfloat16)
    lg_t = _dot(hb, whi_ref[...]) + (_dot(hb, wlo_ref[...]) + _dot(h_lo, whi_ref[...]))
    lg = lg_t.T[0:N_EXPERTS, :] + brt_ref[...]
    iota_e = lax.broadcasted_iota(jnp.int32, (N_EXPERTS, TS), 0)
    vals, hots = [], []
    member = jnp.zeros((N_EXPERTS, TS), jnp.float32)
    for _ in range(TOP_K):
        mx = jnp.max(lg, axis=0, keepdims=True)
        idx = jnp.min(jnp.where(lg == mx, iota_e, N_EXPERTS), axis=0, keepdims=True)
        hot = iota_e == idx
        lg = jnp.where(hot, -jnp.inf, lg)
        member = member + hot.astype(jnp.float32)
        vals.append(mx)
        hots.append(hot)
    ex = [jnp.exp(vk - vals[0]) for vk in vals]
    den = ex[0] + ex[1] + ex[2] + ex[3]
    gate_ref[...] = jnp.concatenate([e / den for e in ex], axis=0)

    ti = lax.broadcasted_iota(jnp.int32, (TS, TS), 0)
    tj = lax.broadcasted_iota(jnp.int32, (TS, TS), 1)
    upper = jnp.where(ti < tj, 1.0, 0.0).astype(jnp.bfloat16)
    cum = _dot(member.astype(jnp.bfloat16), upper)
    cnt = jnp.sum(member, axis=1, keepdims=True)
    pcc = jnp.floor((cnt + (CHUNK - 1)) * (1.0 / CHUNK))
    pcb = jnp.broadcast_to(pcc, (N_EXPERTS, LANES))
    row_e = lax.broadcasted_iota(jnp.int32, (N_EXPERTS, LANES), 0)
    inc = pcb
    sh = 1
    while sh < N_EXPERTS:
        inc = inc + jnp.where(row_e >= sh, pltpu.roll(inc, sh, axis=0), 0.0)
        sh *= 2
    run_start = (inc - pcb)[:, 0:1] * CHUNK
    pcc_ref[0] = pcb.astype(jnp.int32)

    pos = run_start + cum
    slots = [jnp.sum(jnp.where(hot, pos, 0.0), axis=0, keepdims=True).astype(jnp.int32) for hot in hots]
    slot_ref[...] = jnp.concatenate(slots, axis=0)

    iota_r = lax.broadcasted_iota(jnp.int32, (R_CAP, TS), 0)
    sel = (iota_r == slots[0]) | (iota_r == slots[1]) | (iota_r == slots[2]) | (iota_r == slots[3])
    perm = jnp.where(sel, 1.0, 0.0).astype(jnp.bfloat16)
    xs_ref[0] = _pack_halves(_dot(perm, hb))


def _route(x1, g_ffn, w_router, b_router):
    t, d = x1.shape
    nt = t // TS
    w_pad = jnp.pad(w_router.astype(jnp.float32), ((0, 0), (0, LANES - N_EXPERTS)))
    w_hi = w_pad.astype(jnp.bfloat16)
    w_lo = (w_pad - w_hi.astype(jnp.float32)).astype(jnp.bfloat16)
    br_t = b_router.astype(jnp.float32)[:, None]
    const = lambda shape: pl.BlockSpec(shape, lambda i: (0,) * len(shape))
    return pl.pallas_call(
        _route_kernel,
        grid=(nt,),
        in_specs=[pl.BlockSpec((TS, d), lambda i: (i, 0)), const((1, d)), const(w_hi.shape), const(w_lo.shape),
                  const(br_t.shape)],
        out_specs=[
            pl.BlockSpec((1, R_CAP, d // 2), lambda i: (i, 0, 0)),
            pl.BlockSpec((TOP_K, TS), lambda i: (0, i)),
            pl.BlockSpec((TOP_K, TS), lambda i: (0, i)),
            pl.BlockSpec((1, N_EXPERTS, LANES), lambda i: (i, 0, 0)),
        ],
        out_shape=[
            jax.ShapeDtypeStruct((nt, R_CAP, d // 2), jnp.uint32),
            jax.ShapeDtypeStruct((TOP_K, t), jnp.int32),
            jax.ShapeDtypeStruct((TOP_K, t), jnp.float32),
            jax.ShapeDtypeStruct((nt, N_EXPERTS, LANES), jnp.int32),
        ],
        compiler_params=pltpu.CompilerParams(
            dimension_semantics=("arbitrary",), vmem_limit_bytes=VMEM_LIMIT),
        name="route",
    )(x1, g_ffn, w_hi, w_lo, br_t)


def _experts_kernel(src_ref, bexp_ref, nval_ref, nvc_ref,
                    xs_hbm, w1_ref, b1g_ref, b1l_ref, w2_ref, b2_ref, dei_ref,
                    y_hbm,
                    xbuf, ybuf, zbuf, w1g, w1l, w2b, sem_in, sem_out, sem_z):
    b = pl.program_id(0)
    nb = pl.num_programs(0)
    n_tiles = nvc_ref.shape[0]
    slot = b % 2

    def zero_tail(tile, start):
        n = CPT - nvc_ref[tile]
        off = tile * CPT + nvc_ref[tile]
        for pc in _TAIL_PIECES:
            take = (n & pc) != 0

            @pl.when(take)
            def _(off=off, pc=pc):
                cp = pltpu.make_async_copy(zbuf.at[pl.ds(0, pc)], y_hbm.at[pl.ds(off, pc)], sem_z)
                if start:
                    cp.start()
                else:
                    cp.wait()
            off = off + jnp.where(take, pc, 0)

    @pl.when(b == 0)
    def _():
        zbuf[...] = jnp.zeros_like(zbuf)

    @pl.when(b < n_tiles)
    def _():
        zero_tail(b, True)

    def chunk_copy(blk, sl, c, inbound):
        if inbound:
            return pltpu.make_async_copy(xs_hbm.at[src_ref[blk * CPB + c]], xbuf.at[sl, c], sem_in.at[sl])
        return pltpu.make_async_copy(ybuf.at[sl, c], y_hbm.at[src_ref[blk * CPB + c]], sem_out.at[sl])

    def start_all(blk, sl, inbound, static):
        if static:
            for c in range(CPB):
                chunk_copy(blk, sl, c, inbound).start()
        else:
            def body(c, carry):
                chunk_copy(blk, sl, c, inbound).start()
                return carry
            lax.fori_loop(0, nval_ref[blk], body, 0)

    def wait_all(blk, sl, inbound):
        n = nval_ref[blk]

        @pl.when(n == CPB)
        def _():
            if inbound:
                pltpu.make_async_copy(xs_hbm.at[pl.ds(0, CPB)], xbuf.at[sl], sem_in.at[sl]).wait()
            else:
                pltpu.make_async_copy(ybuf.at[sl], y_hbm.at[pl.ds(0, CPB)], sem_out.at[sl]).wait()

        @pl.when(n != CPB)
        def _():
            def body(c, carry):
                chunk_copy(blk, sl, c, inbound).wait()
                return carry
            lax.fori_loop(0, n, body, 0)

    def compute():
        xb = _unpack_halves(xbuf[slot].reshape(TM, xbuf.shape[3]))
        hg = _dot(xb, w1g[...]) + b1g_ref[0]
        hl = _dot(xb, w1l[...]) + b1l_ref[0]
        hg = jnp.minimum(hg, SWIGLU_LIMIT)
        hl = jnp.clip(hl, -SWIGLU_LIMIT, SWIGLU_LIMIT)
        act = hg * jax.nn.sigmoid(SWIGLU_ALPHA * hg) * (hl + 1.0)
        y = _dot(act.astype(jnp.bfloat16), w2b[...]) + b2_ref[0]
        ybuf[slot] = _pack_halves(y).reshape(CPB, CHUNK, y.shape[1] // 2)

    @pl.when(b == 0)
    def _():
        xbuf[...] = jnp.zeros_like(xbuf)
        start_all(0, 0, True, False)

    prev_e = bexp_ref[jnp.maximum(b - 1, 0)]

    @pl.when((b == 0) | (bexp_ref[b] != prev_e))
    def _():
        dei = dei_ref[...]
        for j in range(w1_ref.shape[2] // (2 * LANES)):
            wj = w1_ref[0, :, j * 2 * LANES:(j + 1) * 2 * LANES].astype(jnp.bfloat16)
            r = _dot(wj, dei)
            w1g[:, j * LANES:(j + 1) * LANES] = r[:, 0:LANES].astype(jnp.bfloat16)
            w1l[:, j * LANES:(j + 1) * LANES] = r[:, LANES:].astype(jnp.bfloat16)
        w2b[...] = w2_ref[0].astype(jnp.bfloat16)

    @pl.when(b >= 2)
    def _():
        wait_all(b - 2, slot, False)

    wait_all(b, slot, True)

    prev_n = nval_ref[jnp.maximum(b - 1, 0)]
    next_n = nval_ref[jnp.minimum(b + 1, nb - 1)]
    fast = (b >= 1) & (prev_n == CPB) & (b + 1 < nb) & (next_n == CPB) & (nval_ref[b] > 0)

    @pl.when(fast)
    def _():
        start_all(b - 1, 1 - slot, False, True)
        start_all(b + 1, 1 - slot, True, True)
        compute()

    @pl.when(jnp.logical_not(fast))
    def _():
        @pl.when(b >= 1)
        def _():
            start_all(b - 1, 1 - slot, False, False)

        @pl.when(b + 1 < nb)
        def _():
            start_all(b + 1, 1 - slot, True, False)

        @pl.when(nval_ref[b] > 0)
        def _():
            compute()

    @pl.when(b < n_tiles)
    def _():
        zero_tail(b, False)

    @pl.when(b == nb - 1)
    def _():
        start_all(b, slot, False, False)

        @pl.when(b >= 1)
        def _():
            wait_all(b - 1, 1 - slot, False)
        wait_all(b, slot, False)


def _experts(chunk_src, block_expert, block_nvalid, nvc, xs_chunks, w1, b1g, b1l, w2, b2, dei):
    n_chunks, _, dh = xs_chunks.shape
    nb = block_expert.shape[0]
    assert nb >= nvc.shape[0]
    d, de2 = w1.shape[1], w1.shape[2]
    grid_spec = pltpu.PrefetchScalarGridSpec(
        num_scalar_prefetch=4,
        grid=(nb,),
        in_specs=[
            pl.BlockSpec(memory_space=pl.ANY),
            pl.BlockSpec((1, d, de2), lambda i, src, be, nv, nc: (be[i], 0, 0)),
            pl.BlockSpec((1, 1, de2 // 2), lambda i, src, be, nv, nc: (be[i], 0, 0)),
            pl.BlockSpec((1, 1, de2 // 2), lambda i, src, be, nv, nc: (be[i], 0, 0)),
            pl.BlockSpec((1, de2 // 2, d), lambda i, src, be, nv, nc: (be[i], 0, 0)),
            pl.BlockSpec((1, 1, d), lambda i, src, be, nv, nc: (be[i], 0, 0)),
            pl.BlockSpec(dei.shape, lambda i, src, be, nv, nc: (0, 0)),
        ],
        out_specs=pl.BlockSpec(memory_space=pl.ANY),
        scratch_shapes=[
            pltpu.VMEM((2, CPB, CHUNK, dh), jnp.uint32),
            pltpu.VMEM((2, CPB, CHUNK, dh), jnp.uint32),
            pltpu.VMEM((TAIL_MAX, CHUNK, dh), jnp.uint32),
            pltpu.VMEM((d, de2 // 2), jnp.bfloat16),
            pltpu.VMEM((d, de2 // 2), jnp.bfloat16),
            pltpu.VMEM((de2 // 2, d), jnp.bfloat16),
            pltpu.SemaphoreType.DMA((2,)),
            pltpu.SemaphoreType.DMA((2,)),
            pltpu.SemaphoreType.DMA(()),
        ],
    )
    return pl.pallas_call(
        _experts_kernel,
        grid_spec=grid_spec,
        out_shape=jax.ShapeDtypeStruct(xs_chunks.shape, jnp.uint32),
        compiler_params=pltpu.CompilerParams(
            dimension_semantics=("arbitrary",), vmem_limit_bytes=VMEM_LIMIT),
        name="experts",
    )(chunk_src, block_expert, block_nvalid, nvc, xs_chunks, w1, b1g, b1l, w2, b2, dei)


_PIECES = tuple(1 << i for i in reversed(range((CPT).bit_length())))


def _combine_kernel(nvc_ref,
                    y_hbm, x1_ref, slot_ref, gate_ref, gfin_ref, o_ref, ybuf, sem):
    i = pl.program_id(0)
    nt = pl.num_programs(0)
    sl = i % 2

    def fetch(tile, s_, start):
        n = nvc_ref[tile]
        off = jnp.int32(0)
        for pc in _PIECES:
            if pc * CHUNK > R_CAP:
                continue
            take = (n & pc) != 0

            @pl.when(take)
            def _(off=off, pc=pc):
                cp = pltpu.make_async_copy(
                    y_hbm.at[tile, pl.ds(pl.multiple_of(off * CHUNK, CHUNK), pc * CHUNK)],
                    ybuf.at[s_, pl.ds(pl.multiple_of(off * CHUNK, CHUNK), pc * CHUNK)], sem.at[s_])
                if start:
                    cp.start()
                else:
                    cp.wait()
            off = off + jnp.where(take, pc, 0)

    @pl.when(i == 0)
    def _():
        ybuf[...] = jnp.zeros_like(ybuf)
        fetch(0, 0, True)

    @pl.when(i + 1 < nt)
    def _():
        fetch(i + 1, 1 - sl, True)

    fetch(i, sl, False)
    iota_r = lax.broadcasted_iota(jnp.int32, (R_CAP, TS), 0)
    gt = jnp.zeros((R_CAP, TS), jnp.float32)
    for k in range(TOP_K):
        gt = jnp.where(iota_r == slot_ref[k:k + 1, :], gate_ref[k:k + 1, :], gt)
    moe = lax.dot_general(gt.astype(jnp.bfloat16), _unpack_halves(ybuf[sl]),
                          (((0,), (0,)), ((), ())), preferred_element_type=jnp.float32)
    o_ref[...] = _rms(x1_ref[...] + moe, gfin_ref[...])


def _combine(nvc, y_tiles, x1, slot_t, gate_t, g_final):
    t, d = x1.shape
    nt = t // TS
    grid_spec = pltpu.PrefetchScalarGridSpec(
        num_scalar_prefetch=1,
        grid=(nt,),
        in_specs=[
            pl.BlockSpec(memory_space=pl.ANY),
            pl.BlockSpec((TS, d), lambda i, nv: (i, 0)),
            pl.BlockSpec((TOP_K, TS), lambda i, nv: (0, i)),
            pl.BlockSpec((TOP_K, TS), lambda i, nv: (0, i)),
            pl.BlockSpec((1, d), lambda i, nv: (0, 0)),
        ],
        out_specs=pl.BlockSpec((TS, d), lambda i, nv: (i, 0)),
        scratch_shapes=[pltpu.VMEM((2, R_CAP, d // 2), jnp.uint32), pltpu.SemaphoreType.DMA((2,))],
    )
    return pl.pallas_call(
        _combine_kernel,
        grid_spec=grid_spec,
        out_shape=jax.ShapeDtypeStruct((t, d), jnp.float32),
        compiler_params=pltpu.CompilerParams(
            dimension_semantics=("arbitrary",), vmem_limit_bytes=VMEM_LIMIT),
        name="combine",
    )(nvc, y_tiles, x1, slot_t, gate_t, g_final)


def _chunk_plan(pcc, nb):
    nt = pcc.shape[0]
    i32 = jnp.int32
    pcc_t = pcc.T
    run_start_t = (jnp.cumsum(pcc, axis=1) - pcc).T
    cum_incl = jnp.cumsum(pcc_t, axis=1)
    eblocks = (cum_incl[:, -1] + CPB - 1) // CPB
    bstart = jnp.cumsum(eblocks) - eblocks
    blk = jnp.arange(nb, dtype=i32)
    bexp = jnp.sum(bstart[None, :] <= blk[:, None], axis=1).astype(i32) - 1
    oh = bexp[:, None] == jnp.arange(N_EXPERTS, dtype=i32)[None, :]
    pick = lambda tab: jnp.sum(jnp.where(oh[:, :, None], tab[None], 0), axis=1)
    ci, pc_row, rs_row = pick(cum_incl), pick(pcc_t), pick(run_start_t)
    b0 = jnp.sum(jnp.where(oh, bstart[None, :], 0), axis=1)
    p = (blk - b0)[:, None] * CPB + jnp.arange(CPB, dtype=i32)[None, :]
    before = ci[:, None, :] <= p[:, :, None]
    tile = jnp.sum(before, axis=2).astype(i32)
    cum_excl = jnp.sum(jnp.where(before, pc_row[:, None, :], 0), axis=2)
    at_tile = jnp.arange(nt, dtype=i32)[None, None, :] == tile[:, :, None]
    rs = jnp.sum(jnp.where(at_tile, rs_row[:, None, :], 0), axis=2)
    valid = tile < nt
    src = jnp.where(valid, tile * CPT + rs + (p - cum_excl), 0).astype(i32).reshape(-1)
    nvalid = jnp.sum(valid, axis=1).astype(i32)
    return src, bexp, nvalid


def kernel(x, g_mix, w_in, b_in, sinks, w_dw, b_dw, ln_g, ln_b, g_attn_out, g_conv_out, w_out, b_out,
           g_ffn, w_router, b_router, w1, b1, w2, b2, g_final):
    bsz, seq, d = x.shape
    t = bsz * seq
    nt = t // TS
    depth = g_mix.shape[0]
    assert depth == 1, "combine fuses the final RMSNorm, so exactly one layer is supported"
    f32 = jnp.float32
    max_chunks = nt * ((TS * TOP_K + N_EXPERTS * (CHUNK - 1)) // CHUNK) + N_EXPERTS * (CPB - 1)
    nb = -(-max_chunks // CPB)
    ci = jnp.arange(2 * LANES)
    dei = (ci[:, None] == jnp.where(ci < LANES, 2 * ci, 2 * (ci - LANES) + 1)[None, :]).astype(jnp.bfloat16)

    for l in range(depth):
        sink_cols = jnp.repeat(sinks[l].astype(f32).reshape(N_KV_HEADS, Q_PER_KV), WINDOW, axis=1)[..., None]
        x1 = _mix(x, g_mix[l][None], w_in[l].astype(jnp.bfloat16), b_in[l][None], sink_cols,
                  w_dw[l], b_dw[l][None], ln_g[l][None], ln_b[l][None],
                  g_attn_out[l][None], g_conv_out[l][None], w_out[l].astype(jnp.bfloat16), b_out[l][None])
        x1 = x1.reshape(t, d)
        xs, slot_t, gate_t, pcc = _route(x1, g_ffn[l][None], w_router[l], b_router[l])
        pcc = pcc[:, :, 0]
        src, bexp, nvalid = _chunk_plan(pcc, nb)
        b1l = b1[l].reshape(N_EXPERTS, 1, -1, 2)
        nvc = jnp.sum(pcc, axis=1).astype(jnp.int32)
        y = _experts(src, bexp, nvalid, nvc, xs.reshape(nt * CPT, CHUNK, d // 2), w1[l], b1l[..., 0], b1l[..., 1],
                     w2[l], b2[l][:, None, :], dei)
        x = _combine(nvc, y.reshape(nt, R_CAP, d // 2), x1, slot_t, gate_t, g_final[None]).reshape(bsz, seq, d)
    return x
```

```python
import functools

import jax
import jax.numpy as jnp
from jax import lax
from jax.experimental import pallas as pl
from jax.experimental.pallas import tpu as pltpu

HEAD_DIM = 64
N_Q_HEADS = 8
N_KV_HEADS = 2
Q_PER_KV = N_Q_HEADS // N_KV_HEADS
ATTN_WIDTH = N_Q_HEADS * HEAD_DIM
KV_WIDTH = N_KV_HEADS * HEAD_DIM
WINDOW = 128
CONV_KERNEL = 31
N_EXPERTS = 32
TOP_K = 4
SWIGLU_LIMIT = 7.0
SWIGLU_ALPHA = 1.702
RMS_EPS = 1e-5
LN_EPS = 1e-5

LANES = 128
SUBLANES = 8

TS = 256
CHUNK = SUBLANES
R_CAP = -(-(TS * TOP_K + N_EXPERTS * (CHUNK - 1)) // LANES) * LANES
CPT = R_CAP // CHUNK
TAIL_MAX = CPT - TS * TOP_K // CHUNK
_TAIL_PIECES = tuple(1 << i for i in reversed(range(TAIL_MAX.bit_length())))
MIX_SEQS = 2
MIX_COLS = 256
MIX_FILL_EVERY = 2
TM = 512
CPB = TM // CHUNK
CONV_HALO = 32
CONV_ROWS = 64
NEG_BIG = -1e30
VMEM_LIMIT = 56 * 1024 * 1024


def _rms(x, g):
    return x * lax.rsqrt(jnp.mean(x * x, axis=-1, keepdims=True) + RMS_EPS) * g


def _dot(a, b):
    return jnp.dot(a, b, preferred_element_type=jnp.float32)


_HI16 = 0xFFFF0000


def _pack_halves(v):
    half = v.shape[1] // 2
    as_bits = lambda t: lax.bitcast_convert_type(t.astype(jnp.bfloat16).astype(jnp.float32), jnp.uint32)
    return (as_bits(v[:, half:]) & jnp.uint32(_HI16)) | (as_bits(v[:, :half]) >> 16)


def _unpack_halves(w):
    lo = lax.bitcast_convert_type(w << 16, jnp.float32).astype(jnp.bfloat16)
    hi = lax.bitcast_convert_type(w & jnp.uint32(_HI16), jnp.float32).astype(jnp.bfloat16)
    return jnp.concatenate([lo, hi], axis=1)


def _dot_nt(a, b, precision=None):
    return lax.dot_general(a, b, (((1,), (1,)), ((), ())), precision=precision,
                           preferred_element_type=jnp.float32)


def _mix_kernel(x_ref, gmix_ref, win_ref, bin_ref, sink_ref, wdw_ref, bdw_ref, lng_ref, lnb_ref,
                ga_ref, gc_ref, wout_ref, bout_ref, o_ref, *scratch):
    params = (gmix_ref, win_ref, bin_ref, sink_ref, wdw_ref, bdw_ref, lng_ref, lnb_ref,
              ga_ref, gc_ref, wout_ref, bout_ref)
    per_seq = len(scratch) // MIX_SEQS
    seqs = [scratch[i * per_seq:(i + 1) * per_seq] for i in range(MIX_SEQS)]
    first = pl.program_id(1) == 0

    @pl.when(first)
    def _():
        for kv_prev, u_buf, *_ in seqs:
            kv_prev[...] = jnp.zeros_like(kv_prev)
            u_buf[0:CONV_HALO, :] = jnp.zeros((CONV_HALO, u_buf.shape[1]), jnp.float32)

    _emit_pipelined([_Stream(_mix_tile(x_ref.at[i], o_ref.at[i], first, params, *seqs[i]))
                     for i in range(MIX_SEQS)])


class _Stream:
    def __init__(self, gen):
        self.gen = gen
        self.tag = next(gen, None)

    def step(self):
        self.tag = next(self.gen, None)

    def run_while(self, tag):
        while self.tag == tag:
            self.step()


def _emit_pipelined(streams):
    streams[0].run_while("head")
    for i, cur in enumerate(streams):
        prv = streams[i - 1] if i >= 1 else None
        nxt = streams[i + 1] if i + 1 < len(streams) else None
        n = 0
        while cur.tag == "body":
            cur.step()
            n += 1
            if n % MIX_FILL_EVERY == 0:
                if prv is not None and prv.tag == "tail":
                    prv.step()
                elif nxt is not None and nxt.tag == "head":
                    nxt.step()
        if prv is not None:
            prv.run_while("tail")
        if nxt is not None:
            nxt.run_while("head")
    streams[-1].run_while("tail")


def _mix_tile(x_ref, o_ref, first, params, kv_prev, u_buf, u_sh, conv_buf, pbuf, mbuf):
    gmix_ref, win_ref, bin_ref, sink_ref, wdw_ref, bdw_ref, lng_ref, lnb_ref, ga_ref, gc_ref, wout_ref, bout_ref = params
    ub = None
    for c in range(win_ref.shape[1] // MIX_COLS):
        yield "head"
        if ub is None:
            ub = _rms(x_ref[0], gmix_ref[...]).astype(jnp.bfloat16)
        cs = slice(c * MIX_COLS, (c + 1) * MIX_COLS)
        pbuf[:, cs] = _dot(ub, win_ref[:, cs]) + bin_ref[:, cs]

    yield "body"
    o_q, o_k, o_v, o_a, o_g = 0, ATTN_WIDTH, ATTN_WIDTH + KV_WIDTH, ATTN_WIDTH + 2 * KV_WIDTH, \
        ATTN_WIDTH + 2 * KV_WIDTH + ATTN_WIDTH
    q = pbuf[:, o_q:o_k] * (HEAD_DIM ** -0.5)
    k = pbuf[:, o_k:o_v]
    v = pbuf[:, o_v:o_a]

    lane = lax.broadcasted_iota(jnp.int32, (WINDOW + TS, LANES), 1)
    low = lane < HEAD_DIM
    kfull = jnp.concatenate([kv_prev[:, 0:KV_WIDTH], k], axis=0)
    vfull = jnp.concatenate([kv_prev[:, KV_WIDTH:], v], axis=0)
    kroll = pltpu.roll(kfull, HEAD_DIM, axis=1)
    vroll = pltpu.roll(vfull, HEAD_DIM, axis=1)
    kk = [jnp.where(low, kfull, kroll).astype(jnp.bfloat16), jnp.where(low, kroll, kfull).astype(jnp.bfloat16)]
    vv = [jnp.where(low, vfull, vroll).astype(jnp.bfloat16), jnp.where(low, vroll, vfull).astype(jnp.bfloat16)]
    kv_prev[:, 0:KV_WIDTH] = k[TS - WINDOW:, :]
    kv_prev[:, KV_WIDTH:] = v[TS - WINDOW:, :]

    rows = Q_PER_KV * WINDOW
    qi = lax.broadcasted_iota(jnp.int32, (rows, 2 * WINDOW), 0) & (WINDOW - 1)
    kj = lax.broadcasted_iota(jnp.int32, (rows, 2 * WINDOW), 1)
    band = (kj > qi) & (kj <= qi + WINDOW)
    lane_q = lax.broadcasted_iota(jnp.int32, (WINDOW, LANES), 1)
    low_q = lane_q < HEAD_DIM
    ones_kv = jnp.ones((2 * WINDOW, LANES), jnp.bfloat16)

    def attn_unit(b, g):
        qb = q[b * WINDOW:(b + 1) * WINDOW, :]
        valid = band & ((kj >= WINDOW) | jnp.logical_not(first & (b == 0)))
        parts = []
        for i in range(Q_PER_KV):
            h = g * Q_PER_KV + i
            q128 = qb[:, (h // 2) * LANES:(h // 2 + 1) * LANES]
            keep = low_q if h % 2 == 0 else jnp.logical_not(low_q)
            parts.append(jnp.where(keep, q128, 0.0))
        qs = jnp.concatenate(parts, axis=0).astype(jnp.bfloat16)
        kb = kk[g][b * WINDOW:b * WINDOW + 2 * WINDOW, :]
        vb = vv[g][b * WINDOW:b * WINDOW + 2 * WINDOW, :]
        sc = _dot_nt(qs, kb)
        sc = jnp.where(valid, sc, NEG_BIG)
        sink = sink_ref[g]
        m = jnp.maximum(jnp.max(sc, axis=-1, keepdims=True), sink)
        p = jnp.exp(sc - m).astype(jnp.bfloat16)
        den = _dot(p, ones_kv) + jnp.exp(sink - m)
        pv = _dot(p, vb)
        o = pv / den
        return [o[i * WINDOW:(i + 1) * WINDOW, :] for i in range(Q_PER_KV)]

    cw = u_buf.shape[1]
    u_buf[CONV_HALO:CONV_HALO + TS, :] = pbuf[:, o_a:o_g] * jax.nn.sigmoid(pbuf[:, o_g:])
    n_sh = u_sh.shape[1]
    for s in range(1, SUBLANES):
        u_sh[s - 1] = u_buf[s:s + n_sh, :]
    shift = CONV_HALO - (CONV_KERNEL - 1)

    def conv_strip(c, r0):
        cs = slice(c * LANES, (c + 1) * LANES)
        acc = jnp.zeros((CONV_ROWS, LANES), jnp.float32)
        for j in range(CONV_KERNEL):
            base, s = (shift + j) // SUBLANES * SUBLANES, (shift + j) % SUBLANES
            rs = slice(r0 + base, r0 + base + CONV_ROWS)
            win = u_buf[rs, cs] if s == 0 else u_sh[s - 1, rs, cs]
            acc = acc + wdw_ref[j:j + 1, cs] * win
        conv_buf[r0:r0 + CONV_ROWS, cs] = acc

    units = [(b, g) for b in range(TS // WINDOW) for g in range(N_KV_HEADS)]
    strips = [(c, r0) for c in range(cw // LANES) for r0 in range(0, TS, CONV_ROWS)]
    per_unit = -(-len(strips) // len(units))
    head_out = {}
    for n, (b, g) in enumerate(units):
        yield "body"
        head_out[b, g] = attn_unit(b, g)
        for c, r0 in strips[n * per_unit:(n + 1) * per_unit]:
            yield "body"
            conv_strip(c, r0)

    yield "body"
    attn_blocks = []
    for b in range(TS // WINDOW):
        heads = [o for g in range(N_KV_HEADS) for o in head_out[b, g]]
        cols = [jnp.where(low_q, heads[2 * j], heads[2 * j + 1]) for j in range(N_Q_HEADS // 2)]
        attn_blocks.append(jnp.concatenate(cols, axis=1))
    attn = jnp.concatenate(attn_blocks, axis=0)

    u_buf[0:CONV_HALO, :] = u_buf[TS:TS + CONV_HALO, :]
    cv = conv_buf[...] + bdw_ref[...]
    mu = jnp.mean(cv, axis=-1, keepdims=True)
    xc = cv - mu
    cv = xc * lax.rsqrt(jnp.mean(xc * xc, axis=-1, keepdims=True) + LN_EPS) * lng_ref[...] + lnb_ref[...]
    cv = cv * jax.nn.sigmoid(cv)

    mbuf[...] = jnp.concatenate([_rms(attn, ga_ref[...]), _rms(cv, gc_ref[...])], axis=1).astype(jnp.bfloat16)

    for c in range(wout_ref.shape[1] // MIX_COLS):
        yield "tail"
        cs = slice(c * MIX_COLS, (c + 1) * MIX_COLS)
        o_ref[0, :, cs] = x_ref[0, :, cs] + _dot(mbuf[...], wout_ref[:, cs]) + bout_ref[:, cs]


def _mix(x, g_mix, w_in, b_in, sink_cols, w_dw, b_dw, ln_g, ln_b, g_a, g_c, w_out, b_out):
    bsz, seq, d = x.shape
    assert seq % TS == 0 and TS % WINDOW == 0 and CONV_HALO >= CONV_KERNEL - 1 and bsz % MIX_SEQS == 0
    cw = w_dw.shape[1]
    const = lambda shape: pl.BlockSpec(shape, lambda b, s: (0,) * len(shape))
    x = x.reshape(MIX_SEQS, bsz // MIX_SEQS, seq, d)
    seq_scratch = [
        pltpu.VMEM((WINDOW, 2 * KV_WIDTH), jnp.float32),
        pltpu.VMEM((CONV_HALO + TS, cw), jnp.float32),
        pltpu.VMEM((SUBLANES - 1, CONV_HALO + TS - SUBLANES, cw), jnp.float32),
        pltpu.VMEM((TS, cw), jnp.float32),
        pltpu.VMEM((TS, w_in.shape[1]), jnp.float32),
        pltpu.VMEM((TS, w_out.shape[0]), jnp.bfloat16),
    ]
    return pl.pallas_call(
        _mix_kernel,
        grid=(bsz // MIX_SEQS, seq // TS),
        in_specs=[
            pl.BlockSpec((MIX_SEQS, 1, TS, d), lambda b, s: (0, b, s, 0)),
            const((1, d)), const(w_in.shape), const((1, w_in.shape[1])),
            const(sink_cols.shape), const(w_dw.shape), const((1, cw)), const((1, cw)), const((1, cw)),
            const((1, ATTN_WIDTH)), const((1, cw)), const(w_out.shape), const((1, d)),
        ],
        out_specs=pl.BlockSpec((MIX_SEQS, 1, TS, d), lambda b, s: (0, b, s, 0)),
        out_shape=jax.ShapeDtypeStruct(x.shape, jnp.float32),
        scratch_shapes=seq_scratch * MIX_SEQS,
        compiler_params=pltpu.CompilerParams(
            dimension_semantics=("arbitrary", "arbitrary"), vmem_limit_bytes=VMEM_LIMIT),
        name="mix",
    )(x, g_mix, w_in, b_in, sink_cols, w_dw, b_dw, ln_g, ln_b, g_a, g_c, w_out, b_out).reshape(bsz, seq, d)


def _route_kernel(x1_ref, gffn_ref, whi_ref, wlo_ref, brt_ref, xs_ref, slot_ref, gate_ref, pcc_ref):
    h = _rms(x1_ref[...], gffn_ref[...])
    hb = h.astype(jnp.bfloat16)
    h_lo = (h - hb.astype(jnp.float32)).astype(jnp.bfloat16)
    lg_t = _dot(hb, whi_ref[...]) + (_dot(hb, wlo_ref[...]) + _dot(h_lo, whi_ref[...]))
    lg = lg_t.T[0:N_EXPERTS, :] + brt_ref[...]
    iota_e = lax.broadcasted_iota(jnp.int32, (N_EXPERTS, TS), 0)
    vals, hots = [], []
    member = jnp.zeros((N_EXPERTS, TS), jnp.float32)
    for _ in range(TOP_K):
        mx = jnp.max(lg, axis=0, keepdims=True)
        idx = jnp.min(jnp.where(lg == mx, iota_e, N_EXPERTS), axis=0, keepdims=True)
        hot = iota_e == idx
        lg = jnp.where(hot, -jnp.inf, lg)
        member = member + hot.astype(jnp.float32)
        vals.append(mx)
        hots.append(hot)
    ex = [jnp.exp(vk - vals[0]) for vk in vals]
    den = ex[0] + ex[1] + ex[2] + ex[3]
    gate_ref[...] = jnp.concatenate([e / den for e in ex], axis=0)

    ti = lax.broadcasted_iota(jnp.int32, (TS, TS), 0)
    tj = lax.broadcasted_iota(jnp.int32, (TS, TS), 1)
    upper = jnp.where(ti < tj, 1.0, 0.0).astype(jnp.bfloat16)
    cum = _dot(member.astype(jnp.bfloat16), upper)
    cnt = jnp.sum(member, axis=1, keepdims=True)
    pcc = jnp.floor((cnt + (CHUNK - 1)) * (1.0 / CHUNK))
    pcb = jnp.broadcast_to(pcc, (N_EXPERTS, LANES))
    row_e = lax.broadcasted_iota(jnp.int32, (N_EXPERTS, LANES), 0)
    inc = pcb
    sh = 1
    while sh < N_EXPERTS:
        inc = inc + jnp.where(row_e >= sh, pltpu.roll(inc, sh, axis=0), 0.0)
        sh *= 2
    run_start = (inc - pcb)[:, 0:1] * CHUNK
    pcc_ref[0] = pcb.astype(jnp.int32)

    pos = run_start + cum
    slots = [jnp.sum(jnp.where(hot, pos, 0.0), axis=0, keepdims=True).astype(jnp.int32) for hot in hots]
    slot_ref[...] = jnp.concatenate(slots, axis=0)

    iota_r = lax.broadcasted_iota(jnp.int32, (R_CAP, TS), 0)
    sel = (iota_r == slots[0]) | (iota_r == slots[1]) | (iota_r == slots[2]) | (iota_r == slots[3])
    perm = jnp.where(sel, 1.0, 0.0).astype(jnp.bfloat16)
    xs_ref[0] = _pack_halves(_dot(perm, hb))


def _route(x1, g_ffn, w_router, b_router):
    t, d = x1.shape
    nt = t // TS
    w_pad = jnp.pad(w_router.astype(jnp.float32), ((0, 0), (0, LANES - N_EXPERTS)))
    w_hi = w_pad.astype(jnp.bfloat16)
    w_lo = (w_pad - w_hi.astype(jnp.float32)).astype(jnp.bfloat16)
    br_t = b_router.astype(jnp.float32)[:, None]
    const = lambda shape: pl.BlockSpec(shape, lambda i: (0,) * len(shape))
    return pl.pallas_call(
        _route_kernel,
        grid=(nt,),
        in_specs=[pl.BlockSpec((TS, d), lambda i: (i, 0)), const((1, d)), const(w_hi.shape), const(w_lo.shape),
                  const(br_t.shape)],
        out_specs=[
            pl.BlockSpec((1, R_CAP, d // 2), lambda i: (i, 0, 0)),
            pl.BlockSpec((TOP_K, TS), lambda i: (0, i)),
            pl.BlockSpec((TOP_K, TS), lambda i: (0, i)),
            pl.BlockSpec((1, N_EXPERTS, LANES), lambda i: (i, 0, 0)),
        ],
        out_shape=[
            jax.ShapeDtypeStruct((nt, R_CAP, d // 2), jnp.uint32),
            jax.ShapeDtypeStruct((TOP_K, t), jnp.int32),
            jax.ShapeDtypeStruct((TOP_K, t), jnp.float32),
            jax.ShapeDtypeStruct((nt, N_EXPERTS, LANES), jnp.int32),
        ],
        compiler_params=pltpu.CompilerParams(
            dimension_semantics=("arbitrary",), vmem_limit_bytes=VMEM_LIMIT),
        name="route",
    )(x1, g_ffn, w_hi, w_lo, br_t)


def _experts_kernel(src_ref, bexp_ref, nval_ref, nvc_ref,
                    xs_hbm, w1_ref, b1g_ref, b1l_ref, w2_ref, b2_ref, dei_ref,
                    y_hbm,
                    xbuf, ybuf, zbuf, w1g, w1l, w2b, sem_in, sem_out, sem_z):
    b = pl.program_id(0)
    nb = pl.num_programs(0)
    n_tiles = nvc_ref.shape[0]
    slot = b % 2

    def zero_tail(tile, start):
        n = CPT - nvc_ref[tile]
        off = tile * CPT + nvc_ref[tile]
        for pc in _TAIL_PIECES:
            take = (n & pc) != 0

            @pl.when(take)
            def _(off=off, pc=pc):
                cp = pltpu.make_async_copy(zbuf.at[pl.ds(0, pc)], y_hbm.at[pl.ds(off, pc)], sem_z)
                if start:
                    cp.start()
                else:
                    cp.wait()
            off = off + jnp.where(take, pc, 0)

    @pl.when(b == 0)
    def _():
        zbuf[...] = jnp.zeros_like(zbuf)

    @pl.when(b < n_tiles)
    def _():
        zero_tail(b, True)

    def chunk_copy(blk, sl, c, inbound):
        if inbound:
            return pltpu.make_async_copy(xs_hbm.at[src_ref[blk * CPB + c]], xbuf.at[sl, c], sem_in.at[sl])
        return pltpu.make_async_copy(ybuf.at[sl, c], y_hbm.at[src_ref[blk * CPB + c]], sem_out.at[sl])

    def start_all(blk, sl, inbound, static):
        if static:
            for c in range(CPB):
                chunk_copy(blk, sl, c, inbound).start()
        else:
            def body(c, carry):
                chunk_copy(blk, sl, c, inbound).start()
                return carry
            lax.fori_loop(0, nval_ref[blk], body, 0)

    def wait_all(blk, sl, inbound):
        n = nval_ref[blk]

        @pl.when(n == CPB)
        def _():
            if inbound:
                pltpu.make_async_copy(xs_hbm.at[pl.ds(0, CPB)], xbuf.at[sl], sem_in.at[sl]).wait()
            else:
                pltpu.make_async_copy(ybuf.at[sl], y_hbm.at[pl.ds(0, CPB)], sem_out.at[sl]).wait()

        @pl.when(n != CPB)
        def _():
            def body(c, carry):
                chunk_copy(blk, sl, c, inbound).wait()
                return carry
            lax.fori_loop(0, n, body, 0)

    def compute():
        xb = _unpack_halves(xbuf[slot].reshape(TM, xbuf.shape[3]))
        hg = _dot(xb, w1g[...]) + b1g_ref[0]
        hl = _dot(xb, w1l[...]) + b1l_ref[0]
        hg = jnp.minimum(hg, SWIGLU_LIMIT)
        hl = jnp.clip(hl, -SWIGLU_LIMIT, SWIGLU_LIMIT)
        act = hg * jax.nn.sigmoid(SWIGLU_ALPHA * hg) * (hl + 1.0)
        y = _dot(act.astype(jnp.bfloat16), w2b[...]) + b2_ref[0]
        ybuf[slot] = _pack_halves(y).reshape(CPB, CHUNK, y.shape[1] // 2)

    @pl.when(b == 0)
    def _():
        xbuf[...] = jnp.zeros_like(xbuf)
        start_all(0, 0, True, False)

    prev_e = bexp_ref[jnp.maximum(b - 1, 0)]

    @pl.when((b == 0) | (bexp_ref[b] != prev_e))
    def _():
        dei = dei_ref[...]
        for j in range(w1_ref.shape[2] // (2 * LANES)):
            wj = w1_ref[0, :, j * 2 * LANES:(j + 1) * 2 * LANES].astype(jnp.bfloat16)
            r = _dot(wj, dei)
            w1g[:, j * LANES:(j + 1) * LANES] = r[:, 0:LANES].astype(jnp.bfloat16)
            w1l[:, j * LANES:(j + 1) * LANES] = r[:, LANES:].astype(jnp.bfloat16)
        w2b[...] = w2_ref[0].astype(jnp.bfloat16)

    @pl.when(b >= 2)
    def _():
        wait_all(b - 2, slot, False)

    wait_all(b, slot, True)

    prev_n = nval_ref[jnp.maximum(b - 1, 0)]
    next_n = nval_ref[jnp.minimum(b + 1, nb - 1)]
    fast = (b >= 1) & (prev_n == CPB) & (b + 1 < nb) & (next_n == CPB) & (nval_ref[b] > 0)

    @pl.when(fast)
    def _():
        start_all(b - 1, 1 - slot, False, True)
        start_all(b + 1, 1 - slot, True, True)
        compute()

    @pl.when(jnp.logical_not(fast))
    def _():
        @pl.when(b >= 1)
        def _():
            start_all(b - 1, 1 - slot, False, False)

        @pl.when(b + 1 < nb)
        def _():
            start_all(b + 1, 1 - slot, True, False)

        @pl.when(nval_ref[b] > 0)
        def _():
            compute()

    @pl.when(b < n_tiles)
    def _():
        zero_tail(b, False)

    @pl.when(b == nb - 1)
    def _():
        start_all(b, slot, False, False)

        @pl.when(b >= 1)
        def _():
            wait_all(b - 1, 1 - slot, False)
        wait_all(b, slot, False)


def _experts(chunk_src, block_expert, block_nvalid, nvc, xs_chunks, w1, b1g, b1l, w2, b2, dei):
    n_chunks, _, dh = xs_chunks.shape
    nb = block_expert.shape[0]
    assert nb >= nvc.shape[0]
    d, de2 = w1.shape[1], w1.shape[2]
    grid_spec = pltpu.PrefetchScalarGridSpec(
        num_scalar_prefetch=4,
        grid=(nb,),
        in_specs=[
            pl.BlockSpec(memory_space=pl.ANY),
            pl.BlockSpec((1, d, de2), lambda i, src, be, nv, nc: (be[i], 0, 0)),
            pl.BlockSpec((1, 1, de2 // 2), lambda i, src, be, nv, nc: (be[i], 0, 0)),
            pl.BlockSpec((1, 1, de2 // 2), lambda i, src, be, nv, nc: (be[i], 0, 0)),
            pl.BlockSpec((1, de2 // 2, d), lambda i, src, be, nv, nc: (be[i], 0, 0)),
            pl.BlockSpec((1, 1, d), lambda i, src, be, nv, nc: (be[i], 0, 0)),
            pl.BlockSpec(dei.shape, lambda i, src, be, nv, nc: (0, 0)),
        ],
        out_specs=pl.BlockSpec(memory_space=pl.ANY),
        scratch_shapes=[
            pltpu.VMEM((2, CPB, CHUNK, dh), jnp.uint32),
            pltpu.VMEM((2, CPB, CHUNK, dh), jnp.uint32),
            pltpu.VMEM((TAIL_MAX, CHUNK, dh), jnp.uint32),
            pltpu.VMEM((d, de2 // 2), jnp.bfloat16),
            pltpu.VMEM((d, de2 // 2), jnp.bfloat16),
            pltpu.VMEM((de2 // 2, d), jnp.bfloat16),
            pltpu.SemaphoreType.DMA((2,)),
            pltpu.SemaphoreType.DMA((2,)),
            pltpu.SemaphoreType.DMA(()),
        ],
    )
    return pl.pallas_call(
        _experts_kernel,
        grid_spec=grid_spec,
        out_shape=jax.ShapeDtypeStruct(xs_chunks.shape, jnp.uint32),
        compiler_params=pltpu.CompilerParams(
            dimension_semantics=("arbitrary",), vmem_limit_bytes=VMEM_LIMIT),
        name="experts",
    )(chunk_src, block_expert, block_nvalid, nvc, xs_chunks, w1, b1g, b1l, w2, b2, dei)


_PIECES = tuple(1 << i for i in reversed(range((CPT).bit_length())))


def _combine_kernel(nvc_ref,
                    y_hbm, x1_ref, slot_ref, gate_ref, gfin_ref, o_ref, ybuf, sem):
    i = pl.program_id(0)
    nt = pl.num_programs(0)
    sl = i % 2

    def fetch(tile, s_, start):
        n = nvc_ref[tile]
        off = jnp.int32(0)
        for pc in _PIECES:
            if pc * CHUNK > R_CAP:
                continue
            take = (n & pc) != 0

            @pl.when(take)
            def _(off=off, pc=pc):
                cp = pltpu.make_async_copy(
                    y_hbm.at[tile, pl.ds(pl.multiple_of(off * CHUNK, CHUNK), pc * CHUNK)],
                    ybuf.at[s_, pl.ds(pl.multiple_of(off * CHUNK, CHUNK), pc * CHUNK)], sem.at[s_])
                if start:
                    cp.start()
                else:
                    cp.wait()
            off = off + jnp.where(take, pc, 0)

    @pl.when(i == 0)
    def _():
        ybuf[...] = jnp.zeros_like(ybuf)
        fetch(0, 0, True)

    @pl.when(i + 1 < nt)
    def _():
        fetch(i + 1, 1 - sl, True)

    fetch(i, sl, False)
    iota_r = lax.broadcasted_iota(jnp.int32, (R_CAP, TS), 0)
    gt = jnp.zeros((R_CAP, TS), jnp.float32)
    for k in range(TOP_K):
        gt = jnp.where(iota_r == slot_ref[k:k + 1, :], gate_ref[k:k + 1, :], gt)
    moe = lax.dot_general(gt.astype(jnp.bfloat16), _unpack_halves(ybuf[sl]),
                          (((0,), (0,)), ((), ())), preferred_element_type=jnp.float32)
    o_ref[...] = _rms(x1_ref[...] + moe, gfin_ref[...])


def _combine(nvc, y_tiles, x1, slot_t, gate_t, g_final):
    t, d = x1.shape
    nt = t // TS
    grid_spec = pltpu.PrefetchScalarGridSpec(
        num_scalar_prefetch=1,
        grid=(nt,),
        in_specs=[
            pl.BlockSpec(memory_space=pl.ANY),
            pl.BlockSpec((TS, d), lambda i, nv: (i, 0)),
            pl.BlockSpec((TOP_K, TS), lambda i, nv: (0, i)),
            pl.BlockSpec((TOP_K, TS), lambda i, nv: (0, i)),
            pl.BlockSpec((1, d), lambda i, nv: (0, 0)),
        ],
        out_specs=pl.BlockSpec((TS, d), lambda i, nv: (i, 0)),
        scratch_shapes=[pltpu.VMEM((2, R_CAP, d // 2), jnp.uint32), pltpu.SemaphoreType.DMA((2,))],
    )
    return pl.pallas_call(
        _combine_kernel,
        grid_spec=grid_spec,
        out_shape=jax.ShapeDtypeStruct((t, d), jnp.float32),
        compiler_params=pltpu.CompilerParams(
            dimension_semantics=("arbitrary",), vmem_limit_bytes=VMEM_LIMIT),
        name="combine",
    )(nvc, y_tiles, x1, slot_t, gate_t, g_final)


def _chunk_plan(pcc, nb):
    nt = pcc.shape[0]
    i32 = jnp.int32
    pcc_t = pcc.T
    run_start_t = (jnp.cumsum(pcc, axis=1) - pcc).T
    cum_incl = jnp.cumsum(pcc_t, axis=1)
    eblocks = (cum_incl[:, -1] + CPB - 1) // CPB
    bstart = jnp.cumsum(eblocks) - eblocks
    blk = jnp.arange(nb, dtype=i32)
    bexp = jnp.sum(bstart[None, :] <= blk[:, None], axis=1).astype(i32) - 1
    oh = bexp[:, None] == jnp.arange(N_EXPERTS, dtype=i32)[None, :]
    pick = lambda tab: jnp.sum(jnp.where(oh[:, :, None], tab[None], 0), axis=1)
    ci, pc_row, rs_row = pick(cum_incl), pick(pcc_t), pick(run_start_t)
    b0 = jnp.sum(jnp.where(oh, bstart[None, :], 0), axis=1)
    p = (blk - b0)[:, None] * CPB + jnp.arange(CPB, dtype=i32)[None, :]
    before = ci[:, None, :] <= p[:, :, None]
    tile = jnp.sum(before, axis=2).astype(i32)
    cum_excl = jnp.sum(jnp.where(before, pc_row[:, None, :], 0), axis=2)
    at_tile = jnp.arange(nt, dtype=i32)[None, None, :] == tile[:, :, None]
    rs = jnp.sum(jnp.where(at_tile, rs_row[:, None, :], 0), axis=2)
    valid = tile < nt
    src = jnp.where(valid, tile * CPT + rs + (p - cum_excl), 0).astype(i32).reshape(-1)
    nvalid = jnp.sum(valid, axis=1).astype(i32)
    return src, bexp, nvalid


def kernel(x, g_mix, w_in, b_in, sinks, w_dw, b_dw, ln_g, ln_b, g_attn_out, g_conv_out, w_out, b_out,
           g_ffn, w_router, b_router, w1, b1, w2, b2, g_final):
    bsz, seq, d = x.shape
    t = bsz * seq
    nt = t // TS
    depth = g_mix.shape[0]
    assert depth == 1, "combine fuses the final RMSNorm, so exactly one layer is supported"
    f32 = jnp.float32
    max_chunks = nt * ((TS * TOP_K + N_EXPERTS * (CHUNK - 1)) // CHUNK) + N_EXPERTS * (CPB - 1)
    nb = -(-max_chunks // CPB)
    ci = jnp.arange(2 * LANES)
    dei = (ci[:, None] == jnp.where(ci < LANES, 2 * ci, 2 * (ci - LANES) + 1)[None, :]).astype(jnp.bfloat16)

    for l in range(depth):
        sink_cols = jnp.repeat(sinks[l].astype(f32).reshape(N_KV_HEADS, Q_PER_KV), WINDOW, axis=1)[..., None]
        x1 = _mix(x, g_mix[l][None], w_in[l].astype(jnp.bfloat16), b_in[l][None], sink_cols,
                  w_dw[l], b_dw[l][None], ln_g[l][None], ln_b[l][None],
                  g_attn_out[l][None], g_conv_out[l][None], w_out[l].astype(jnp.bfloat16), b_out[l][None])
        x1 = x1.reshape(t, d)
        xs, slot_t, gate_t, pcc = _route(x1, g_ffn[l][None], w_router[l], b_router[l])
        pcc = pcc[:, :, 0]
        src, bexp, nvalid = _chunk_plan(pcc, nb)
        b1l = b1[l].reshape(N_EXPERTS, 1, -1, 2)
        nvc = jnp.sum(pcc, axis=1).astype(jnp.int32)
        y = _experts(src, bexp, nvalid, nvc, xs.reshape(nt * CPT, CHUNK, d // 2), w1[l], b1l[..., 0], b1l[..., 1],
                     w2[l], b2[l][:, None, :], dei)
        x = _combine(nvc, y.reshape(nt, R_CAP, d // 2), x1, slot_t, gate_t, g_final[None]).reshape(bsz, seq, d)
    return x
```

```python
import functools

import jax
import jax.numpy as jnp
from jax import lax
from jax.experimental import pallas as pl
from jax.experimental.pallas import tpu as pltpu

HEAD_DIM = 64
N_Q_HEADS = 8
N_KV_HEADS = 2
Q_PER_KV = N_Q_HEADS // N_KV_HEADS
ATTN_WIDTH = N_Q_HEADS * HEAD_DIM
KV_WIDTH = N_KV_HEADS * HEAD_DIM
WINDOW = 128
CONV_KERNEL = 31
N_EXPERTS = 32
TOP_K = 4
SWIGLU_LIMIT = 7.0
SWIGLU_ALPHA = 1.702
RMS_EPS = 1e-5
LN_EPS = 1e-5

LANES = 128
SUBLANES = 8

TS = 256
CHUNK = SUBLANES
R_CAP = -(-(TS * TOP_K + N_EXPERTS * (CHUNK - 1)) // LANES) * LANES
CPT = R_CAP // CHUNK
TAIL_MAX = CPT - TS * TOP_K // CHUNK
_TAIL_PIECES = tuple(1 << i for i in reversed(range(TAIL_MAX.bit_length())))
MIX_SEQS = 2
MIX_COLS = 256
MIX_FILL_EVERY = 2
TM = 512
CPB = TM // CHUNK
CONV_HALO = 32
CONV_ROWS = 64
NEG_BIG = -1e30
VMEM_LIMIT = 56 * 1024 * 1024


def _rms(x, g):
    return x * lax.rsqrt(jnp.mean(x * x, axis=-1, keepdims=True) + RMS_EPS) * g


def _dot(a, b):
    return jnp.dot(a, b, preferred_element_type=jnp.float32)


_HI16 = 0xFFFF0000


def _pack_halves(v, already_bf16=False):
    half = v.shape[1] // 2
    rnd = (lambda t: t) if already_bf16 else (lambda t: t.astype(jnp.bfloat16).astype(jnp.float32))
    as_bits = lambda t: lax.bitcast_convert_type(rnd(t), jnp.uint32)
    return (as_bits(v[:, half:]) & jnp.uint32(_HI16)) | (as_bits(v[:, :half]) >> 16)


def _unpack_halves(w):
    lo = lax.bitcast_convert_type(w << 16, jnp.float32).astype(jnp.bfloat16)
    hi = lax.bitcast_convert_type(w & jnp.uint32(_HI16), jnp.float32).astype(jnp.bfloat16)
    return jnp.concatenate([lo, hi], axis=1)


def _dot_nt(a, b, precision=None):
    return lax.dot_general(a, b, (((1,), (1,)), ((), ())), precision=precision,
                           preferred_element_type=jnp.float32)


def _mix_kernel(x_ref, gmix_ref, win_ref, bin_ref, sink_ref, wdw_ref, bdw_ref, lng_ref, lnb_ref,
                ga_ref, gc_ref, wout_ref, bout_ref, o_ref, *scratch):
    params = (gmix_ref, win_ref, bin_ref, sink_ref, wdw_ref, bdw_ref, lng_ref, lnb_ref,
              ga_ref, gc_ref, wout_ref, bout_ref)
    per_seq = len(scratch) // MIX_SEQS
    seqs = [scratch[i * per_seq:(i + 1) * per_seq] for i in range(MIX_SEQS)]
    first = pl.program_id(1) == 0

    @pl.when(first)
    def _():
        for kv_prev, u_buf, *_ in seqs:
            kv_prev[...] = jnp.zeros_like(kv_prev)
            u_buf[0:CONV_HALO, :] = jnp.zeros((CONV_HALO, u_buf.shape[1]), jnp.float32)

    _emit_pipelined([_Stream(_mix_tile(x_ref.at[i], o_ref.at[i], first, params, *seqs[i]))
                     for i in range(MIX_SEQS)])


class _Stream:
    def __init__(self, gen):
        self.gen = gen
        self.tag = next(gen, None)

    def step(self):
        self.tag = next(self.gen, None)

    def run_while(self, tag):
        while self.tag == tag:
            self.step()


def _emit_pipelined(streams):
    streams[0].run_while("head")
    for i, cur in enumerate(streams):
        prv = streams[i - 1] if i >= 1 else None
        nxt = streams[i + 1] if i + 1 < len(streams) else None
        n = 0
        while cur.tag == "body":
            cur.step()
            n += 1
            if n % MIX_FILL_EVERY == 0:
                if prv is not None and prv.tag == "tail":
                    prv.step()
                elif nxt is not None and nxt.tag == "head":
                    nxt.step()
        if prv is not None:
            prv.run_while("tail")
        if nxt is not None:
            nxt.run_while("head")
    streams[-1].run_while("tail")


def _mix_tile(x_ref, o_ref, first, params, kv_prev, u_buf, u_sh, conv_buf, pbuf, mbuf):
    gmix_ref, win_ref, bin_ref, sink_ref, wdw_ref, bdw_ref, lng_ref, lnb_ref, ga_ref, gc_ref, wout_ref, bout_ref = params
    ub = None
    for c in range(win_ref.shape[1] // MIX_COLS):
        yield "head"
        if ub is None:
            ub = _rms(x_ref[0], gmix_ref[...]).astype(jnp.bfloat16)
        cs = slice(c * MIX_COLS, (c + 1) * MIX_COLS)
        pbuf[:, cs] = _dot(ub, win_ref[:, cs]) + bin_ref[:, cs]

    yield "body"
    o_q, o_k, o_v, o_a, o_g = 0, ATTN_WIDTH, ATTN_WIDTH + KV_WIDTH, ATTN_WIDTH + 2 * KV_WIDTH, \
        ATTN_WIDTH + 2 * KV_WIDTH + ATTN_WIDTH
    q = pbuf[:, o_q:o_k] * (HEAD_DIM ** -0.5)
    k = pbuf[:, o_k:o_v]
    v = pbuf[:, o_v:o_a]

    lane = lax.broadcasted_iota(jnp.int32, (WINDOW + TS, LANES), 1)
    low = lane < HEAD_DIM
    kfull = jnp.concatenate([kv_prev[:, 0:KV_WIDTH], k], axis=0)
    vfull = jnp.concatenate([kv_prev[:, KV_WIDTH:], v], axis=0)
    kroll = pltpu.roll(kfull, HEAD_DIM, axis=1)
    vroll = pltpu.roll(vfull, HEAD_DIM, axis=1)
    kk = [jnp.where(low, kfull, kroll).astype(jnp.bfloat16), jnp.where(low, kroll, kfull).astype(jnp.bfloat16)]
    vv = [jnp.where(low, vfull, vroll).astype(jnp.bfloat16), jnp.where(low, vroll, vfull).astype(jnp.bfloat16)]
    kv_prev[:, 0:KV_WIDTH] = k[TS - WINDOW:, :]
    kv_prev[:, KV_WIDTH:] = v[TS - WINDOW:, :]

    rows = Q_PER_KV * WINDOW
    qi = lax.broadcasted_iota(jnp.int32, (rows, 2 * WINDOW), 0) & (WINDOW - 1)
    kj = lax.broadcasted_iota(jnp.int32, (rows, 2 * WINDOW), 1)
    band = (kj > qi) & (kj <= qi + WINDOW)
    lane_q = lax.broadcasted_iota(jnp.int32, (WINDOW, LANES), 1)
    low_q = lane_q < HEAD_DIM
    ones_kv = jnp.ones((2 * WINDOW, LANES), jnp.bfloat16)

    def attn_unit(b, g):
        qb = q[b * WINDOW:(b + 1) * WINDOW, :]
        valid = band & ((kj >= WINDOW) | jnp.logical_not(first & (b == 0)))
        parts = []
        for i in range(Q_PER_KV):
            h = g * Q_PER_KV + i
            q128 = qb[:, (h // 2) * LANES:(h // 2 + 1) * LANES]
            keep = low_q if h % 2 == 0 else jnp.logical_not(low_q)
            parts.append(jnp.where(keep, q128, 0.0))
        qs = jnp.concatenate(parts, axis=0).astype(jnp.bfloat16)
        kb = kk[g][b * WINDOW:b * WINDOW + 2 * WINDOW, :]
        vb = vv[g][b * WINDOW:b * WINDOW + 2 * WINDOW, :]
        sc = _dot_nt(qs, kb)
        sc = jnp.where(valid, sc, NEG_BIG)
        sink = sink_ref[g]
        m = jnp.maximum(jnp.max(sc, axis=-1, keepdims=True), sink)
        p = jnp.exp(sc - m).astype(jnp.bfloat16)
        den = _dot(p, ones_kv) + jnp.exp(sink - m)
        pv = _dot(p, vb)
        o = pv / den
        return [o[i * WINDOW:(i + 1) * WINDOW, :] for i in range(Q_PER_KV)]

    cw = u_buf.shape[1]
    u_buf[CONV_HALO:CONV_HALO + TS, :] = pbuf[:, o_a:o_g] * jax.nn.sigmoid(pbuf[:, o_g:])
    n_sh = u_sh.shape[1]
    for s in range(1, SUBLANES):
        u_sh[s - 1] = u_buf[s:s + n_sh, :]
    shift = CONV_HALO - (CONV_KERNEL - 1)

    def conv_strip(c, r0):
        cs = slice(c * LANES, (c + 1) * LANES)
        acc = jnp.zeros((CONV_ROWS, LANES), jnp.float32)
        for j in range(CONV_KERNEL):
            base, s = (shift + j) // SUBLANES * SUBLANES, (shift + j) % SUBLANES
            rs = slice(r0 + base, r0 + base + CONV_ROWS)
            win = u_buf[rs, cs] if s == 0 else u_sh[s - 1, rs, cs]
            acc = acc + wdw_ref[j:j + 1, cs] * win
        conv_buf[r0:r0 + CONV_ROWS, cs] = acc

    units = [(b, g) for b in range(TS // WINDOW) for g in range(N_KV_HEADS)]
    strips = [(c, r0) for c in range(cw // LANES) for r0 in range(0, TS, CONV_ROWS)]
    per_unit = -(-len(strips) // len(units))
    head_out = {}
    for n, (b, g) in enumerate(units):
        yield "body"
        head_out[b, g] = attn_unit(b, g)
        for c, r0 in strips[n * per_unit:(n + 1) * per_unit]:
            yield "body"
            conv_strip(c, r0)

    yield "body"
    attn_blocks = []
    for b in range(TS // WINDOW):
        heads = [o for g in range(N_KV_HEADS) for o in head_out[b, g]]
        cols = [jnp.where(low_q, heads[2 * j], heads[2 * j + 1]) for j in range(N_Q_HEADS // 2)]
        attn_blocks.append(jnp.concatenate(cols, axis=1))
    attn = jnp.concatenate(attn_blocks, axis=0)

    u_buf[0:CONV_HALO, :] = u_buf[TS:TS + CONV_HALO, :]
    cv = conv_buf[...] + bdw_ref[...]
    mu = jnp.mean(cv, axis=-1, keepdims=True)
    xc = cv - mu
    cv = xc * lax.rsqrt(jnp.mean(xc * xc, axis=-1, keepdims=True) + LN_EPS) * lng_ref[...] + lnb_ref[...]
    cv = cv * jax.nn.sigmoid(cv)

    mbuf[...] = jnp.concatenate([_rms(attn, ga_ref[...]), _rms(cv, gc_ref[...])], axis=1).astype(jnp.bfloat16)

    for c in range(wout_ref.shape[1] // MIX_COLS):
        yield "tail"
        cs = slice(c * MIX_COLS, (c + 1) * MIX_COLS)
        o_ref[0, :, cs] = x_ref[0, :, cs] + _dot(mbuf[...], wout_ref[:, cs]) + bout_ref[:, cs]


def _mix(x, g_mix, w_in, b_in, sink_cols, w_dw, b_dw, ln_g, ln_b, g_a, g_c, w_out, b_out):
    bsz, seq, d = x.shape
    assert seq % TS == 0 and TS % WINDOW == 0 and CONV_HALO >= CONV_KERNEL - 1 and bsz % MIX_SEQS == 0
    cw = w_dw.shape[1]
    const = lambda shape: pl.BlockSpec(shape, lambda b, s: (0,) * len(shape))
    x = x.reshape(MIX_SEQS, bsz // MIX_SEQS, seq, d)
    seq_scratch = [
        pltpu.VMEM((WINDOW, 2 * KV_WIDTH), jnp.float32),
        pltpu.VMEM((CONV_HALO + TS, cw), jnp.float32),
        pltpu.VMEM((SUBLANES - 1, CONV_HALO + TS - SUBLANES, cw), jnp.float32),
        pltpu.VMEM((TS, cw), jnp.float32),
        pltpu.VMEM((TS, w_in.shape[1]), jnp.float32),
        pltpu.VMEM((TS, w_out.shape[0]), jnp.bfloat16),
    ]
    return pl.pallas_call(
        _mix_kernel,
        grid=(bsz // MIX_SEQS, seq // TS),
        in_specs=[
            pl.BlockSpec((MIX_SEQS, 1, TS, d), lambda b, s: (0, b, s, 0)),
            const((1, d)), const(w_in.shape), const((1, w_in.shape[1])),
            const(sink_cols.shape), const(w_dw.shape), const((1, cw)), const((1, cw)), const((1, cw)),
            const((1, ATTN_WIDTH)), const((1, cw)), const(w_out.shape), const((1, d)),
        ],
        out_specs=pl.BlockSpec((MIX_SEQS, 1, TS, d), lambda b, s: (0, b, s, 0)),
        out_shape=jax.ShapeDtypeStruct(x.shape, jnp.float32),
        scratch_shapes=seq_scratch * MIX_SEQS,
        compiler_params=pltpu.CompilerParams(
            dimension_semantics=("arbitrary", "arbitrary"), vmem_limit_bytes=VMEM_LIMIT),
        name="mix",
    )(x, g_mix, w_in, b_in, sink_cols, w_dw, b_dw, ln_g, ln_b, g_a, g_c, w_out, b_out).reshape(bsz, seq, d)


def _route_kernel(x1_ref, gffn_ref, whi_ref, wlo_ref, brt_ref, xs_ref, slot_ref, gate_ref, pcc_ref):
    h = _rms(x1_ref[...], gffn_ref[...])
    hb = h.astype(jnp.bfloat16)
    h_lo = (h - hb.astype(jnp.float32)).astype(jnp.bfloat16)
    lg_t = _dot(hb, whi_ref[...]) + (_dot(hb, wlo_ref[...]) + _dot(h_lo, whi_ref[...]))
    lg = lg_t.T[0:N_EXPERTS, :] + brt_ref[...]
    iota_e = lax.broadcasted_iota(jnp.int32, (N_EXPERTS, TS), 0)
    vals, hots = [], []
    member = jnp.zeros((N_EXPERTS, TS), jnp.float32)
    for _ in range(TOP_K):
        mx = jnp.max(lg, axis=0, keepdims=True)
        idx = jnp.min(jnp.where(lg == mx, iota_e, N_EXPERTS), axis=0, keepdims=True)
        hot = iota_e == idx
        lg = jnp.where(hot, -jnp.inf, lg)
        member = member + hot.astype(jnp.float32)
        vals.append(mx)
        hots.append(hot)
    ex = [jnp.exp(vk - vals[0]) for vk in vals]
    den = ex[0] + ex[1] + ex[2] + ex[3]
    gate_ref[...] = jnp.concatenate([e / den for e in ex], axis=0)

    ti = lax.broadcasted_iota(jnp.int32, (TS, TS), 0)
    tj = lax.broadcasted_iota(jnp.int32, (TS, TS), 1)
    upper = jnp.where(ti < tj, 1.0, 0.0).astype(jnp.bfloat16)
    cum = _dot(member.astype(jnp.bfloat16), upper)
    cnt = jnp.sum(member, axis=1, keepdims=True)
    pcc = jnp.floor((cnt + (CHUNK - 1)) * (1.0 / CHUNK))
    pcb = jnp.broadcast_to(pcc, (N_EXPERTS, LANES))
    row_e = lax.broadcasted_iota(jnp.int32, (N_EXPERTS, LANES), 0)
    inc = pcb
    sh = 1
    while sh < N_EXPERTS:
        inc = inc + jnp.where(row_e >= sh, pltpu.roll(inc, sh, axis=0), 0.0)
        sh *= 2
    run_start = (inc - pcb)[:, 0:1] * CHUNK
    pcc_ref[0] = pcb.astype(jnp.int32)

    pos = run_start + cum
    slots = [jnp.sum(jnp.where(hot, pos, 0.0), axis=0, keepdims=True).astype(jnp.int32) for hot in hots]
    slot_ref[...] = jnp.concatenate(slots, axis=0)

    iota_r = lax.broadcasted_iota(jnp.int32, (R_CAP, TS), 0)
    sel = (iota_r == slots[0]) | (iota_r == slots[1]) | (iota_r == slots[2]) | (iota_r == slots[3])
    perm = jnp.where(sel, 1.0, 0.0).astype(jnp.bfloat16)
    xs_ref[0] = _pack_halves(_dot(perm, hb), already_bf16=True)


def _route(x1, g_ffn, w_router, b_router):
    t, d = x1.shape
    nt = t // TS
    w_pad = jnp.pad(w_router.astype(jnp.float32), ((0, 0), (0, LANES - N_EXPERTS)))
    w_hi = w_pad.astype(jnp.bfloat16)
    w_lo = (w_pad - w_hi.astype(jnp.float32)).astype(jnp.bfloat16)
    br_t = b_router.astype(jnp.float32)[:, None]
    const = lambda shape: pl.BlockSpec(shape, lambda i: (0,) * len(shape))
    return pl.pallas_call(
        _route_kernel,
        grid=(nt,),
        in_specs=[pl.BlockSpec((TS, d), lambda i: (i, 0)), const((1, d)), const(w_hi.shape), const(w_lo.shape),
                  const(br_t.shape)],
        out_specs=[
            pl.BlockSpec((1, R_CAP, d // 2), lambda i: (i, 0, 0)),
            pl.BlockSpec((TOP_K, TS), lambda i: (0, i)),
            pl.BlockSpec((TOP_K, TS), lambda i: (0, i)),
            pl.BlockSpec((1, N_EXPERTS, LANES), lambda i: (i, 0, 0)),
        ],
        out_shape=[
            jax.ShapeDtypeStruct((nt, R_CAP, d // 2), jnp.uint32),
            jax.ShapeDtypeStruct((TOP_K, t), jnp.int32),
            jax.ShapeDtypeStruct((TOP_K, t), jnp.float32),
            jax.ShapeDtypeStruct((nt, N_EXPERTS, LANES), jnp.int32),
        ],
        compiler_params=pltpu.CompilerParams(
            dimension_semantics=("arbitrary",), vmem_limit_bytes=VMEM_LIMIT),
        name="route",
    )(x1, g_ffn, w_hi, w_lo, br_t)


def _experts_kernel(src_ref, bexp_ref, nval_ref, nvc_ref,
                    xs_hbm, w1_ref, b1g_ref, b1l_ref, w2_ref, b2_ref, dei_ref,
                    y_hbm,
                    xbuf, ybuf, zbuf, w1g, w1l, w2b, sem_in, sem_out, sem_z):
    b = pl.program_id(0)
    nb = pl.num_programs(0)
    n_tiles = nvc_ref.shape[0]
    slot = b % 2

    def zero_tail(tile, start):
        n = CPT - nvc_ref[tile]
        off = tile * CPT + nvc_ref[tile]
        for pc in _TAIL_PIECES:
            take = (n & pc) != 0

            @pl.when(take)
            def _(off=off, pc=pc):
                cp = pltpu.make_async_copy(zbuf.at[pl.ds(0, pc)], y_hbm.at[pl.ds(off, pc)], sem_z)
                if start:
                    cp.start()
                else:
                    cp.wait()
            off = off + jnp.where(take, pc, 0)

    @pl.when(b == 0)
    def _():
        zbuf[...] = jnp.zeros_like(zbuf)

    @pl.when(b < n_tiles)
    def _():
        zero_tail(b, True)

    def chunk_copy(blk, sl, c, inbound):
        if inbound:
            return pltpu.make_async_copy(xs_hbm.at[src_ref[blk * CPB + c]], xbuf.at[sl, c], sem_in.at[sl])
        return pltpu.make_async_copy(ybuf.at[sl, c], y_hbm.at[src_ref[blk * CPB + c]], sem_out.at[sl])

    def start_all(blk, sl, inbound, static):
        if static:
            for c in range(CPB):
                chunk_copy(blk, sl, c, inbound).start()
        else:
            def body(c, carry):
                chunk_copy(blk, sl, c, inbound).start()
                return carry
            lax.fori_loop(0, nval_ref[blk], body, 0)

    def wait_all(blk, sl, inbound):
        n = nval_ref[blk]

        @pl.when(n == CPB)
        def _():
            if inbound:
                pltpu.make_async_copy(xs_hbm.at[pl.ds(0, CPB)], xbuf.at[sl], sem_in.at[sl]).wait()
            else:
                pltpu.make_async_copy(ybuf.at[sl], y_hbm.at[pl.ds(0, CPB)], sem_out.at[sl]).wait()

        @pl.when(n != CPB)
        def _():
            def body(c, carry):
                chunk_copy(blk, sl, c, inbound).wait()
                return carry
            lax.fori_loop(0, n, body, 0)

    def compute(n_chunks=CPB):
        rows = n_chunks * CHUNK
        xb = _unpack_halves(xbuf[slot, 0:n_chunks].reshape(rows, xbuf.shape[3]))
        hg = _dot(xb, w1g[...]) + b1g_ref[0]
        hl = _dot(xb, w1l[...]) + b1l_ref[0]
        hg = jnp.minimum(hg, SWIGLU_LIMIT)
        hl = jnp.clip(hl, -SWIGLU_LIMIT, SWIGLU_LIMIT)
        act = hg * jax.nn.sigmoid(SWIGLU_ALPHA * hg) * (hl + 1.0)
        y = _dot(act.astype(jnp.bfloat16), w2b[...]) + b2_ref[0]
        ybuf[slot, 0:n_chunks] = _pack_halves(y).reshape(n_chunks, CHUNK, y.shape[1] // 2)

    @pl.when(b == 0)
    def _():
        xbuf[...] = jnp.zeros_like(xbuf)
        start_all(0, 0, True, False)

    prev_e = bexp_ref[jnp.maximum(b - 1, 0)]

    @pl.when((b == 0) | (bexp_ref[b] != prev_e))
    def _():
        dei = dei_ref[...]
        for j in range(w1_ref.shape[2] // (2 * LANES)):
            wj = w1_ref[0, :, j * 2 * LANES:(j + 1) * 2 * LANES].astype(jnp.bfloat16)
            r = _dot(wj, dei)
            w1g[:, j * LANES:(j + 1) * LANES] = r[:, 0:LANES].astype(jnp.bfloat16)
            w1l[:, j * LANES:(j + 1) * LANES] = r[:, LANES:].astype(jnp.bfloat16)
        w2b[...] = w2_ref[0].astype(jnp.bfloat16)

    @pl.when(b >= 2)
    def _():
        wait_all(b - 2, slot, False)

    wait_all(b, slot, True)

    prev_n = nval_ref[jnp.maximum(b - 1, 0)]
    next_n = nval_ref[jnp.minimum(b + 1, nb - 1)]
    fast = (b >= 1) & (prev_n == CPB) & (b + 1 < nb) & (next_n == CPB) & (nval_ref[b] > 0)
    half = nval_ref[b] <= CPB // 2

    @pl.when(fast & jnp.logical_not(half))
    def _():
        start_all(b - 1, 1 - slot, False, True)
        start_all(b + 1, 1 - slot, True, True)
        compute()

    @pl.when(fast & half)
    def _():
        start_all(b - 1, 1 - slot, False, True)
        start_all(b + 1, 1 - slot, True, True)
        compute(CPB // 2)

    @pl.when(jnp.logical_not(fast))
    def _():
        @pl.when(b >= 1)
        def _():
            start_all(b - 1, 1 - slot, False, False)

        @pl.when(b + 1 < nb)
        def _():
            start_all(b + 1, 1 - slot, True, False)

        @pl.when(nval_ref[b] > 0)
        def _():
            compute()

    @pl.when(b < n_tiles)
    def _():
        zero_tail(b, False)

    @pl.when(b == nb - 1)
    def _():
        start_all(b, slot, False, False)

        @pl.when(b >= 1)
        def _():
            wait_all(b - 1, 1 - slot, False)
        wait_all(b, slot, False)


def _experts(chunk_src, block_expert, block_nvalid, nvc, xs_chunks, w1, b1g, b1l, w2, b2, dei):
    n_chunks, _, dh = xs_chunks.shape
    nb = block_expert.shape[0]
    assert nb >= nvc.shape[0]
    d, de2 = w1.shape[1], w1.shape[2]
    grid_spec = pltpu.PrefetchScalarGridSpec(
        num_scalar_prefetch=4,
        grid=(nb,),
        in_specs=[
            pl.BlockSpec(memory_space=pl.ANY),
            pl.BlockSpec((1, d, de2), lambda i, src, be, nv, nc: (be[i], 0, 0)),
            pl.BlockSpec((1, 1, de2 // 2), lambda i, src, be, nv, nc: (be[i], 0, 0)),
            pl.BlockSpec((1, 1, de2 // 2), lambda i, src, be, nv, nc: (be[i], 0, 0)),
            pl.BlockSpec((1, de2 // 2, d), lambda i, src, be, nv, nc: (be[i], 0, 0)),
            pl.BlockSpec((1, 1, d), lambda i, src, be, nv, nc: (be[i], 0, 0)),
            pl.BlockSpec(dei.shape, lambda i, src, be, nv, nc: (0, 0)),
        ],
        out_specs=pl.BlockSpec(memory_space=pl.ANY),
        scratch_shapes=[
            pltpu.VMEM((2, CPB, CHUNK, dh), jnp.uint32),
            pltpu.VMEM((2, CPB, CHUNK, dh), jnp.uint32),
            pltpu.VMEM((TAIL_MAX, CHUNK, dh), jnp.uint32),
            pltpu.VMEM((d, de2 // 2), jnp.bfloat16),
            pltpu.VMEM((d, de2 // 2), jnp.bfloat16),
            pltpu.VMEM((de2 // 2, d), jnp.bfloat16),
            pltpu.SemaphoreType.DMA((2,)),
            pltpu.SemaphoreType.DMA((2,)),
            pltpu.SemaphoreType.DMA(()),
        ],
    )
    return pl.pallas_call(
        _experts_kernel,
        grid_spec=grid_spec,
        out_shape=jax.ShapeDtypeStruct(xs_chunks.shape, jnp.uint32),
        compiler_params=pltpu.CompilerParams(
            dimension_semantics=("arbitrary",), vmem_limit_bytes=VMEM_LIMIT),
        name="experts",
    )(chunk_src, block_expert, block_nvalid, nvc, xs_chunks, w1, b1g, b1l, w2, b2, dei)


_PIECES = tuple(1 << i for i in reversed(range((CPT).bit_length())))


def _combine_kernel(nvc_ref,
                    y_hbm, x1_ref, slot_ref, gate_ref, gfin_ref, o_ref, ybuf, sem):
    i = pl.program_id(0)
    nt = pl.num_programs(0)
    sl = i % 2

    def fetch(tile, s_, start):
        n = nvc_ref[tile]
        off = jnp.int32(0)
        for pc in _PIECES:
            if pc * CHUNK > R_CAP:
                continue
            take = (n & pc) != 0

            @pl.when(take)
            def _(off=off, pc=pc):
                cp = pltpu.make_async_copy(
                    y_hbm.at[tile, pl.ds(pl.multiple_of(off * CHUNK, CHUNK), pc * CHUNK)],
                    ybuf.at[s_, pl.ds(pl.multiple_of(off * CHUNK, CHUNK), pc * CHUNK)], sem.at[s_])
                if start:
                    cp.start()
                else:
                    cp.wait()
            off = off + jnp.where(take, pc, 0)

    @pl.when(i == 0)
    def _():
        ybuf[...] = jnp.zeros_like(ybuf)
        fetch(0, 0, True)

    @pl.when(i + 1 < nt)
    def _():
        fetch(i + 1, 1 - sl, True)

    fetch(i, sl, False)
    iota_r = lax.broadcasted_iota(jnp.int32, (R_CAP, TS), 0)
    gt = jnp.zeros((R_CAP, TS), jnp.float32)
    for k in range(TOP_K):
        gt = jnp.where(iota_r == slot_ref[k:k + 1, :], gate_ref[k:k + 1, :], gt)
    moe = lax.dot_general(gt.astype(jnp.bfloat16), _unpack_halves(ybuf[sl]),
                          (((0,), (0,)), ((), ())), preferred_element_type=jnp.float32)
    o_ref[...] = _rms(x1_ref[...] + moe, gfin_ref[...])


def _combine(nvc, y_tiles, x1, slot_t, gate_t, g_final):
    t, d = x1.shape
    nt = t // TS
    grid_spec = pltpu.PrefetchScalarGridSpec(
        num_scalar_prefetch=1,
        grid=(nt,),
        in_specs=[
            pl.BlockSpec(memory_space=pl.ANY),
            pl.BlockSpec((TS, d), lambda i, nv: (i, 0)),
            pl.BlockSpec((TOP_K, TS), lambda i, nv: (0, i)),
            pl.BlockSpec((TOP_K, TS), lambda i, nv: (0, i)),
            pl.BlockSpec((1, d), lambda i, nv: (0, 0)),
        ],
        out_specs=pl.BlockSpec((TS, d), lambda i, nv: (i, 0)),
        scratch_shapes=[pltpu.VMEM((2, R_CAP, d // 2), jnp.uint32), pltpu.SemaphoreType.DMA((2,))],
    )
    return pl.pallas_call(
        _combine_kernel,
        grid_spec=grid_spec,
        out_shape=jax.ShapeDtypeStruct((t, d), jnp.float32),
        compiler_params=pltpu.CompilerParams(
            dimension_semantics=("arbitrary",), vmem_limit_bytes=VMEM_LIMIT),
        name="combine",
    )(nvc, y_tiles, x1, slot_t, gate_t, g_final)


def _chunk_plan(pcc, nb):
    nt = pcc.shape[0]
    i32 = jnp.int32
    pcc_t = pcc.T
    run_start_t = (jnp.cumsum(pcc, axis=1) - pcc).T
    cum_incl = jnp.cumsum(pcc_t, axis=1)
    eblocks = (cum_incl[:, -1] + CPB - 1) // CPB
    bstart = jnp.cumsum(eblocks) - eblocks
    blk = jnp.arange(nb, dtype=i32)
    bexp = jnp.sum(bstart[None, :] <= blk[:, None], axis=1).astype(i32) - 1
    oh = bexp[:, None] == jnp.arange(N_EXPERTS, dtype=i32)[None, :]
    pick = lambda tab: jnp.sum(jnp.where(oh[:, :, None], tab[None], 0), axis=1)
    ci, pc_row, rs_row = pick(cum_incl), pick(pcc_t), pick(run_start_t)
    b0 = jnp.sum(jnp.where(oh, bstart[None, :], 0), axis=1)
    p = (blk - b0)[:, None] * CPB + jnp.arange(CPB, dtype=i32)[None, :]
    before = ci[:, None, :] <= p[:, :, None]
    tile = jnp.sum(before, axis=2).astype(i32)
    cum_excl = jnp.sum(jnp.where(before, pc_row[:, None, :], 0), axis=2)
    at_tile = jnp.arange(nt, dtype=i32)[None, None, :] == tile[:, :, None]
    rs = jnp.sum(jnp.where(at_tile, rs_row[:, None, :], 0), axis=2)
    valid = tile < nt
    src = jnp.where(valid, tile * CPT + rs + (p - cum_excl), 0).astype(i32).reshape(-1)
    nvalid = jnp.sum(valid, axis=1).astype(i32)
    return src, bexp, nvalid


def kernel(x, g_mix, w_in, b_in, sinks, w_dw, b_dw, ln_g, ln_b, g_attn_out, g_conv_out, w_out, b_out,
           g_ffn, w_router, b_router, w1, b1, w2, b2, g_final):
    bsz, seq, d = x.shape
    t = bsz * seq
    nt = t // TS
    depth = g_mix.shape[0]
    assert depth == 1, "combine fuses the final RMSNorm, so exactly one layer is supported"
    f32 = jnp.float32
    max_chunks = nt * ((TS * TOP_K + N_EXPERTS * (CHUNK - 1)) // CHUNK) + N_EXPERTS * (CPB - 1)
    nb = -(-max_chunks // CPB)
    ci = jnp.arange(2 * LANES)
    dei = (ci[:, None] == jnp.where(ci < LANES, 2 * ci, 2 * (ci - LANES) + 1)[None, :]).astype(jnp.bfloat16)

    for l in range(depth):
        sink_cols = jnp.repeat(sinks[l].astype(f32).reshape(N_KV_HEADS, Q_PER_KV), WINDOW, axis=1)[..., None]
        x1 = _mix(x, g_mix[l][None], w_in[l].astype(jnp.bfloat16), b_in[l][None], sink_cols,
                  w_dw[l], b_dw[l][None], ln_g[l][None], ln_b[l][None],
                  g_attn_out[l][None], g_conv_out[l][None], w_out[l].astype(jnp.bfloat16), b_out[l][None])
        x1 = x1.reshape(t, d)
        xs, slot_t, gate_t, pcc = _route(x1, g_ffn[l][None], w_router[l], b_router[l])
        pcc = pcc[:, :, 0]
        src, bexp, nvalid = _chunk_plan(pcc, nb)
        b1l = b1[l].reshape(N_EXPERTS, 1, -1, 2)
        nvc = jnp.sum(pcc, axis=1).astype(jnp.int32)
        y = _experts(src, bexp, nvalid, nvc, xs.reshape(nt * CPT, CHUNK, d // 2), w1[l], b1l[..., 0], b1l[..., 1],
                     w2[l], b2[l][:, None, :], dei)
        x = _combine(nvc, y.reshape(nt, R_CAP, d // 2), x1, slot_t, gate_t, g_final[None]).reshape(bsz, seq, d)
    return x
```

```python
import functools

import jax
import jax.numpy as jnp
from jax import lax
from jax.experimental import pallas as pl
from jax.experimental.pallas import tpu as pltpu

HEAD_DIM = 64
N_Q_HEADS = 8
N_KV_HEADS = 2
Q_PER_KV = N_Q_HEADS // N_KV_HEADS
ATTN_WIDTH = N_Q_HEADS * HEAD_DIM
KV_WIDTH = N_KV_HEADS * HEAD_DIM
WINDOW = 128
CONV_KERNEL = 31
N_EXPERTS = 32
TOP_K = 4
SWIGLU_LIMIT = 7.0
SWIGLU_ALPHA = 1.702
RMS_EPS = 1e-5
LN_EPS = 1e-5

LANES = 128
SUBLANES = 8

TS = 256
CHUNK = SUBLANES
R_CAP = -(-(TS * TOP_K + N_EXPERTS * (CHUNK - 1)) // LANES) * LANES
CPT = R_CAP // CHUNK
TAIL_MAX = CPT - TS * TOP_K // CHUNK
_TAIL_PIECES = tuple(1 << i for i in reversed(range(TAIL_MAX.bit_length())))
MIX_SEQS = 2
MIX_COLS = 256
MIX_FILL_EVERY = 2
TM = 512
CPB = TM // CHUNK
CONV_HALO = 32
CONV_ROWS = 64
NEG_BIG = -1e30
VMEM_LIMIT = 56 * 1024 * 1024


def _rms(x, g):
    return x * lax.rsqrt(jnp.mean(x * x, axis=-1, keepdims=True) + RMS_EPS) * g


def _dot(a, b):
    return jnp.dot(a, b, preferred_element_type=jnp.float32)


_HI16 = 0xFFFF0000


def _pack_halves(v):
    half = v.shape[1] // 2
    as_bits = lambda t: lax.bitcast_convert_type(t.astype(jnp.bfloat16).astype(jnp.float32), jnp.uint32)
    return (as_bits(v[:, half:]) & jnp.uint32(_HI16)) | (as_bits(v[:, :half]) >> 16)


def _unpack_halves(w):
    lo = lax.bitcast_convert_type(w << 16, jnp.float32).astype(jnp.bfloat16)
    hi = lax.bitcast_convert_type(w & jnp.uint32(_HI16), jnp.float32).astype(jnp.bfloat16)
    return jnp.concatenate([lo, hi], axis=1)


def _dot_nt(a, b, precision=None):
    return lax.dot_general(a, b, (((1,), (1,)), ((), ())), precision=precision,
                           preferred_element_type=jnp.float32)


def _mix_kernel(x_ref, gmix_ref, win_ref, bin_ref, sink_ref, wdw_ref, bdw_ref, lng_ref, lnb_ref,
                ga_ref, gc_ref, wout_ref, bout_ref, o_ref, *scratch):
    params = (gmix_ref, win_ref, bin_ref, sink_ref, wdw_ref, bdw_ref, lng_ref, lnb_ref,
              ga_ref, gc_ref, wout_ref, bout_ref)
    per_seq = len(scratch) // MIX_SEQS
    seqs = [scratch[i * per_seq:(i + 1) * per_seq] for i in range(MIX_SEQS)]
    first = pl.program_id(1) == 0

    @pl.when(first)
    def _():
        for kv_prev, u_buf, *_ in seqs:
            kv_prev[...] = jnp.zeros_like(kv_prev)
            u_buf[0:CONV_HALO, :] = jnp.zeros((CONV_HALO, u_buf.shape[1]), jnp.float32)

    _emit_pipelined([_Stream(_mix_tile(x_ref.at[i], o_ref.at[i], first, params, *seqs[i]))
                     for i in range(MIX_SEQS)])


class _Stream:
    def __init__(self, gen):
        self.gen = gen
        self.tag = next(gen, None)

    def step(self):
        self.tag = next(self.gen, None)

    def run_while(self, tag):
        while self.tag == tag:
            self.step()


def _emit_pipelined(streams):
    streams[0].run_while("head")
    for i, cur in enumerate(streams):
        prv = streams[i - 1] if i >= 1 else None
        nxt = streams[i + 1] if i + 1 < len(streams) else None
        n = 0
        while cur.tag == "body":
            cur.step()
            n += 1
            if n % MIX_FILL_EVERY == 0:
                if prv is not None and prv.tag == "tail":
                    prv.step()
                elif nxt is not None and nxt.tag == "head":
                    nxt.step()
        if prv is not None:
            prv.run_while("tail")
        if nxt is not None:
            nxt.run_while("head")
    streams[-1].run_while("tail")


def _mix_tile(x_ref, o_ref, first, params, kv_prev, u_buf, u_sh, conv_buf, pbuf, mbuf):
    gmix_ref, win_ref, bin_ref, sink_ref, wdw_ref, bdw_ref, lng_ref, lnb_ref, ga_ref, gc_ref, wout_ref, bout_ref = params
    ub = None
    for c in range(win_ref.shape[1] // MIX_COLS):
        yield "head"
        if ub is None:
            ub = _rms(x_ref[0], gmix_ref[...]).astype(jnp.bfloat16)
        cs = slice(c * MIX_COLS, (c + 1) * MIX_COLS)
        pbuf[:, cs] = _dot(ub, win_ref[:, cs]) + bin_ref[:, cs]

    yield "body"
    o_q, o_k, o_v, o_a, o_g = 0, ATTN_WIDTH, ATTN_WIDTH + KV_WIDTH, ATTN_WIDTH + 2 * KV_WIDTH, \
        ATTN_WIDTH + 2 * KV_WIDTH + ATTN_WIDTH
    q = pbuf[:, o_q:o_k] * (HEAD_DIM ** -0.5)
    k = pbuf[:, o_k:o_v]
    v = pbuf[:, o_v:o_a]

    lane = lax.broadcasted_iota(jnp.int32, (WINDOW + TS, LANES), 1)
    low = lane < HEAD_DIM
    kfull = jnp.concatenate([kv_prev[:, 0:KV_WIDTH], k], axis=0)
    vfull = jnp.concatenate([kv_prev[:, KV_WIDTH:], v], axis=0)
    kroll = pltpu.roll(kfull, HEAD_DIM, axis=1)
    vroll = pltpu.roll(vfull, HEAD_DIM, axis=1)
    kk = [jnp.where(low, kfull, kroll).astype(jnp.bfloat16), jnp.where(low, kroll, kfull).astype(jnp.bfloat16)]
    vv = [jnp.where(low, vfull, vroll).astype(jnp.bfloat16), jnp.where(low, vroll, vfull).astype(jnp.bfloat16)]
    kv_prev[:, 0:KV_WIDTH] = k[TS - WINDOW:, :]
    kv_prev[:, KV_WIDTH:] = v[TS - WINDOW:, :]

    rows = Q_PER_KV * WINDOW
    qi = lax.broadcasted_iota(jnp.int32, (rows, 2 * WINDOW), 0) & (WINDOW - 1)
    kj = lax.broadcasted_iota(jnp.int32, (rows, 2 * WINDOW), 1)
    band = (kj > qi) & (kj <= qi + WINDOW)
    lane_q = lax.broadcasted_iota(jnp.int32, (WINDOW, LANES), 1)
    low_q = lane_q < HEAD_DIM
    ones_kv = jnp.ones((2 * WINDOW, LANES), jnp.bfloat16)

    def attn_unit(b, g):
        qb = q[b * WINDOW:(b + 1) * WINDOW, :]
        valid = band & ((kj >= WINDOW) | jnp.logical_not(first & (b == 0)))
        parts = []
        for i in range(Q_PER_KV):
            h = g * Q_PER_KV + i
            q128 = qb[:, (h // 2) * LANES:(h // 2 + 1) * LANES]
            keep = low_q if h % 2 == 0 else jnp.logical_not(low_q)
            parts.append(jnp.where(keep, q128, 0.0))
        qs = jnp.concatenate(parts, axis=0).astype(jnp.bfloat16)
        kb = kk[g][b * WINDOW:b * WINDOW + 2 * WINDOW, :]
        vb = vv[g][b * WINDOW:b * WINDOW + 2 * WINDOW, :]
        sc = _dot_nt(qs, kb)
        sc = jnp.where(valid, sc, NEG_BIG)
        sink = sink_ref[g]
        m = jnp.maximum(jnp.max(sc, axis=-1, keepdims=True), sink)
        p = jnp.exp(sc - m).astype(jnp.bfloat16)
        den = _dot(p, ones_kv) + jnp.exp(sink - m)
        pv = _dot(p, vb)
        o = pv / den
        return [o[i * WINDOW:(i + 1) * WINDOW, :] for i in range(Q_PER_KV)]

    cw = u_buf.shape[1]
    u_buf[CONV_HALO:CONV_HALO + TS, :] = pbuf[:, o_a:o_g] * jax.nn.sigmoid(pbuf[:, o_g:])
    n_sh = u_sh.shape[1]
    for s in range(1, SUBLANES):
        u_sh[s - 1] = u_buf[s:s + n_sh, :]
    shift = CONV_HALO - (CONV_KERNEL - 1)

    def conv_strip(c, r0):
        cs = slice(c * LANES, (c + 1) * LANES)
        acc = jnp.zeros((CONV_ROWS, LANES), jnp.float32)
        for j in range(CONV_KERNEL):
            base, s = (shift + j) // SUBLANES * SUBLANES, (shift + j) % SUBLANES
            rs = slice(r0 + base, r0 + base + CONV_ROWS)
            win = u_buf[rs, cs] if s == 0 else u_sh[s - 1, rs, cs]
            acc = acc + wdw_ref[j:j + 1, cs] * win
        conv_buf[r0:r0 + CONV_ROWS, cs] = acc

    units = [(b, g) for b in range(TS // WINDOW) for g in range(N_KV_HEADS)]
    strips = [(c, r0) for c in range(cw // LANES) for r0 in range(0, TS, CONV_ROWS)]
    per_unit = -(-len(strips) // len(units))
    head_out = {}
    for n, (b, g) in enumerate(units):
        yield "body"
        head_out[b, g] = attn_unit(b, g)
        for c, r0 in strips[n * per_unit:(n + 1) * per_unit]:
            yield "body"
            conv_strip(c, r0)

    yield "body"
    attn_blocks = []
    for b in range(TS // WINDOW):
        heads = [o for g in range(N_KV_HEADS) for o in head_out[b, g]]
        cols = [jnp.where(low_q, heads[2 * j], heads[2 * j + 1]) for j in range(N_Q_HEADS // 2)]
        attn_blocks.append(jnp.concatenate(cols, axis=1))
    attn = jnp.concatenate(attn_blocks, axis=0)

    u_buf[0:CONV_HALO, :] = u_buf[TS:TS + CONV_HALO, :]
    cv = conv_buf[...] + bdw_ref[...]
    mu = jnp.mean(cv, axis=-1, keepdims=True)
    xc = cv - mu
    cv = xc * lax.rsqrt(jnp.mean(xc * xc, axis=-1, keepdims=True) + LN_EPS) * lng_ref[...] + lnb_ref[...]
    cv = cv * jax.nn.sigmoid(cv)

    mbuf[...] = jnp.concatenate([_rms(attn, ga_ref[...]), _rms(cv, gc_ref[...])], axis=1).astype(jnp.bfloat16)

    for c in range(wout_ref.shape[1] // MIX_COLS):
        yield "tail"
        cs = slice(c * MIX_COLS, (c + 1) * MIX_COLS)
        o_ref[0, :, cs] = x_ref[0, :, cs] + _dot(mbuf[...], wout_ref[:, cs]) + bout_ref[:, cs]


def _mix(x, g_mix, w_in, b_in, sink_cols, w_dw, b_dw, ln_g, ln_b, g_a, g_c, w_out, b_out):
    bsz, seq, d = x.shape
    assert seq % TS == 0 and TS % WINDOW == 0 and CONV_HALO >= CONV_KERNEL - 1 and bsz % MIX_SEQS == 0
    cw = w_dw.shape[1]
    const = lambda shape: pl.BlockSpec(shape, lambda b, s: (0,) * len(shape))
    x = x.reshape(MIX_SEQS, bsz // MIX_SEQS, seq, d)
    seq_scratch = [
        pltpu.VMEM((WINDOW, 2 * KV_WIDTH), jnp.float32),
        pltpu.VMEM((CONV_HALO + TS, cw), jnp.float32),
        pltpu.VMEM((SUBLANES - 1, CONV_HALO + TS - SUBLANES, cw), jnp.float32),
        pltpu.VMEM((TS, cw), jnp.float32),
        pltpu.VMEM((TS, w_in.shape[1]), jnp.float32),
        pltpu.VMEM((TS, w_out.shape[0]), jnp.bfloat16),
    ]
    return pl.pallas_call(
        _mix_kernel,
        grid=(bsz // MIX_SEQS, seq // TS),
        in_specs=[
            pl.BlockSpec((MIX_SEQS, 1, TS, d), lambda b, s: (0, b, s, 0)),
            const((1, d)), const(w_in.shape), const((1, w_in.shape[1])),
            const(sink_cols.shape), const(w_dw.shape), const((1, cw)), const((1, cw)), const((1, cw)),
            const((1, ATTN_WIDTH)), const((1, cw)), const(w_out.shape), const((1, d)),
        ],
        out_specs=pl.BlockSpec((MIX_SEQS, 1, TS, d), lambda b, s: (0, b, s, 0)),
        out_shape=jax.ShapeDtypeStruct(x.shape, jnp.float32),
        scratch_shapes=seq_scratch * MIX_SEQS,
        compiler_params=pltpu.CompilerParams(
            dimension_semantics=("arbitrary", "arbitrary"), vmem_limit_bytes=VMEM_LIMIT),
        name="mix",
    )(x, g_mix, w_in, b_in, sink_cols, w_dw, b_dw, ln_g, ln_b, g_a, g_c, w_out, b_out).reshape(bsz, seq, d)


def _route_kernel(x1_ref, gffn_ref, whi_ref, wlo_ref, brt_ref, xs_ref, slot_ref, gate_ref, pcc_ref):
    h = _rms(x1_ref[...], gffn_ref[...])
    hb = h.astype(jnp.bfloat16)
    h_lo = (h - hb.astype(jnp.float32)).astype(jnp.bfloat16)
    lg_t = _dot(hb, whi_ref[...]) + (_dot(hb, wlo_ref[...]) + _dot(h_lo, whi_ref[...]))
    lg = lg_t.T[0:N_EXPERTS, :] + brt_ref[...]
    iota_e = lax.broadcasted_iota(jnp.int32, (N_EXPERTS, TS), 0)
    vals, hots = [], []
    member = jnp.zeros((N_EXPERTS, TS), jnp.float32)
    for _ in range(TOP_K):
        mx = jnp.max(lg, axis=0, keepdims=True)
        idx = jnp.min(jnp.where(lg == mx, iota_e, N_EXPERTS), axis=0, keepdims=True)
        hot = iota_e == idx
        lg = jnp.where(hot, -jnp.inf, lg)
        member = member + hot.astype(jnp.float32)
        vals.append(mx)
        hots.append(hot)
    ex = [jnp.exp(vk - vals[0]) for vk in vals]
    den = ex[0] + ex[1] + ex[2] + ex[3]
    gate_ref[...] = jnp.concatenate([e / den for e in ex], axis=0)

    ti = lax.broadcasted_iota(jnp.int32, (TS, TS), 0)
    tj = lax.broadcasted_iota(jnp.int32, (TS, TS), 1)
    upper = jnp.where(ti < tj, 1.0, 0.0).astype(jnp.bfloat16)
    cum = _dot(member.astype(jnp.bfloat16), upper)
    cnt = jnp.sum(member, axis=1, keepdims=True)
    pcc = jnp.floor((cnt + (CHUNK - 1)) * (1.0 / CHUNK))
    pcb = jnp.broadcast_to(pcc, (N_EXPERTS, LANES))
    row_e = lax.broadcasted_iota(jnp.int32, (N_EXPERTS, LANES), 0)
    inc = pcb
    sh = 1
    while sh < N_EXPERTS:
        inc = inc + jnp.where(row_e >= sh, pltpu.roll(inc, sh, axis=0), 0.0)
        sh *= 2
    run_start = (inc - pcb)[:, 0:1] * CHUNK
    pcc_ref[0] = pcb.astype(jnp.int32)

    pos = run_start + cum
    slots = [jnp.sum(jnp.where(hot, pos, 0.0), axis=0, keepdims=True).astype(jnp.int32) for hot in hots]
    slot_ref[...] = jnp.concatenate(slots, axis=0)

    iota_r = lax.broadcasted_iota(jnp.int32, (R_CAP, TS), 0)
    sel = (iota_r == slots[0]) | (iota_r == slots[1]) | (iota_r == slots[2]) | (iota_r == slots[3])
    perm = jnp.where(sel, 1.0, 0.0).astype(jnp.bfloat16)
    xs_ref[0] = _pack_halves(_dot(perm, hb))


def _route(x1, g_ffn, w_router, b_router):
    t, d = x1.shape
    nt = t // TS
    w_pad = jnp.pad(w_router.astype(jnp.float32), ((0, 0), (0, LANES - N_EXPERTS)))
    w_hi = w_pad.astype(jnp.bfloat16)
    w_lo = (w_pad - w_hi.astype(jnp.float32)).astype(jnp.bfloat16)
    br_t = b_router.astype(jnp.float32)[:, None]
    const = lambda shape: pl.BlockSpec(shape, lambda i: (0,) * len(shape))
    return pl.pallas_call(
        _route_kernel,
        grid=(nt,),
        in_specs=[pl.BlockSpec((TS, d), lambda i: (i, 0)), const((1, d)), const(w_hi.shape), const(w_lo.shape),
                  const(br_t.shape)],
        out_specs=[
            pl.BlockSpec((1, R_CAP, d // 2), lambda i: (i, 0, 0)),
            pl.BlockSpec((TOP_K, TS), lambda i: (0, i)),
            pl.BlockSpec((TOP_K, TS), lambda i: (0, i)),
            pl.BlockSpec((1, N_EXPERTS, LANES), lambda i: (i, 0, 0)),
        ],
        out_shape=[
            jax.ShapeDtypeStruct((nt, R_CAP, d // 2), jnp.uint32),
            jax.ShapeDtypeStruct((TOP_K, t), jnp.int32),
            jax.ShapeDtypeStruct((TOP_K, t), jnp.float32),
            jax.ShapeDtypeStruct((nt, N_EXPERTS, LANES), jnp.int32),
        ],
        compiler_params=pltpu.CompilerParams(
            dimension_semantics=("arbitrary",), vmem_limit_bytes=VMEM_LIMIT),
        name="route",
    )(x1, g_ffn, w_hi, w_lo, br_t)


def _experts_kernel(src_ref, dst_ref, bexp_ref, act_ref, nvc_ref,
                    xs_hbm, w1_ref, b1g_ref, b1l_ref, w2_ref, b2_ref, dei_ref,
                    y_hbm,
                    xbuf, ybuf, zbuf, w1g, w1l, w2b, sem_in, sem_out, sem_z):
    b = pl.program_id(0)
    nb = pl.num_programs(0)
    n_tiles = nvc_ref.shape[0]
    n_real = n_tiles * CPT
    slot = b % 2

    def zero_tail(tile, start):
        n = CPT - nvc_ref[tile]
        off = tile * CPT + nvc_ref[tile]
        for pc in _TAIL_PIECES:
            take = (n & pc) != 0

            @pl.when(take)
            def _(off=off, pc=pc):
                cp = pltpu.make_async_copy(zbuf.at[pl.ds(0, pc)], y_hbm.at[pl.ds(off, pc)], sem_z)
                if start:
                    cp.start()
                else:
                    cp.wait()
            off = off + jnp.where(take, pc, 0)

    @pl.when(b == 0)
    def _():
        zbuf[...] = jnp.zeros_like(zbuf)
        for k in range(2 * CPB // TAIL_MAX):
            cp = pltpu.make_async_copy(zbuf, y_hbm.at[pl.ds(n_real + k * TAIL_MAX, TAIL_MAX)], sem_z)
            cp.start()
            cp.wait()

    @pl.when(b < n_tiles)
    def _():
        zero_tail(b, True)

    def active(blk):
        return act_ref[blk] > 0

    def start_all(blk, sl, inbound):
        for c in range(CPB):
            if inbound:
                pltpu.make_async_copy(xs_hbm.at[src_ref[blk * CPB + c]], xbuf.at[sl, c], sem_in.at[sl]).start()
            else:
                pltpu.make_async_copy(ybuf.at[sl, c], y_hbm.at[dst_ref[blk * CPB + c]], sem_out.at[sl]).start()

    def wait_all(sl, inbound):
        if inbound:
            pltpu.make_async_copy(xs_hbm.at[pl.ds(0, CPB)], xbuf.at[sl], sem_in.at[sl]).wait()
        else:
            pltpu.make_async_copy(ybuf.at[sl], y_hbm.at[pl.ds(0, CPB)], sem_out.at[sl]).wait()

    def compute():
        xb = _unpack_halves(xbuf[slot].reshape(TM, xbuf.shape[3]))
        hg = _dot(xb, w1g[...]) + b1g_ref[0]
        hl = _dot(xb, w1l[...]) + b1l_ref[0]
        hg = jnp.minimum(hg, SWIGLU_LIMIT)
        hl = jnp.clip(hl, -SWIGLU_LIMIT, SWIGLU_LIMIT)
        act = hg * jax.nn.sigmoid(SWIGLU_ALPHA * hg) * (hl + 1.0)
        y = _dot(act.astype(jnp.bfloat16), w2b[...]) + b2_ref[0]
        ybuf[slot] = _pack_halves(y).reshape(CPB, CHUNK, y.shape[1] // 2)

    @pl.when((b == 0) & active(0))
    def _():
        start_all(0, 0, True)

    prev_e = bexp_ref[jnp.maximum(b - 1, 0)]

    @pl.when((b == 0) | (bexp_ref[b] != prev_e))
    def _():
        dei = dei_ref[...]
        for j in range(w1_ref.shape[2] // (2 * LANES)):
            wj = w1_ref[0, :, j * 2 * LANES:(j + 1) * 2 * LANES].astype(jnp.bfloat16)
            r = _dot(wj, dei)
            w1g[:, j * LANES:(j + 1) * LANES] = r[:, 0:LANES].astype(jnp.bfloat16)
            w1l[:, j * LANES:(j + 1) * LANES] = r[:, LANES:].astype(jnp.bfloat16)
        w2b[...] = w2_ref[0].astype(jnp.bfloat16)

    @pl.when((b >= 2) & active(jnp.maximum(b - 2, 0)))
    def _():
        wait_all(slot, False)

    @pl.when(active(b))
    def _():
        wait_all(slot, True)

    prev_on = (b >= 1) & active(jnp.maximum(b - 1, 0))
    next_on = (b + 1 < nb) & active(jnp.minimum(b + 1, nb - 1))
    fast = prev_on & next_on & active(b)

    @pl.when(fast)
    def _():
        start_all(b - 1, 1 - slot, False)
        start_all(b + 1, 1 - slot, True)
        compute()

    @pl.when(jnp.logical_not(fast))
    def _():
        @pl.when(prev_on)
        def _():
            start_all(b - 1, 1 - slot, False)

        @pl.when(next_on)
        def _():
            start_all(b + 1, 1 - slot, True)

        @pl.when(active(b))
        def _():
            compute()

    @pl.when(b < n_tiles)
    def _():
        zero_tail(b, False)

    @pl.when(b == nb - 1)
    def _():
        @pl.when(active(b))
        def _():
            start_all(b, slot, False)

        @pl.when(prev_on)
        def _():
            wait_all(1 - slot, False)

        @pl.when(active(b))
        def _():
            wait_all(slot, False)


def _experts(chunk_src, chunk_dst, block_expert, block_active, nvc, xs_chunks, w1, b1g, b1l, w2, b2, dei):
    n_chunks, _, dh = xs_chunks.shape
    nb = block_expert.shape[0]
    assert nb >= nvc.shape[0] and (2 * CPB) % TAIL_MAX == 0
    d, de2 = w1.shape[1], w1.shape[2]
    grid_spec = pltpu.PrefetchScalarGridSpec(
        num_scalar_prefetch=5,
        grid=(nb,),
        in_specs=[
            pl.BlockSpec(memory_space=pl.ANY),
            pl.BlockSpec((1, d, de2), lambda i, src, dst, be, on, nc: (be[i], 0, 0)),
            pl.BlockSpec((1, 1, de2 // 2), lambda i, src, dst, be, on, nc: (be[i], 0, 0)),
            pl.BlockSpec((1, 1, de2 // 2), lambda i, src, dst, be, on, nc: (be[i], 0, 0)),
            pl.BlockSpec((1, de2 // 2, d), lambda i, src, dst, be, on, nc: (be[i], 0, 0)),
            pl.BlockSpec((1, 1, d), lambda i, src, dst, be, on, nc: (be[i], 0, 0)),
            pl.BlockSpec(dei.shape, lambda i, src, dst, be, on, nc: (0, 0)),
        ],
        out_specs=pl.BlockSpec(memory_space=pl.ANY),
        scratch_shapes=[
            pltpu.VMEM((2, CPB, CHUNK, dh), jnp.uint32),
            pltpu.VMEM((2, CPB, CHUNK, dh), jnp.uint32),
            pltpu.VMEM((TAIL_MAX, CHUNK, dh), jnp.uint32),
            pltpu.VMEM((d, de2 // 2), jnp.bfloat16),
            pltpu.VMEM((d, de2 // 2), jnp.bfloat16),
            pltpu.VMEM((de2 // 2, d), jnp.bfloat16),
            pltpu.SemaphoreType.DMA((2,)),
            pltpu.SemaphoreType.DMA((2,)),
            pltpu.SemaphoreType.DMA(()),
        ],
    )
    return pl.pallas_call(
        _experts_kernel,
        grid_spec=grid_spec,
        out_shape=jax.ShapeDtypeStruct((n_chunks + 2 * CPB, CHUNK, dh), jnp.uint32),
        compiler_params=pltpu.CompilerParams(
            dimension_semantics=("arbitrary",), vmem_limit_bytes=VMEM_LIMIT),
        name="experts",
    )(chunk_src, chunk_dst, block_expert, block_active, nvc, xs_chunks, w1, b1g, b1l, w2, b2, dei)


_PIECES = tuple(1 << i for i in reversed(range((CPT).bit_length())))


def _combine_kernel(nvc_ref,
                    y_hbm, x1_ref, slot_ref, gate_ref, gfin_ref, o_ref, ybuf, sem):
    i = pl.program_id(0)
    nt = pl.num_programs(0)
    sl = i % 2

    def fetch(tile, s_, start):
        n = nvc_ref[tile]
        off = jnp.int32(0)
        for pc in _PIECES:
            if pc * CHUNK > R_CAP:
                continue
            take = (n & pc) != 0

            @pl.when(take)
            def _(off=off, pc=pc):
                cp = pltpu.make_async_copy(y_hbm.at[pl.ds(tile * CPT + off, pc)],
                                           ybuf.at[s_, pl.ds(off, pc)], sem.at[s_])
                if start:
                    cp.start()
                else:
                    cp.wait()
            off = off + jnp.where(take, pc, 0)

    @pl.when(i == 0)
    def _():
        ybuf[...] = jnp.zeros_like(ybuf)
        fetch(0, 0, True)

    @pl.when(i + 1 < nt)
    def _():
        fetch(i + 1, 1 - sl, True)

    fetch(i, sl, False)
    iota_r = lax.broadcasted_iota(jnp.int32, (R_CAP, TS), 0)
    gt = jnp.zeros((R_CAP, TS), jnp.float32)
    for k in range(TOP_K):
        gt = jnp.where(iota_r == slot_ref[k:k + 1, :], gate_ref[k:k + 1, :], gt)
    moe = lax.dot_general(gt.astype(jnp.bfloat16), _unpack_halves(ybuf[sl].reshape(R_CAP, ybuf.shape[3])),
                          (((0,), (0,)), ((), ())), preferred_element_type=jnp.float32)
    o_ref[...] = _rms(x1_ref[...] + moe, gfin_ref[...])


def _combine(nvc, y_chunks, x1, slot_t, gate_t, g_final):
    t, d = x1.shape
    nt = t // TS
    grid_spec = pltpu.PrefetchScalarGridSpec(
        num_scalar_prefetch=1,
        grid=(nt,),
        in_specs=[
            pl.BlockSpec(memory_space=pl.ANY),
            pl.BlockSpec((TS, d), lambda i, nv: (i, 0)),
            pl.BlockSpec((TOP_K, TS), lambda i, nv: (0, i)),
            pl.BlockSpec((TOP_K, TS), lambda i, nv: (0, i)),
            pl.BlockSpec((1, d), lambda i, nv: (0, 0)),
        ],
        out_specs=pl.BlockSpec((TS, d), lambda i, nv: (i, 0)),
        scratch_shapes=[pltpu.VMEM((2, CPT, CHUNK, d // 2), jnp.uint32), pltpu.SemaphoreType.DMA((2,))],
    )
    return pl.pallas_call(
        _combine_kernel,
        grid_spec=grid_spec,
        out_shape=jax.ShapeDtypeStruct((t, d), jnp.float32),
        compiler_params=pltpu.CompilerParams(
            dimension_semantics=("arbitrary",), vmem_limit_bytes=VMEM_LIMIT),
        name="combine",
    )(nvc, y_chunks, x1, slot_t, gate_t, g_final)


def _chunk_plan(pcc, nb):
    nt = pcc.shape[0]
    i32 = jnp.int32
    pcc_t = pcc.T
    run_start_t = (jnp.cumsum(pcc, axis=1) - pcc).T
    cum_incl = jnp.cumsum(pcc_t, axis=1)
    eblocks = (cum_incl[:, -1] + CPB - 1) // CPB
    bstart = jnp.cumsum(eblocks) - eblocks
    blk = jnp.arange(nb, dtype=i32)
    bexp = jnp.sum(bstart[None, :] <= blk[:, None], axis=1).astype(i32) - 1
    oh = bexp[:, None] == jnp.arange(N_EXPERTS, dtype=i32)[None, :]
    pick = lambda tab: jnp.sum(jnp.where(oh[:, :, None], tab[None], 0), axis=1)
    ci, pc_row, rs_row = pick(cum_incl), pick(pcc_t), pick(run_start_t)
    b0 = jnp.sum(jnp.where(oh, bstart[None, :], 0), axis=1)
    p = (blk - b0)[:, None] * CPB + jnp.arange(CPB, dtype=i32)[None, :]
    before = ci[:, None, :] <= p[:, :, None]
    tile = jnp.sum(before, axis=2).astype(i32)
    cum_excl = jnp.sum(jnp.where(before, pc_row[:, None, :], 0), axis=2)
    at_tile = jnp.arange(nt, dtype=i32)[None, None, :] == tile[:, :, None]
    rs = jnp.sum(jnp.where(at_tile, rs_row[:, None, :], 0), axis=2)
    valid = tile < nt
    chunk = tile * CPT + rs + (p - cum_excl)
    trash = nt * CPT + (blk % 2)[:, None] * CPB + jnp.arange(CPB, dtype=i32)[None, :]
    src = jnp.where(valid, chunk, 0).astype(i32).reshape(-1)
    dst = jnp.where(valid, chunk, trash).astype(i32).reshape(-1)
    active = jnp.any(valid, axis=1).astype(i32)
    return src, dst, bexp, active


def kernel(x, g_mix, w_in, b_in, sinks, w_dw, b_dw, ln_g, ln_b, g_attn_out, g_conv_out, w_out, b_out,
           g_ffn, w_router, b_router, w1, b1, w2, b2, g_final):
    bsz, seq, d = x.shape
    t = bsz * seq
    nt = t // TS
    depth = g_mix.shape[0]
    assert depth == 1, "combine fuses the final RMSNorm, so exactly one layer is supported"
    f32 = jnp.float32
    max_chunks = nt * ((TS * TOP_K + N_EXPERTS * (CHUNK - 1)) // CHUNK) + N_EXPERTS * (CPB - 1)
    nb = -(-max_chunks // CPB)
    ci = jnp.arange(2 * LANES)
    dei = (ci[:, None] == jnp.where(ci < LANES, 2 * ci, 2 * (ci - LANES) + 1)[None, :]).astype(jnp.bfloat16)

    for l in range(depth):
        sink_cols = jnp.repeat(sinks[l].astype(f32).reshape(N_KV_HEADS, Q_PER_KV), WINDOW, axis=1)[..., None]
        x1 = _mix(x, g_mix[l][None], w_in[l].astype(jnp.bfloat16), b_in[l][None], sink_cols,
                  w_dw[l], b_dw[l][None], ln_g[l][None], ln_b[l][None],
                  g_attn_out[l][None], g_conv_out[l][None], w_out[l].astype(jnp.bfloat16), b_out[l][None])
        x1 = x1.reshape(t, d)
        xs, slot_t, gate_t, pcc = _route(x1, g_ffn[l][None], w_router[l], b_router[l])
        pcc = pcc[:, :, 0]
        src, dst, bexp, active = _chunk_plan(pcc, nb)
        b1l = b1[l].reshape(N_EXPERTS, 1, -1, 2)
        nvc = jnp.sum(pcc, axis=1).astype(jnp.int32)
        y = _experts(src, dst, bexp, active, nvc, xs.reshape(nt * CPT, CHUNK, d // 2), w1[l], b1l[..., 0], b1l[..., 1],
                     w2[l], b2[l][:, None, :], dei)
        x = _combine(nvc, y, x1, slot_t, gate_t, g_final[None]).reshape(bsz, seq, d)
    return x
```

```python
import functools

import jax
import jax.numpy as jnp
from jax import lax
from jax.experimental import pallas as pl
from jax.experimental.pallas import tpu as pltpu

HEAD_DIM = 64
N_Q_HEADS = 8
N_KV_HEADS = 2
Q_PER_KV = N_Q_HEADS // N_KV_HEADS
ATTN_WIDTH = N_Q_HEADS * HEAD_DIM
KV_WIDTH = N_KV_HEADS * HEAD_DIM
WINDOW = 128
CONV_KERNEL = 31
N_EXPERTS = 32
TOP_K = 4
SWIGLU_LIMIT = 7.0
SWIGLU_ALPHA = 1.702
RMS_EPS = 1e-5
LN_EPS = 1e-5

LANES = 128
SUBLANES = 8

TS = 256
CHUNK = SUBLANES
R_CAP = -(-(TS * TOP_K + N_EXPERTS * (CHUNK - 1)) // LANES) * LANES
CPT = R_CAP // CHUNK
TAIL_MAX = CPT - TS * TOP_K // CHUNK
_TAIL_PIECES = tuple(1 << i for i in reversed(range(TAIL_MAX.bit_length())))
MIX_SEQS = 2
MIX_COLS = 256
MIX_FILL_EVERY = 2
TM = 512
CPB = TM // CHUNK
CONV_HALO = 32
CONV_ROWS = 64
NEG_BIG = -1e30
VMEM_LIMIT = 56 * 1024 * 1024


def _rms(x, g):
    return x * lax.rsqrt(jnp.mean(x * x, axis=-1, keepdims=True) + RMS_EPS) * g


def _dot(a, b):
    return jnp.dot(a, b, preferred_element_type=jnp.float32)


_HI16 = 0xFFFF0000


def _pack_halves(v):
    half = v.shape[1] // 2
    as_bits = lambda t: lax.bitcast_convert_type(t.astype(jnp.bfloat16).astype(jnp.float32), jnp.uint32)
    return (as_bits(v[:, half:]) & jnp.uint32(_HI16)) | (as_bits(v[:, :half]) >> 16)


def _unpack_halves(w):
    lo = lax.bitcast_convert_type(w << 16, jnp.float32).astype(jnp.bfloat16)
    hi = lax.bitcast_convert_type(w & jnp.uint32(_HI16), jnp.float32).astype(jnp.bfloat16)
    return jnp.concatenate([lo, hi], axis=1)


def _dot_nt(a, b, precision=None):
    return lax.dot_general(a, b, (((1,), (1,)), ((), ())), precision=precision,
                           preferred_element_type=jnp.float32)


def _mix_kernel(x_ref, gmix_ref, win_ref, bin_ref, sink_ref, wdw_ref, bdw_ref, lng_ref, lnb_ref,
                ga_ref, gc_ref, wout_ref, bout_ref, o_ref, *scratch):
    params = (gmix_ref, win_ref, bin_ref, sink_ref, wdw_ref, bdw_ref, lng_ref, lnb_ref,
              ga_ref, gc_ref, wout_ref, bout_ref)
    per_seq = len(scratch) // MIX_SEQS
    seqs = [scratch[i * per_seq:(i + 1) * per_seq] for i in range(MIX_SEQS)]
    first = pl.program_id(1) == 0

    @pl.when(first)
    def _():
        for kv_prev, u_buf, *_ in seqs:
            kv_prev[...] = jnp.zeros_like(kv_prev)
            u_buf[0:CONV_HALO, :] = jnp.zeros((CONV_HALO, u_buf.shape[1]), jnp.float32)

    _emit_pipelined([_Stream(_mix_tile(x_ref.at[i], o_ref.at[i], first, params, *seqs[i]))
                     for i in range(MIX_SEQS)])


class _Stream:
    def __init__(self, gen):
        self.gen = gen
        self.tag = next(gen, None)

    def step(self):
        self.tag = next(self.gen, None)

    def run_while(self, tag):
        while self.tag == tag:
            self.step()


def _emit_pipelined(streams):
    streams[0].run_while("head")
    for i, cur in enumerate(streams):
        prv = streams[i - 1] if i >= 1 else None
        nxt = streams[i + 1] if i + 1 < len(streams) else None
        n = 0
        while cur.tag == "body":
            cur.step()
            n += 1
            if n % MIX_FILL_EVERY == 0:
                if prv is not None and prv.tag == "tail":
                    prv.step()
                elif nxt is not None and nxt.tag == "head":
                    nxt.step()
        if prv is not None:
            prv.run_while("tail")
        if nxt is not None:
            nxt.run_while("head")
    streams[-1].run_while("tail")


def _mix_tile(x_ref, o_ref, first, params, kv_prev, u_buf, u_sh, conv_buf, pbuf, mbuf):
    gmix_ref, win_ref, bin_ref, sink_ref, wdw_ref, bdw_ref, lng_ref, lnb_ref, ga_ref, gc_ref, wout_ref, bout_ref = params
    ub = None
    for c in range(win_ref.shape[1] // MIX_COLS):
        yield "head"
        if ub is None:
            ub = _rms(x_ref[0], gmix_ref[...]).astype(jnp.bfloat16)
        cs = slice(c * MIX_COLS, (c + 1) * MIX_COLS)
        pbuf[:, cs] = _dot(ub, win_ref[:, cs]) + bin_ref[:, cs]

    yield "body"
    o_q, o_k, o_v, o_a, o_g = 0, ATTN_WIDTH, ATTN_WIDTH + KV_WIDTH, ATTN_WIDTH + 2 * KV_WIDTH, \
        ATTN_WIDTH + 2 * KV_WIDTH + ATTN_WIDTH
    q = pbuf[:, o_q:o_k] * (HEAD_DIM ** -0.5)
    k = pbuf[:, o_k:o_v]
    v = pbuf[:, o_v:o_a]

    lane = lax.broadcasted_iota(jnp.int32, (WINDOW + TS, LANES), 1)
    low = lane < HEAD_DIM
    kfull = jnp.concatenate([kv_prev[:, 0:KV_WIDTH], k], axis=0)
    vfull = jnp.concatenate([kv_prev[:, KV_WIDTH:], v], axis=0)
    kroll = pltpu.roll(kfull, HEAD_DIM, axis=1)
    vroll = pltpu.roll(vfull, HEAD_DIM, axis=1)
    kk = [jnp.where(low, kfull, kroll).astype(jnp.bfloat16), jnp.where(low, kroll, kfull).astype(jnp.bfloat16)]
    vv = [jnp.where(low, vfull, vroll).astype(jnp.bfloat16), jnp.where(low, vroll, vfull).astype(jnp.bfloat16)]
    kv_prev[:, 0:KV_WIDTH] = k[TS - WINDOW:, :]
    kv_prev[:, KV_WIDTH:] = v[TS - WINDOW:, :]

    rows = Q_PER_KV * WINDOW
    qi = lax.broadcasted_iota(jnp.int32, (rows, 2 * WINDOW), 0) & (WINDOW - 1)
    kj = lax.broadcasted_iota(jnp.int32, (rows, 2 * WINDOW), 1)
    band = (kj > qi) & (kj <= qi + WINDOW)
    lane_q = lax.broadcasted_iota(jnp.int32, (WINDOW, LANES), 1)
    low_q = lane_q < HEAD_DIM
    ones_kv = jnp.ones((2 * WINDOW, LANES), jnp.bfloat16)

    def attn_unit(b, g):
        qb = q[b * WINDOW:(b + 1) * WINDOW, :]
        valid = band & ((kj >= WINDOW) | jnp.logical_not(first & (b == 0)))
        parts = []
        for i in range(Q_PER_KV):
            h = g * Q_PER_KV + i
            q128 = qb[:, (h // 2) * LANES:(h // 2 + 1) * LANES]
            keep = low_q if h % 2 == 0 else jnp.logical_not(low_q)
            parts.append(jnp.where(keep, q128, 0.0))
        qs = jnp.concatenate(parts, axis=0).astype(jnp.bfloat16)
        kb = kk[g][b * WINDOW:b * WINDOW + 2 * WINDOW, :]
        vb = vv[g][b * WINDOW:b * WINDOW + 2 * WINDOW, :]
        sc = _dot_nt(qs, kb)
        sc = jnp.where(valid, sc, NEG_BIG)
        sink = sink_ref[g]
        m = jnp.maximum(jnp.max(sc, axis=-1, keepdims=True), sink)
        p = jnp.exp(sc - m).astype(jnp.bfloat16)
        den = _dot(p, ones_kv) + jnp.exp(sink - m)
        pv = _dot(p, vb)
        o = pv / den
        return [o[i * WINDOW:(i + 1) * WINDOW, :] for i in range(Q_PER_KV)]

    cw = u_buf.shape[1]
    u_buf[CONV_HALO:CONV_HALO + TS, :] = pbuf[:, o_a:o_g] * jax.nn.sigmoid(pbuf[:, o_g:])
    n_sh = u_sh.shape[1]
    for s in range(1, SUBLANES):
        u_sh[s - 1] = u_buf[s:s + n_sh, :]
    shift = CONV_HALO - (CONV_KERNEL - 1)

    def conv_strip(c, r0):
        cs = slice(c * LANES, (c + 1) * LANES)
        acc = jnp.zeros((CONV_ROWS, LANES), jnp.float32)
        for j in range(CONV_KERNEL):
            base, s = (shift + j) // SUBLANES * SUBLANES, (shift + j) % SUBLANES
            rs = slice(r0 + base, r0 + base + CONV_ROWS)
            win = u_buf[rs, cs] if s == 0 else u_sh[s - 1, rs, cs]
            acc = acc + wdw_ref[j:j + 1, cs] * win
        conv_buf[r0:r0 + CONV_ROWS, cs] = acc

    units = [(b, g) for b in range(TS // WINDOW) for g in range(N_KV_HEADS)]
    strips = [(c, r0) for c in range(cw // LANES) for r0 in range(0, TS, CONV_ROWS)]
    head_out = {}
    for n, (b, g) in enumerate(units):
        yield "body"
        head_out[b, g] = attn_unit(b, g)
    for c, r0 in strips:
        yield "body"
        conv_strip(c, r0)

    yield "body"
    attn_blocks = []
    for b in range(TS // WINDOW):
        heads = [o for g in range(N_KV_HEADS) for o in head_out[b, g]]
        cols = [jnp.where(low_q, heads[2 * j], heads[2 * j + 1]) for j in range(N_Q_HEADS // 2)]
        attn_blocks.append(jnp.concatenate(cols, axis=1))
    attn = jnp.concatenate(attn_blocks, axis=0)

    u_buf[0:CONV_HALO, :] = u_buf[TS:TS + CONV_HALO, :]
    cv = conv_buf[...] + bdw_ref[...]
    mu = jnp.mean(cv, axis=-1, keepdims=True)
    xc = cv - mu
    cv = xc * lax.rsqrt(jnp.mean(xc * xc, axis=-1, keepdims=True) + LN_EPS) * lng_ref[...] + lnb_ref[...]
    cv = cv * jax.nn.sigmoid(cv)

    mbuf[...] = jnp.concatenate([_rms(attn, ga_ref[...]), _rms(cv, gc_ref[...])], axis=1).astype(jnp.bfloat16)

    for c in range(wout_ref.shape[1] // MIX_COLS):
        yield "tail"
        cs = slice(c * MIX_COLS, (c + 1) * MIX_COLS)
        o_ref[0, :, cs] = x_ref[0, :, cs] + _dot(mbuf[...], wout_ref[:, cs]) + bout_ref[:, cs]


def _mix(x, g_mix, w_in, b_in, sink_cols, w_dw, b_dw, ln_g, ln_b, g_a, g_c, w_out, b_out):
    bsz, seq, d = x.shape
    assert seq % TS == 0 and TS % WINDOW == 0 and CONV_HALO >= CONV_KERNEL - 1 and bsz % MIX_SEQS == 0
    cw = w_dw.shape[1]
    const = lambda shape: pl.BlockSpec(shape, lambda b, s: (0,) * len(shape))
    x = x.reshape(MIX_SEQS, bsz // MIX_SEQS, seq, d)
    seq_scratch = [
        pltpu.VMEM((WINDOW, 2 * KV_WIDTH), jnp.float32),
        pltpu.VMEM((CONV_HALO + TS, cw), jnp.float32),
        pltpu.VMEM((SUBLANES - 1, CONV_HALO + TS - SUBLANES, cw), jnp.float32),
        pltpu.VMEM((TS, cw), jnp.float32),
        pltpu.VMEM((TS, w_in.shape[1]), jnp.float32),
        pltpu.VMEM((TS, w_out.shape[0]), jnp.bfloat16),
    ]
    return pl.pallas_call(
        _mix_kernel,
        grid=(bsz // MIX_SEQS, seq // TS),
        in_specs=[
            pl.BlockSpec((MIX_SEQS, 1, TS, d), lambda b, s: (0, b, s, 0)),
            const((1, d)), const(w_in.shape), const((1, w_in.shape[1])),
            const(sink_cols.shape), const(w_dw.shape), const((1, cw)), const((1, cw)), const((1, cw)),
            const((1, ATTN_WIDTH)), const((1, cw)), const(w_out.shape), const((1, d)),
        ],
        out_specs=pl.BlockSpec((MIX_SEQS, 1, TS, d), lambda b, s: (0, b, s, 0)),
        out_shape=jax.ShapeDtypeStruct(x.shape, jnp.float32),
        scratch_shapes=seq_scratch * MIX_SEQS,
        compiler_params=pltpu.CompilerParams(
            dimension_semantics=("arbitrary", "arbitrary"), vmem_limit_bytes=VMEM_LIMIT),
        name="mix",
    )(x, g_mix, w_in, b_in, sink_cols, w_dw, b_dw, ln_g, ln_b, g_a, g_c, w_out, b_out).reshape(bsz, seq, d)


def _route_kernel(x1_ref, gffn_ref, whi_ref, wlo_ref, brt_ref, xs_ref, slot_ref, gate_ref, pcc_ref):
    h = _rms(x1_ref[...], gffn_ref[...])
    hb = h.astype(jnp.bfloat16)
    h_lo = (h - hb.astype(jnp.float32)).astype(jnp.bfloat16)
    lg_t = _dot(hb, whi_ref[...]) + (_dot(hb, wlo_ref[...]) + _dot(h_lo, whi_ref[...]))
    lg = lg_t.T[0:N_EXPERTS, :] + brt_ref[...]
    iota_e = lax.broadcasted_iota(jnp.int32, (N_EXPERTS, TS), 0)
    vals, hots = [], []
    member = jnp.zeros((N_EXPERTS, TS), jnp.float32)
    for _ in range(TOP_K):
        mx = jnp.max(lg, axis=0, keepdims=True)
        idx = jnp.min(jnp.where(lg == mx, iota_e, N_EXPERTS), axis=0, keepdims=True)
        hot = iota_e == idx
        lg = jnp.where(hot, -jnp.inf, lg)
        member = member + hot.astype(jnp.float32)
        vals.append(mx)
        hots.append(hot)
    ex = [jnp.exp(vk - vals[0]) for vk in vals]
    den = ex[0] + ex[1] + ex[2] + ex[3]
    gate_ref[...] = jnp.concatenate([e / den for e in ex], axis=0)

    ti = lax.broadcasted_iota(jnp.int32, (TS, TS), 0)
    tj = lax.broadcasted_iota(jnp.int32, (TS, TS), 1)
    upper = jnp.where(ti < tj, 1.0, 0.0).astype(jnp.bfloat16)
    cum = _dot(member.astype(jnp.bfloat16), upper)
    cnt = jnp.sum(member, axis=1, keepdims=True)
    pcc = jnp.floor((cnt + (CHUNK - 1)) * (1.0 / CHUNK))
    pcb = jnp.broadcast_to(pcc, (N_EXPERTS, LANES))
    row_e = lax.broadcasted_iota(jnp.int32, (N_EXPERTS, LANES), 0)
    inc = pcb
    sh = 1
    while sh < N_EXPERTS:
        inc = inc + jnp.where(row_e >= sh, pltpu.roll(inc, sh, axis=0), 0.0)
        sh *= 2
    run_start = (inc - pcb)[:, 0:1] * CHUNK
    pcc_ref[0] = pcb.astype(jnp.int32)

    pos = run_start + cum
    slots = [jnp.sum(jnp.where(hot, pos, 0.0), axis=0, keepdims=True).astype(jnp.int32) for hot in hots]
    slot_ref[...] = jnp.concatenate(slots, axis=0)

    iota_r = lax.broadcasted_iota(jnp.int32, (R_CAP, TS), 0)
    sel = (iota_r == slots[0]) | (iota_r == slots[1]) | (iota_r == slots[2]) | (iota_r == slots[3])
    perm = jnp.where(sel, 1.0, 0.0).astype(jnp.bfloat16)
    xs_ref[0] = _pack_halves(_dot(perm, hb))


def _route(x1, g_ffn, w_router, b_router):
    t, d = x1.shape
    nt = t // TS
    w_pad = jnp.pad(w_router.astype(jnp.float32), ((0, 0), (0, LANES - N_EXPERTS)))
    w_hi = w_pad.astype(jnp.bfloat16)
    w_lo = (w_pad - w_hi.astype(jnp.float32)).astype(jnp.bfloat16)
    br_t = b_router.astype(jnp.float32)[:, None]
    const = lambda shape: pl.BlockSpec(shape, lambda i: (0,) * len(shape))
    return pl.pallas_call(
        _route_kernel,
        grid=(nt,),
        in_specs=[pl.BlockSpec((TS, d), lambda i: (i, 0)), const((1, d)), const(w_hi.shape), const(w_lo.shape),
                  const(br_t.shape)],
        out_specs=[
            pl.BlockSpec((1, R_CAP, d // 2), lambda i: (i, 0, 0)),
            pl.BlockSpec((TOP_K, TS), lambda i: (0, i)),
            pl.BlockSpec((TOP_K, TS), lambda i: (0, i)),
            pl.BlockSpec((1, N_EXPERTS, LANES), lambda i: (i, 0, 0)),
        ],
        out_shape=[
            jax.ShapeDtypeStruct((nt, R_CAP, d // 2), jnp.uint32),
            jax.ShapeDtypeStruct((TOP_K, t), jnp.int32),
            jax.ShapeDtypeStruct((TOP_K, t), jnp.float32),
            jax.ShapeDtypeStruct((nt, N_EXPERTS, LANES), jnp.int32),
        ],
        compiler_params=pltpu.CompilerParams(
            dimension_semantics=("arbitrary",), vmem_limit_bytes=VMEM_LIMIT),
        name="route",
    )(x1, g_ffn, w_hi, w_lo, br_t)


def _experts_kernel(src_ref, dst_ref, bexp_ref, act_ref, nvc_ref,
                    xs_hbm, w1_ref, b1g_ref, b1l_ref, w2_ref, b2_ref, dei_ref,
                    y_hbm,
                    xbuf, ybuf, zbuf, w1g, w1l, w2b, sem_in, sem_out, sem_z):
    b = pl.program_id(0)
    nb = pl.num_programs(0)
    n_tiles = nvc_ref.shape[0]
    n_real = n_tiles * CPT
    slot = b % 2

    def zero_tail(tile, start):
        n = CPT - nvc_ref[tile]
        off = tile * CPT + nvc_ref[tile]
        for pc in _TAIL_PIECES:
            take = (n & pc) != 0

            @pl.when(take)
            def _(off=off, pc=pc):
                cp = pltpu.make_async_copy(zbuf.at[pl.ds(0, pc)], y_hbm.at[pl.ds(off, pc)], sem_z)
                if start:
                    cp.start()
                else:
                    cp.wait()
            off = off + jnp.where(take, pc, 0)

    @pl.when(b == 0)
    def _():
        zbuf[...] = jnp.zeros_like(zbuf)
        for k in range(2 * CPB // TAIL_MAX):
            cp = pltpu.make_async_copy(zbuf, y_hbm.at[pl.ds(n_real + k * TAIL_MAX, TAIL_MAX)], sem_z)
            cp.start()
            cp.wait()

    @pl.when(b < n_tiles)
    def _():
        zero_tail(b, True)

    def active(blk):
        return act_ref[blk] > 0

    def start_all(blk, sl, inbound):
        for c in range(CPB):
            if inbound:
                pltpu.make_async_copy(xs_hbm.at[src_ref[blk * CPB + c]], xbuf.at[sl, c], sem_in.at[sl]).start()
            else:
                pltpu.make_async_copy(ybuf.at[sl, c], y_hbm.at[dst_ref[blk * CPB + c]], sem_out.at[sl]).start()

    def wait_all(sl, inbound):
        if inbound:
            pltpu.make_async_copy(xs_hbm.at[pl.ds(0, CPB)], xbuf.at[sl], sem_in.at[sl]).wait()
        else:
            pltpu.make_async_copy(ybuf.at[sl], y_hbm.at[pl.ds(0, CPB)], sem_out.at[sl]).wait()

    def compute():
        xb = _unpack_halves(xbuf[slot].reshape(TM, xbuf.shape[3]))
        hg = _dot(xb, w1g[...]) + b1g_ref[0]
        hl = _dot(xb, w1l[...]) + b1l_ref[0]
        hg = jnp.minimum(hg, SWIGLU_LIMIT)
        hl = jnp.clip(hl, -SWIGLU_LIMIT, SWIGLU_LIMIT)
        act = hg * jax.nn.sigmoid(SWIGLU_ALPHA * hg) * (hl + 1.0)
        y = _dot(act.astype(jnp.bfloat16), w2b[...]) + b2_ref[0]
        ybuf[slot] = _pack_halves(y).reshape(CPB, CHUNK, y.shape[1] // 2)

    @pl.when((b == 0) & active(0))
    def _():
        start_all(0, 0, True)

    prev_e = bexp_ref[jnp.maximum(b - 1, 0)]

    @pl.when((b == 0) | (bexp_ref[b] != prev_e))
    def _():
        dei = dei_ref[...]
        for j in range(w1_ref.shape[2] // (2 * LANES)):
            wj = w1_ref[0, :, j * 2 * LANES:(j + 1) * 2 * LANES].astype(jnp.bfloat16)
            r = _dot(wj, dei)
            w1g[:, j * LANES:(j + 1) * LANES] = r[:, 0:LANES].astype(jnp.bfloat16)
            w1l[:, j * LANES:(j + 1) * LANES] = r[:, LANES:].astype(jnp.bfloat16)
        w2b[...] = w2_ref[0].astype(jnp.bfloat16)

    @pl.when((b >= 2) & active(jnp.maximum(b - 2, 0)))
    def _():
        wait_all(slot, False)

    @pl.when(active(b))
    def _():
        wait_all(slot, True)

    prev_on = (b >= 1) & active(jnp.maximum(b - 1, 0))
    next_on = (b + 1 < nb) & active(jnp.minimum(b + 1, nb - 1))
    fast = prev_on & next_on & active(b)

    @pl.when(fast)
    def _():
        start_all(b - 1, 1 - slot, False)
        start_all(b + 1, 1 - slot, True)
        compute()

    @pl.when(jnp.logical_not(fast))
    def _():
        @pl.when(prev_on)
        def _():
            start_all(b - 1, 1 - slot, False)

        @pl.when(next_on)
        def _():
            start_all(b + 1, 1 - slot, True)

        @pl.when(active(b))
        def _():
            compute()

    @pl.when(b < n_tiles)
    def _():
        zero_tail(b, False)

    @pl.when(b == nb - 1)
    def _():
        @pl.when(active(b))
        def _():
            start_all(b, slot, False)

        @pl.when(prev_on)
        def _():
            wait_all(1 - slot, False)

        @pl.when(active(b))
        def _():
            wait_all(slot, False)


def _experts(chunk_src, chunk_dst, block_expert, block_active, nvc, xs_chunks, w1, b1g, b1l, w2, b2, dei):
    n_chunks, _, dh = xs_chunks.shape
    nb = block_expert.shape[0]
    assert nb >= nvc.shape[0] and (2 * CPB) % TAIL_MAX == 0
    d, de2 = w1.shape[1], w1.shape[2]
    grid_spec = pltpu.PrefetchScalarGridSpec(
        num_scalar_prefetch=5,
        grid=(nb,),
        in_specs=[
            pl.BlockSpec(memory_space=pl.ANY),
            pl.BlockSpec((1, d, de2), lambda i, src, dst, be, on, nc: (be[i], 0, 0)),
            pl.BlockSpec((1, 1, de2 // 2), lambda i, src, dst, be, on, nc: (be[i], 0, 0)),
            pl.BlockSpec((1, 1, de2 // 2), lambda i, src, dst, be, on, nc: (be[i], 0, 0)),
            pl.BlockSpec((1, de2 // 2, d), lambda i, src, dst, be, on, nc: (be[i], 0, 0)),
            pl.BlockSpec((1, 1, d), lambda i, src, dst, be, on, nc: (be[i], 0, 0)),
            pl.BlockSpec(dei.shape, lambda i, src, dst, be, on, nc: (0, 0)),
        ],
        out_specs=pl.BlockSpec(memory_space=pl.ANY),
        scratch_shapes=[
            pltpu.VMEM((2, CPB, CHUNK, dh), jnp.uint32),
            pltpu.VMEM((2, CPB, CHUNK, dh), jnp.uint32),
            pltpu.VMEM((TAIL_MAX, CHUNK, dh), jnp.uint32),
            pltpu.VMEM((d, de2 // 2), jnp.bfloat16),
            pltpu.VMEM((d, de2 // 2), jnp.bfloat16),
            pltpu.VMEM((de2 // 2, d), jnp.bfloat16),
            pltpu.SemaphoreType.DMA((2,)),
            pltpu.SemaphoreType.DMA((2,)),
            pltpu.SemaphoreType.DMA(()),
        ],
    )
    return pl.pallas_call(
        _experts_kernel,
        grid_spec=grid_spec,
        out_shape=jax.ShapeDtypeStruct((n_chunks + 2 * CPB, CHUNK, dh), jnp.uint32),
        compiler_params=pltpu.CompilerParams(
            dimension_semantics=("arbitrary",), vmem_limit_bytes=VMEM_LIMIT),
        name="experts",
    )(chunk_src, chunk_dst, block_expert, block_active, nvc, xs_chunks, w1, b1g, b1l, w2, b2, dei)


_PIECES = tuple(1 << i for i in reversed(range((CPT).bit_length())))


def _combine_kernel(nvc_ref,
                    y_hbm, x1_ref, slot_ref, gate_ref, gfin_ref, o_ref, ybuf, sem):
    i = pl.program_id(0)
    nt = pl.num_programs(0)
    sl = i % 2

    def fetch(tile, s_, start):
        n = nvc_ref[tile]
        off = jnp.int32(0)
        for pc in _PIECES:
            if pc * CHUNK > R_CAP:
                continue
            take = (n & pc) != 0

            @pl.when(take)
            def _(off=off, pc=pc):
                cp = pltpu.make_async_copy(y_hbm.at[pl.ds(tile * CPT + off, pc)],
                                           ybuf.at[s_, pl.ds(off, pc)], sem.at[s_])
                if start:
                    cp.start()
                else:
                    cp.wait()
            off = off + jnp.where(take, pc, 0)

    @pl.when(i == 0)
    def _():
        ybuf[...] = jnp.zeros_like(ybuf)
        fetch(0, 0, True)

    @pl.when(i + 1 < nt)
    def _():
        fetch(i + 1, 1 - sl, True)

    fetch(i, sl, False)
    iota_r = lax.broadcasted_iota(jnp.int32, (R_CAP, TS), 0)
    gt = jnp.zeros((R_CAP, TS), jnp.float32)
    for k in range(TOP_K):
        gt = jnp.where(iota_r == slot_ref[k:k + 1, :], gate_ref[k:k + 1, :], gt)
    moe = lax.dot_general(gt.astype(jnp.bfloat16), _unpack_halves(ybuf[sl].reshape(R_CAP, ybuf.shape[3])),
                          (((0,), (0,)), ((), ())), preferred_element_type=jnp.float32)
    o_ref[...] = _rms(x1_ref[...] + moe, gfin_ref[...])


def _combine(nvc, y_chunks, x1, slot_t, gate_t, g_final):
    t, d = x1.shape
    nt = t // TS
    grid_spec = pltpu.PrefetchScalarGridSpec(
        num_scalar_prefetch=1,
        grid=(nt,),
        in_specs=[
            pl.BlockSpec(memory_space=pl.ANY),
            pl.BlockSpec((TS, d), lambda i, nv: (i, 0)),
            pl.BlockSpec((TOP_K, TS), lambda i, nv: (0, i)),
            pl.BlockSpec((TOP_K, TS), lambda i, nv: (0, i)),
            pl.BlockSpec((1, d), lambda i, nv: (0, 0)),
        ],
        out_specs=pl.BlockSpec((TS, d), lambda i, nv: (i, 0)),
        scratch_shapes=[pltpu.VMEM((2, CPT, CHUNK, d // 2), jnp.uint32), pltpu.SemaphoreType.DMA((2,))],
    )
    return pl.pallas_call(
        _combine_kernel,
        grid_spec=grid_spec,
        out_shape=jax.ShapeDtypeStruct((t, d), jnp.float32),
        compiler_params=pltpu.CompilerParams(
            dimension_semantics=("arbitrary",), vmem_limit_bytes=VMEM_LIMIT),
        name="combine",
    )(nvc, y_chunks, x1, slot_t, gate_t, g_final)


def _chunk_plan(pcc, nb):
    nt = pcc.shape[0]
    i32 = jnp.int32
    pcc_t = pcc.T
    run_start_t = (jnp.cumsum(pcc, axis=1) - pcc).T
    cum_incl = jnp.cumsum(pcc_t, axis=1)
    eblocks = (cum_incl[:, -1] + CPB - 1) // CPB
    bstart = jnp.cumsum(eblocks) - eblocks
    blk = jnp.arange(nb, dtype=i32)
    bexp = jnp.sum(bstart[None, :] <= blk[:, None], axis=1).astype(i32) - 1
    oh = bexp[:, None] == jnp.arange(N_EXPERTS, dtype=i32)[None, :]
    pick = lambda tab: jnp.sum(jnp.where(oh[:, :, None], tab[None], 0), axis=1)
    ci, pc_row, rs_row = pick(cum_incl), pick(pcc_t), pick(run_start_t)
    b0 = jnp.sum(jnp.where(oh, bstart[None, :], 0), axis=1)
    p = (blk - b0)[:, None] * CPB + jnp.arange(CPB, dtype=i32)[None, :]
    before = ci[:, None, :] <= p[:, :, None]
    tile = jnp.sum(before, axis=2).astype(i32)
    cum_excl = jnp.sum(jnp.where(before, pc_row[:, None, :], 0), axis=2)
    at_tile = jnp.arange(nt, dtype=i32)[None, None, :] == tile[:, :, None]
    rs = jnp.sum(jnp.where(at_tile, rs_row[:, None, :], 0), axis=2)
    valid = tile < nt
    chunk = tile * CPT + rs + (p - cum_excl)
    trash = nt * CPT + (blk % 2)[:, None] * CPB + jnp.arange(CPB, dtype=i32)[None, :]
    src = jnp.where(valid, chunk, 0).astype(i32).reshape(-1)
    dst = jnp.where(valid, chunk, trash).astype(i32).reshape(-1)
    active = jnp.any(valid, axis=1).astype(i32)
    return src, dst, bexp, active


def kernel(x, g_mix, w_in, b_in, sinks, w_dw, b_dw, ln_g, ln_b, g_attn_out, g_conv_out, w_out, b_out,
           g_ffn, w_router, b_router, w1, b1, w2, b2, g_final):
    bsz, seq, d = x.shape
    t = bsz * seq
    nt = t // TS
    depth = g_mix.shape[0]
    assert depth == 1, "combine fuses the final RMSNorm, so exactly one layer is supported"
    f32 = jnp.float32
    max_chunks = nt * ((TS * TOP_K + N_EXPERTS * (CHUNK - 1)) // CHUNK) + N_EXPERTS * (CPB - 1)
    nb = -(-max_chunks // CPB)
    ci = jnp.arange(2 * LANES)
    dei = (ci[:, None] == jnp.where(ci < LANES, 2 * ci, 2 * (ci - LANES) + 1)[None, :]).astype(jnp.bfloat16)

    for l in range(depth):
        sink_cols = jnp.repeat(sinks[l].astype(f32).reshape(N_KV_HEADS, Q_PER_KV), WINDOW, axis=1)[..., None]
        x1 = _mix(x, g_mix[l][None], w_in[l].astype(jnp.bfloat16), b_in[l][None], sink_cols,
                  w_dw[l], b_dw[l][None], ln_g[l][None], ln_b[l][None],
                  g_attn_out[l][None], g_conv_out[l][None], w_out[l].astype(jnp.bfloat16), b_out[l][None])
        x1 = x1.reshape(t, d)
        xs, slot_t, gate_t, pcc = _route(x1, g_ffn[l][None], w_router[l], b_router[l])
        pcc = pcc[:, :, 0]
        src, dst, bexp, active = _chunk_plan(pcc, nb)
        b1l = b1[l].reshape(N_EXPERTS, 1, -1, 2)
        nvc = jnp.sum(pcc, axis=1).astype(jnp.int32)
        y = _experts(src, dst, bexp, active, nvc, xs.reshape(nt * CPT, CHUNK, d // 2), w1[l], b1l[..., 0], b1l[..., 1],
                     w2[l], b2[l][:, None, :], dei)
        x = _combine(nvc, y, x1, slot_t, gate_t, g_final[None]).reshape(bsz, seq, d)
    return x
```

```python
import functools

import jax
import jax.numpy as jnp
from jax import lax
from jax.experimental import pallas as pl
from jax.experimental.pallas import tpu as pltpu

HEAD_DIM = 64
N_Q_HEADS = 8
N_KV_HEADS = 2
Q_PER_KV = N_Q_HEADS // N_KV_HEADS
ATTN_WIDTH = N_Q_HEADS * HEAD_DIM
KV_WIDTH = N_KV_HEADS * HEAD_DIM
WINDOW = 128
CONV_KERNEL = 31
N_EXPERTS = 32
TOP_K = 4
SWIGLU_LIMIT = 7.0
SWIGLU_ALPHA = 1.702
RMS_EPS = 1e-5
LN_EPS = 1e-5

LANES = 128
SUBLANES = 8

TS = 256
CHUNK = SUBLANES
R_CAP = -(-(TS * TOP_K + N_EXPERTS * (CHUNK - 1)) // LANES) * LANES
CPT = R_CAP // CHUNK
TAIL_MAX = CPT - TS * TOP_K // CHUNK
_TAIL_PIECES = tuple(1 << i for i in reversed(range(TAIL_MAX.bit_length())))
MIX_SEQS = 2
MIX_COLS = 256
MIX_FILL_EVERY = 2
ROUTE_TILES = 2
TM = 512
CPB = TM // CHUNK
CONV_HALO = 32
CONV_ROWS = 64
NEG_BIG = -1e30
VMEM_LIMIT = 56 * 1024 * 1024


def _rms(x, g):
    return x * lax.rsqrt(jnp.mean(x * x, axis=-1, keepdims=True) + RMS_EPS) * g


def _dot(a, b):
    return jnp.dot(a, b, preferred_element_type=jnp.float32)


_HI16 = 0xFFFF0000


def _pack_halves(v):
    half = v.shape[1] // 2
    as_bits = lambda t: lax.bitcast_convert_type(t.astype(jnp.bfloat16).astype(jnp.float32), jnp.uint32)
    return (as_bits(v[:, half:]) & jnp.uint32(_HI16)) | (as_bits(v[:, :half]) >> 16)


def _unpack_halves(w):
    lo = lax.bitcast_convert_type(w << 16, jnp.float32).astype(jnp.bfloat16)
    hi = lax.bitcast_convert_type(w & jnp.uint32(_HI16), jnp.float32).astype(jnp.bfloat16)
    return jnp.concatenate([lo, hi], axis=1)


def _dot_nt(a, b, precision=None):
    return lax.dot_general(a, b, (((1,), (1,)), ((), ())), precision=precision,
                           preferred_element_type=jnp.float32)


def _mix_kernel(x_ref, gmix_ref, win_ref, bin_ref, sink_ref, wdw_ref, bdw_ref, lng_ref, lnb_ref,
                ga_ref, gc_ref, wout_ref, bout_ref, o_ref, *scratch):
    params = (gmix_ref, win_ref, bin_ref, sink_ref, wdw_ref, bdw_ref, lng_ref, lnb_ref,
              ga_ref, gc_ref, wout_ref, bout_ref)
    per_seq = len(scratch) // MIX_SEQS
    seqs = [scratch[i * per_seq:(i + 1) * per_seq] for i in range(MIX_SEQS)]
    first = pl.program_id(1) == 0

    @pl.when(first)
    def _():
        for kv_prev, u_buf, *_ in seqs:
            kv_prev[...] = jnp.zeros_like(kv_prev)
            u_buf[0:CONV_HALO, :] = jnp.zeros((CONV_HALO, u_buf.shape[1]), jnp.float32)

    _emit_pipelined([_Stream(_mix_tile(x_ref.at[i], o_ref.at[i], first, params, *seqs[i]))
                     for i in range(MIX_SEQS)])


class _Stream:
    def __init__(self, gen):
        self.gen = gen
        self.tag = next(gen, None)

    def step(self):
        self.tag = next(self.gen, None)

    def run_while(self, tag):
        while self.tag == tag:
            self.step()


def _emit_pipelined(streams):
    streams[0].run_while("head")
    for i, cur in enumerate(streams):
        prv = streams[i - 1] if i >= 1 else None
        nxt = streams[i + 1] if i + 1 < len(streams) else None
        n = 0
        while cur.tag == "body":
            cur.step()
            n += 1
            if n % MIX_FILL_EVERY == 0:
                if prv is not None and prv.tag == "tail":
                    prv.step()
                elif nxt is not None and nxt.tag == "head":
                    nxt.step()
        if prv is not None:
            prv.run_while("tail")
        if nxt is not None:
            nxt.run_while("head")
    streams[-1].run_while("tail")


def _mix_tile(x_ref, o_ref, first, params, kv_prev, u_buf, u_sh, conv_buf, pbuf, mbuf):
    gmix_ref, win_ref, bin_ref, sink_ref, wdw_ref, bdw_ref, lng_ref, lnb_ref, ga_ref, gc_ref, wout_ref, bout_ref = params
    ub = None
    for c in range(win_ref.shape[1] // MIX_COLS):
        yield "head"
        if ub is None:
            ub = _rms(x_ref[0], gmix_ref[...]).astype(jnp.bfloat16)
        cs = slice(c * MIX_COLS, (c + 1) * MIX_COLS)
        pbuf[:, cs] = _dot(ub, win_ref[:, cs]) + bin_ref[:, cs]

    yield "body"
    o_q, o_k, o_v, o_a, o_g = 0, ATTN_WIDTH, ATTN_WIDTH + KV_WIDTH, ATTN_WIDTH + 2 * KV_WIDTH, \
        ATTN_WIDTH + 2 * KV_WIDTH + ATTN_WIDTH
    q = pbuf[:, o_q:o_k] * (HEAD_DIM ** -0.5)
    k = pbuf[:, o_k:o_v]
    v = pbuf[:, o_v:o_a]

    lane = lax.broadcasted_iota(jnp.int32, (WINDOW + TS, LANES), 1)
    low = lane < HEAD_DIM
    kfull = jnp.concatenate([kv_prev[:, 0:KV_WIDTH], k], axis=0)
    vfull = jnp.concatenate([kv_prev[:, KV_WIDTH:], v], axis=0)
    kroll = pltpu.roll(kfull, HEAD_DIM, axis=1)
    vroll = pltpu.roll(vfull, HEAD_DIM, axis=1)
    kk = [jnp.where(low, kfull, kroll).astype(jnp.bfloat16), jnp.where(low, kroll, kfull).astype(jnp.bfloat16)]
    vv = [jnp.where(low, vfull, vroll).astype(jnp.bfloat16), jnp.where(low, vroll, vfull).astype(jnp.bfloat16)]
    kv_prev[:, 0:KV_WIDTH] = k[TS - WINDOW:, :]
    kv_prev[:, KV_WIDTH:] = v[TS - WINDOW:, :]

    rows = Q_PER_KV * WINDOW
    qi = lax.broadcasted_iota(jnp.int32, (rows, 2 * WINDOW), 0) & (WINDOW - 1)
    kj = lax.broadcasted_iota(jnp.int32, (rows, 2 * WINDOW), 1)
    band = (kj > qi) & (kj <= qi + WINDOW)
    lane_q = lax.broadcasted_iota(jnp.int32, (WINDOW, LANES), 1)
    low_q = lane_q < HEAD_DIM
    ones_kv = jnp.ones((2 * WINDOW, LANES), jnp.bfloat16)

    def attn_unit(b, g):
        qb = q[b * WINDOW:(b + 1) * WINDOW, :]
        valid = band & ((kj >= WINDOW) | jnp.logical_not(first & (b == 0)))
        parts = []
        for i in range(Q_PER_KV):
            h = g * Q_PER_KV + i
            q128 = qb[:, (h // 2) * LANES:(h // 2 + 1) * LANES]
            keep = low_q if h % 2 == 0 else jnp.logical_not(low_q)
            parts.append(jnp.where(keep, q128, 0.0))
        qs = jnp.concatenate(parts, axis=0).astype(jnp.bfloat16)
        kb = kk[g][b * WINDOW:b * WINDOW + 2 * WINDOW, :]
        vb = vv[g][b * WINDOW:b * WINDOW + 2 * WINDOW, :]
        sc = _dot_nt(qs, kb)
        sc = jnp.where(valid, sc, NEG_BIG)
        sink = sink_ref[g]
        m = jnp.maximum(jnp.max(sc, axis=-1, keepdims=True), sink)
        p = jnp.exp(sc - m).astype(jnp.bfloat16)
        den = _dot(p, ones_kv) + jnp.exp(sink - m)
        pv = _dot(p, vb)
        o = pv / den
        return [o[i * WINDOW:(i + 1) * WINDOW, :] for i in range(Q_PER_KV)]

    cw = u_buf.shape[1]
    u_buf[CONV_HALO:CONV_HALO + TS, :] = pbuf[:, o_a:o_g] * jax.nn.sigmoid(pbuf[:, o_g:])
    n_sh = u_sh.shape[1]
    for s in range(1, SUBLANES):
        u_sh[s - 1] = u_buf[s:s + n_sh, :]
    shift = CONV_HALO - (CONV_KERNEL - 1)

    def conv_strip(c, r0):
        cs = slice(c * LANES, (c + 1) * LANES)
        acc = jnp.zeros((CONV_ROWS, LANES), jnp.float32)
        for j in range(CONV_KERNEL):
            base, s = (shift + j) // SUBLANES * SUBLANES, (shift + j) % SUBLANES
            rs = slice(r0 + base, r0 + base + CONV_ROWS)
            win = u_buf[rs, cs] if s == 0 else u_sh[s - 1, rs, cs]
            acc = acc + wdw_ref[j:j + 1, cs] * win
        conv_buf[r0:r0 + CONV_ROWS, cs] = acc

    units = [(b, g) for b in range(TS // WINDOW) for g in range(N_KV_HEADS)]
    strips = [(c, r0) for c in range(cw // LANES) for r0 in range(0, TS, CONV_ROWS)]
    head_out = {}
    for n, (b, g) in enumerate(units):
        yield "body"
        head_out[b, g] = attn_unit(b, g)
    for c, r0 in strips:
        yield "body"
        conv_strip(c, r0)

    yield "body"
    attn_blocks = []
    for b in range(TS // WINDOW):
        heads = [o for g in range(N_KV_HEADS) for o in head_out[b, g]]
        cols = [jnp.where(low_q, heads[2 * j], heads[2 * j + 1]) for j in range(N_Q_HEADS // 2)]
        attn_blocks.append(jnp.concatenate(cols, axis=1))
    attn = jnp.concatenate(attn_blocks, axis=0)

    u_buf[0:CONV_HALO, :] = u_buf[TS:TS + CONV_HALO, :]
    cv = conv_buf[...] + bdw_ref[...]
    mu = jnp.mean(cv, axis=-1, keepdims=True)
    xc = cv - mu
    cv = xc * lax.rsqrt(jnp.mean(xc * xc, axis=-1, keepdims=True) + LN_EPS) * lng_ref[...] + lnb_ref[...]
    cv = cv * jax.nn.sigmoid(cv)

    mbuf[...] = jnp.concatenate([_rms(attn, ga_ref[...]), _rms(cv, gc_ref[...])], axis=1).astype(jnp.bfloat16)

    for c in range(wout_ref.shape[1] // MIX_COLS):
        yield "tail"
        cs = slice(c * MIX_COLS, (c + 1) * MIX_COLS)
        o_ref[0, :, cs] = x_ref[0, :, cs] + _dot(mbuf[...], wout_ref[:, cs]) + bout_ref[:, cs]


def _mix(x, g_mix, w_in, b_in, sink_cols, w_dw, b_dw, ln_g, ln_b, g_a, g_c, w_out, b_out):
    bsz, seq, d = x.shape
    assert seq % TS == 0 and TS % WINDOW == 0 and CONV_HALO >= CONV_KERNEL - 1 and bsz % MIX_SEQS == 0
    cw = w_dw.shape[1]
    const = lambda shape: pl.BlockSpec(shape, lambda b, s: (0,) * len(shape))
    x = x.reshape(MIX_SEQS, bsz // MIX_SEQS, seq, d)
    seq_scratch = [
        pltpu.VMEM((WINDOW, 2 * KV_WIDTH), jnp.float32),
        pltpu.VMEM((CONV_HALO + TS, cw), jnp.float32),
        pltpu.VMEM((SUBLANES - 1, CONV_HALO + TS - SUBLANES, cw), jnp.float32),
        pltpu.VMEM((TS, cw), jnp.float32),
        pltpu.VMEM((TS, w_in.shape[1]), jnp.float32),
        pltpu.VMEM((TS, w_out.shape[0]), jnp.bfloat16),
    ]
    return pl.pallas_call(
        _mix_kernel,
        grid=(bsz // MIX_SEQS, seq // TS),
        in_specs=[
            pl.BlockSpec((MIX_SEQS, 1, TS, d), lambda b, s: (0, b, s, 0)),
            const((1, d)), const(w_in.shape), const((1, w_in.shape[1])),
            const(sink_cols.shape), const(w_dw.shape), const((1, cw)), const((1, cw)), const((1, cw)),
            const((1, ATTN_WIDTH)), const((1, cw)), const(w_out.shape), const((1, d)),
        ],
        out_specs=pl.BlockSpec((MIX_SEQS, 1, TS, d), lambda b, s: (0, b, s, 0)),
        out_shape=jax.ShapeDtypeStruct(x.shape, jnp.float32),
        scratch_shapes=seq_scratch * MIX_SEQS,
        compiler_params=pltpu.CompilerParams(
            dimension_semantics=("arbitrary", "arbitrary"), vmem_limit_bytes=VMEM_LIMIT),
        name="mix",
    )(x, g_mix, w_in, b_in, sink_cols, w_dw, b_dw, ln_g, ln_b, g_a, g_c, w_out, b_out).reshape(bsz, seq, d)


def _route_kernel(x1_ref, gffn_ref, whi_ref, wlo_ref, brt_ref, xs_ref, slot_ref, gate_ref, pcc_ref):
    tiles = [_route_tile(i, x1_ref, gffn_ref, whi_ref, wlo_ref, brt_ref, xs_ref, slot_ref, gate_ref, pcc_ref)
             for i in range(ROUTE_TILES)]
    live = list(tiles)
    while live:
        live = [t for t in live if next(t, None) is not None]


def _route_tile(i, x1_ref, gffn_ref, whi_ref, wlo_ref, brt_ref, xs_ref, slot_ref, gate_ref, pcc_ref):
    cols = slice(i * TS, (i + 1) * TS)
    h = _rms(x1_ref[cols, :], gffn_ref[...])
    hb = h.astype(jnp.bfloat16)
    h_lo = (h - hb.astype(jnp.float32)).astype(jnp.bfloat16)
    yield "split"
    lg_t = _dot(hb, whi_ref[...]) + (_dot(hb, wlo_ref[...]) + _dot(h_lo, whi_ref[...]))
    lg = lg_t.T[0:N_EXPERTS, :] + brt_ref[...]
    yield "logits"
    iota_e = lax.broadcasted_iota(jnp.int32, (N_EXPERTS, TS), 0)
    vals, hots = [], []
    member = jnp.zeros((N_EXPERTS, TS), jnp.float32)
    for _ in range(TOP_K):
        mx = jnp.max(lg, axis=0, keepdims=True)
        idx = jnp.min(jnp.where(lg == mx, iota_e, N_EXPERTS), axis=0, keepdims=True)
        hot = iota_e == idx
        lg = jnp.where(hot, -jnp.inf, lg)
        member = member + hot.astype(jnp.float32)
        vals.append(mx)
        hots.append(hot)
    ex = [jnp.exp(vk - vals[0]) for vk in vals]
    den = ex[0] + ex[1] + ex[2] + ex[3]
    gate_ref[:, cols] = jnp.concatenate([e / den for e in ex], axis=0)
    yield "top4"

    ti = lax.broadcasted_iota(jnp.int32, (TS, TS), 0)
    tj = lax.broadcasted_iota(jnp.int32, (TS, TS), 1)
    upper = jnp.where(ti < tj, 1.0, 0.0).astype(jnp.bfloat16)
    cum = _dot(member.astype(jnp.bfloat16), upper)
    cnt = jnp.sum(member, axis=1, keepdims=True)
    pcc = jnp.floor((cnt + (CHUNK - 1)) * (1.0 / CHUNK))
    pcb = jnp.broadcast_to(pcc, (N_EXPERTS, LANES))
    row_e = lax.broadcasted_iota(jnp.int32, (N_EXPERTS, LANES), 0)
    inc = pcb
    sh = 1
    while sh < N_EXPERTS:
        inc = inc + jnp.where(row_e >= sh, pltpu.roll(inc, sh, axis=0), 0.0)
        sh *= 2
    run_start = (inc - pcb)[:, 0:1] * CHUNK
    pcc_ref[i] = pcb.astype(jnp.int32)

    pos = run_start + cum
    slots = [jnp.sum(jnp.where(hot, pos, 0.0), axis=0, keepdims=True).astype(jnp.int32) for hot in hots]
    slot_ref[:, cols] = jnp.concatenate(slots, axis=0)
    yield "slots"

    iota_r = lax.broadcasted_iota(jnp.int32, (R_CAP, TS), 0)
    sel = (iota_r == slots[0]) | (iota_r == slots[1]) | (iota_r == slots[2]) | (iota_r == slots[3])
    perm = jnp.where(sel, 1.0, 0.0).astype(jnp.bfloat16)
    xs = _dot(perm, hb)
    yield "permute"
    xs_ref[i] = _pack_halves(xs)


def _route(x1, g_ffn, w_router, b_router):
    t, d = x1.shape
    nt = t // TS
    assert nt % ROUTE_TILES == 0
    w_pad = jnp.pad(w_router.astype(jnp.float32), ((0, 0), (0, LANES - N_EXPERTS)))
    w_hi = w_pad.astype(jnp.bfloat16)
    w_lo = (w_pad - w_hi.astype(jnp.float32)).astype(jnp.bfloat16)
    br_t = b_router.astype(jnp.float32)[:, None]
    const = lambda shape: pl.BlockSpec(shape, lambda i: (0,) * len(shape))
    return pl.pallas_call(
        _route_kernel,
        grid=(nt // ROUTE_TILES,),
        in_specs=[pl.BlockSpec((ROUTE_TILES * TS, d), lambda i: (i, 0)), const((1, d)), const(w_hi.shape),
                  const(w_lo.shape), const(br_t.shape)],
        out_specs=[
            pl.BlockSpec((ROUTE_TILES, R_CAP, d // 2), lambda i: (i, 0, 0)),
            pl.BlockSpec((TOP_K, ROUTE_TILES * TS), lambda i: (0, i)),
            pl.BlockSpec((TOP_K, ROUTE_TILES * TS), lambda i: (0, i)),
            pl.BlockSpec((ROUTE_TILES, N_EXPERTS, LANES), lambda i: (i, 0, 0)),
        ],
        out_shape=[
            jax.ShapeDtypeStruct((nt, R_CAP, d // 2), jnp.uint32),
            jax.ShapeDtypeStruct((TOP_K, t), jnp.int32),
            jax.ShapeDtypeStruct((TOP_K, t), jnp.float32),
            jax.ShapeDtypeStruct((nt, N_EXPERTS, LANES), jnp.int32),
        ],
        compiler_params=pltpu.CompilerParams(
            dimension_semantics=("arbitrary",), vmem_limit_bytes=VMEM_LIMIT),
        name="route",
    )(x1, g_ffn, w_hi, w_lo, br_t)


def _experts_kernel(src_ref, dst_ref, bexp_ref, act_ref, nvc_ref,
                    xs_hbm, w1_ref, b1g_ref, b1l_ref, w2_ref, b2_ref, dei_ref,
                    y_hbm,
                    xbuf, ybuf, zbuf, w1g, w1l, w2b, sem_in, sem_out, sem_z):
    b = pl.program_id(0)
    nb = pl.num_programs(0)
    n_tiles = nvc_ref.shape[0]
    n_real = n_tiles * CPT
    slot = b % 2

    def zero_tail(tile, start):
        n = CPT - nvc_ref[tile]
        off = tile * CPT + nvc_ref[tile]
        for pc in _TAIL_PIECES:
            take = (n & pc) != 0

            @pl.when(take)
            def _(off=off, pc=pc):
                cp = pltpu.make_async_copy(zbuf.at[pl.ds(0, pc)], y_hbm.at[pl.ds(off, pc)], sem_z)
                if start:
                    cp.start()
                else:
                    cp.wait()
            off = off + jnp.where(take, pc, 0)

    @pl.when(b == 0)
    def _():
        zbuf[...] = jnp.zeros_like(zbuf)
        for k in range(2 * CPB // TAIL_MAX):
            cp = pltpu.make_async_copy(zbuf, y_hbm.at[pl.ds(n_real + k * TAIL_MAX, TAIL_MAX)], sem_z)
            cp.start()
            cp.wait()

    @pl.when(b < n_tiles)
    def _():
        zero_tail(b, True)

    def active(blk):
        return act_ref[blk] > 0

    def start_all(blk, sl, inbound):
        for c in range(CPB):
            if inbound:
                pltpu.make_async_copy(xs_hbm.at[src_ref[blk * CPB + c]], xbuf.at[sl, c], sem_in.at[sl]).start()
            else:
                pltpu.make_async_copy(ybuf.at[sl, c], y_hbm.at[dst_ref[blk * CPB + c]], sem_out.at[sl]).start()

    def wait_all(sl, inbound):
        if inbound:
            pltpu.make_async_copy(xs_hbm.at[pl.ds(0, CPB)], xbuf.at[sl], sem_in.at[sl]).wait()
        else:
            pltpu.make_async_copy(ybuf.at[sl], y_hbm.at[pl.ds(0, CPB)], sem_out.at[sl]).wait()

    def compute():
        xb = _unpack_halves(xbuf[slot].reshape(TM, xbuf.shape[3]))
        hg = _dot(xb, w1g[...]) + b1g_ref[0]
        hl = _dot(xb, w1l[...]) + b1l_ref[0]
        hg = jnp.minimum(hg, SWIGLU_LIMIT)
        hl = jnp.clip(hl, -SWIGLU_LIMIT, SWIGLU_LIMIT)
        act = hg * jax.nn.sigmoid(SWIGLU_ALPHA * hg) * (hl + 1.0)
        y = _dot(act.astype(jnp.bfloat16), w2b[...]) + b2_ref[0]
        ybuf[slot] = _pack_halves(y).reshape(CPB, CHUNK, y.shape[1] // 2)

    @pl.when((b == 0) & active(0))
    def _():
        start_all(0, 0, True)

    prev_e = bexp_ref[jnp.maximum(b - 1, 0)]

    @pl.when((b == 0) | (bexp_ref[b] != prev_e))
    def _():
        dei = dei_ref[...]
        for j in range(w1_ref.shape[2] // (2 * LANES)):
            wj = w1_ref[0, :, j * 2 * LANES:(j + 1) * 2 * LANES].astype(jnp.bfloat16)
            r = _dot(wj, dei)
            w1g[:, j * LANES:(j + 1) * LANES] = r[:, 0:LANES].astype(jnp.bfloat16)
            w1l[:, j * LANES:(j + 1) * LANES] = r[:, LANES:].astype(jnp.bfloat16)
        w2b[...] = w2_ref[0].astype(jnp.bfloat16)

    @pl.when((b >= 2) & active(jnp.maximum(b - 2, 0)))
    def _():
        wait_all(slot, False)

    @pl.when(active(b))
    def _():
        wait_all(slot, True)

    prev_on = (b >= 1) & active(jnp.maximum(b - 1, 0))
    next_on = (b + 1 < nb) & active(jnp.minimum(b + 1, nb - 1))
    fast = prev_on & next_on & active(b)

    @pl.when(fast)
    def _():
        start_all(b - 1, 1 - slot, False)
        start_all(b + 1, 1 - slot, True)
        compute()

    @pl.when(jnp.logical_not(fast))
    def _():
        @pl.when(prev_on)
        def _():
            start_all(b - 1, 1 - slot, False)

        @pl.when(next_on)
        def _():
            start_all(b + 1, 1 - slot, True)

        @pl.when(active(b))
        def _():
            compute()

    @pl.when(b < n_tiles)
    def _():
        zero_tail(b, False)

    @pl.when(b == nb - 1)
    def _():
        @pl.when(active(b))
        def _():
            start_all(b, slot, False)

        @pl.when(prev_on)
        def _():
            wait_all(1 - slot, False)

        @pl.when(active(b))
        def _():
            wait_all(slot, False)


def _experts(chunk_src, chunk_dst, block_expert, block_active, nvc, xs_chunks, w1, b1g, b1l, w2, b2, dei):
    n_chunks, _, dh = xs_chunks.shape
    nb = block_expert.shape[0]
    assert nb >= nvc.shape[0] and (2 * CPB) % TAIL_MAX == 0
    d, de2 = w1.shape[1], w1.shape[2]
    grid_spec = pltpu.PrefetchScalarGridSpec(
        num_scalar_prefetch=5,
        grid=(nb,),
        in_specs=[
            pl.BlockSpec(memory_space=pl.ANY),
            pl.BlockSpec((1, d, de2), lambda i, src, dst, be, on, nc: (be[i], 0, 0)),
            pl.BlockSpec((1, 1, de2 // 2), lambda i, src, dst, be, on, nc: (be[i], 0, 0)),
            pl.BlockSpec((1, 1, de2 // 2), lambda i, src, dst, be, on, nc: (be[i], 0, 0)),
            pl.BlockSpec((1, de2 // 2, d), lambda i, src, dst, be, on, nc: (be[i], 0, 0)),
            pl.BlockSpec((1, 1, d), lambda i, src, dst, be, on, nc: (be[i], 0, 0)),
            pl.BlockSpec(dei.shape, lambda i, src, dst, be, on, nc: (0, 0)),
        ],
        out_specs=pl.BlockSpec(memory_space=pl.ANY),
        scratch_shapes=[
            pltpu.VMEM((2, CPB, CHUNK, dh), jnp.uint32),
            pltpu.VMEM((2, CPB, CHUNK, dh), jnp.uint32),
            pltpu.VMEM((TAIL_MAX, CHUNK, dh), jnp.uint32),
            pltpu.VMEM((d, de2 // 2), jnp.bfloat16),
            pltpu.VMEM((d, de2 // 2), jnp.bfloat16),
            pltpu.VMEM((de2 // 2, d), jnp.bfloat16),
            pltpu.SemaphoreType.DMA((2,)),
            pltpu.SemaphoreType.DMA((2,)),
            pltpu.SemaphoreType.DMA(()),
        ],
    )
    return pl.pallas_call(
        _experts_kernel,
        grid_spec=grid_spec,
        out_shape=jax.ShapeDtypeStruct((n_chunks + 2 * CPB, CHUNK, dh), jnp.uint32),
        compiler_params=pltpu.CompilerParams(
            dimension_semantics=("arbitrary",), vmem_limit_bytes=VMEM_LIMIT),
        name="experts",
    )(chunk_src, chunk_dst, block_expert, block_active, nvc, xs_chunks, w1, b1g, b1l, w2, b2, dei)


_PIECES = tuple(1 << i for i in reversed(range((CPT).bit_length())))


def _combine_kernel(nvc_ref,
                    y_hbm, x1_ref, slot_ref, gate_ref, gfin_ref, o_ref, ybuf, sem):
    i = pl.program_id(0)
    nt = pl.num_programs(0)
    sl = i % 2

    def fetch(tile, s_, start):
        n = nvc_ref[tile]
        off = jnp.int32(0)
        for pc in _PIECES:
            if pc * CHUNK > R_CAP:
                continue
            take = (n & pc) != 0

            @pl.when(take)
            def _(off=off, pc=pc):
                cp = pltpu.make_async_copy(y_hbm.at[pl.ds(tile * CPT + off, pc)],
                                           ybuf.at[s_, pl.ds(off, pc)], sem.at[s_])
                if start:
                    cp.start()
                else:
                    cp.wait()
            off = off + jnp.where(take, pc, 0)

    @pl.when(i == 0)
    def _():
        ybuf[...] = jnp.zeros_like(ybuf)
        fetch(0, 0, True)

    @pl.when(i + 1 < nt)
    def _():
        fetch(i + 1, 1 - sl, True)

    fetch(i, sl, False)
    iota_r = lax.broadcasted_iota(jnp.int32, (R_CAP, TS), 0)
    gt = jnp.zeros((R_CAP, TS), jnp.float32)
    for k in range(TOP_K):
        gt = jnp.where(iota_r == slot_ref[k:k + 1, :], gate_ref[k:k + 1, :], gt)
    moe = lax.dot_general(gt.astype(jnp.bfloat16), _unpack_halves(ybuf[sl].reshape(R_CAP, ybuf.shape[3])),
                          (((0,), (0,)), ((), ())), preferred_element_type=jnp.float32)
    o_ref[...] = _rms(x1_ref[...] + moe, gfin_ref[...])


def _combine(nvc, y_chunks, x1, slot_t, gate_t, g_final):
    t, d = x1.shape
    nt = t // TS
    grid_spec = pltpu.PrefetchScalarGridSpec(
        num_scalar_prefetch=1,
        grid=(nt,),
        in_specs=[
            pl.BlockSpec(memory_space=pl.ANY),
            pl.BlockSpec((TS, d), lambda i, nv: (i, 0)),
            pl.BlockSpec((TOP_K, TS), lambda i, nv: (0, i)),
            pl.BlockSpec((TOP_K, TS), lambda i, nv: (0, i)),
            pl.BlockSpec((1, d), lambda i, nv: (0, 0)),
        ],
        out_specs=pl.BlockSpec((TS, d), lambda i, nv: (i, 0)),
        scratch_shapes=[pltpu.VMEM((2, CPT, CHUNK, d // 2), jnp.uint32), pltpu.SemaphoreType.DMA((2,))],
    )
    return pl.pallas_call(
        _combine_kernel,
        grid_spec=grid_spec,
        out_shape=jax.ShapeDtypeStruct((t, d), jnp.float32),
        compiler_params=pltpu.CompilerParams(
            dimension_semantics=("arbitrary",), vmem_limit_bytes=VMEM_LIMIT),
        name="combine",
    )(nvc, y_chunks, x1, slot_t, gate_t, g_final)


def _chunk_plan(pcc, nb):
    nt = pcc.shape[0]
    i32 = jnp.int32
    pcc_t = pcc.T
    run_start_t = (jnp.cumsum(pcc, axis=1) - pcc).T
    cum_incl = jnp.cumsum(pcc_t, axis=1)
    eblocks = (cum_incl[:, -1] + CPB - 1) // CPB
    bstart = jnp.cumsum(eblocks) - eblocks
    blk = jnp.arange(nb, dtype=i32)
    bexp = jnp.sum(bstart[None, :] <= blk[:, None], axis=1).astype(i32) - 1
    oh = bexp[:, None] == jnp.arange(N_EXPERTS, dtype=i32)[None, :]
    pick = lambda tab: jnp.sum(jnp.where(oh[:, :, None], tab[None], 0), axis=1)
    ci, pc_row, rs_row = pick(cum_incl), pick(pcc_t), pick(run_start_t)
    b0 = jnp.sum(jnp.where(oh, bstart[None, :], 0), axis=1)
    p = (blk - b0)[:, None] * CPB + jnp.arange(CPB, dtype=i32)[None, :]
    before = ci[:, None, :] <= p[:, :, None]
    tile = jnp.sum(before, axis=2).astype(i32)
    cum_excl = jnp.sum(jnp.where(before, pc_row[:, None, :], 0), axis=2)
    at_tile = jnp.arange(nt, dtype=i32)[None, None, :] == tile[:, :, None]
    rs = jnp.sum(jnp.where(at_tile, rs_row[:, None, :], 0), axis=2)
    valid = tile < nt
    chunk = tile * CPT + rs + (p - cum_excl)
    trash = nt * CPT + (blk % 2)[:, None] * CPB + jnp.arange(CPB, dtype=i32)[None, :]
    src = jnp.where(valid, chunk, 0).astype(i32).reshape(-1)
    dst = jnp.where(valid, chunk, trash).astype(i32).reshape(-1)
    active = jnp.any(valid, axis=1).astype(i32)
    return src, dst, bexp, active


def kernel(x, g_mix, w_in, b_in, sinks, w_dw, b_dw, ln_g, ln_b, g_attn_out, g_conv_out, w_out, b_out,
           g_ffn, w_router, b_router, w1, b1, w2, b2, g_final):
    bsz, seq, d = x.shape
    t = bsz * seq
    nt = t // TS
    depth = g_mix.shape[0]
    assert depth == 1, "combine fuses the final RMSNorm, so exactly one layer is supported"
    f32 = jnp.float32
    max_chunks = nt * ((TS * TOP_K + N_EXPERTS * (CHUNK - 1)) // CHUNK) + N_EXPERTS * (CPB - 1)
    nb = -(-max_chunks // CPB)
    ci = jnp.arange(2 * LANES)
    dei = (ci[:, None] == jnp.where(ci < LANES, 2 * ci, 2 * (ci - LANES) + 1)[None, :]).astype(jnp.bfloat16)

    for l in range(depth):
        sink_cols = jnp.repeat(sinks[l].astype(f32).reshape(N_KV_HEADS, Q_PER_KV), WINDOW, axis=1)[..., None]
        x1 = _mix(x, g_mix[l][None], w_in[l].astype(jnp.bfloat16), b_in[l][None], sink_cols,
                  w_dw[l], b_dw[l][None], ln_g[l][None], ln_b[l][None],
                  g_attn_out[l][None], g_conv_out[l][None], w_out[l].astype(jnp.bfloat16), b_out[l][None])
        x1 = x1.reshape(t, d)
        xs, slot_t, gate_t, pcc = _route(x1, g_ffn[l][None], w_router[l], b_router[l])
        pcc = pcc[:, :, 0]
        src, dst, bexp, active = _chunk_plan(pcc, nb)
        b1l = b1[l].reshape(N_EXPERTS, 1, -1, 2)
        nvc = jnp.sum(pcc, axis=1).astype(jnp.int32)
        y = _experts(src, dst, bexp, active, nvc, xs.reshape(nt * CPT, CHUNK, d // 2), w1[l], b1l[..., 0], b1l[..., 1],
                     w2[l], b2[l][:, None, :], dei)
        x = _combine(nvc, y, x1, slot_t, gate_t, g_final[None]).reshape(bsz, seq, d)
    return x
```

```python
import functools

import jax
import jax.numpy as jnp
from jax import lax
from jax.experimental import pallas as pl
from jax.experimental.pallas import tpu as pltpu

HEAD_DIM = 64
N_Q_HEADS = 8
N_KV_HEADS = 2
Q_PER_KV = N_Q_HEADS // N_KV_HEADS
ATTN_WIDTH = N_Q_HEADS * HEAD_DIM
KV_WIDTH = N_KV_HEADS * HEAD_DIM
WINDOW = 128
CONV_KERNEL = 31
N_EXPERTS = 32
TOP_K = 4
SWIGLU_LIMIT = 7.0
SWIGLU_ALPHA = 1.702
RMS_EPS = 1e-5
LN_EPS = 1e-5

LANES = 128
SUBLANES = 8

TS = 256
CHUNK = SUBLANES
R_CAP = -(-(TS * TOP_K + N_EXPERTS * (CHUNK - 1)) // LANES) * LANES
CPT = R_CAP // CHUNK
TAIL_MAX = CPT - TS * TOP_K // CHUNK
_TAIL_PIECES = tuple(1 << i for i in reversed(range(TAIL_MAX.bit_length())))
MIX_SEQS = 2
MIX_COLS = 256
MIX_FILL_EVERY = 2
ROUTE_TILES = 4
TM = 512
CPB = TM // CHUNK
CONV_HALO = 32
CONV_ROWS = 64
NEG_BIG = -1e30
VMEM_LIMIT = 56 * 1024 * 1024


def _rms(x, g):
    return x * lax.rsqrt(jnp.mean(x * x, axis=-1, keepdims=True) + RMS_EPS) * g


def _dot(a, b):
    return jnp.dot(a, b, preferred_element_type=jnp.float32)


_HI16 = 0xFFFF0000


def _pack_halves(v):
    half = v.shape[1] // 2
    as_bits = lambda t: lax.bitcast_convert_type(t.astype(jnp.bfloat16).astype(jnp.float32), jnp.uint32)
    return (as_bits(v[:, half:]) & jnp.uint32(_HI16)) | (as_bits(v[:, :half]) >> 16)


def _unpack_halves(w):
    lo = lax.bitcast_convert_type(w << 16, jnp.float32).astype(jnp.bfloat16)
    hi = lax.bitcast_convert_type(w & jnp.uint32(_HI16), jnp.float32).astype(jnp.bfloat16)
    return jnp.concatenate([lo, hi], axis=1)


def _dot_nt(a, b, precision=None):
    return lax.dot_general(a, b, (((1,), (1,)), ((), ())), precision=precision,
                           preferred_element_type=jnp.float32)


def _mix_kernel(x_ref, gmix_ref, win_ref, bin_ref, sink_ref, wdw_ref, bdw_ref, lng_ref, lnb_ref,
                ga_ref, gc_ref, wout_ref, bout_ref, o_ref, *scratch):
    params = (gmix_ref, win_ref, bin_ref, sink_ref, wdw_ref, bdw_ref, lng_ref, lnb_ref,
              ga_ref, gc_ref, wout_ref, bout_ref)
    per_seq = len(scratch) // MIX_SEQS
    seqs = [scratch[i * per_seq:(i + 1) * per_seq] for i in range(MIX_SEQS)]
    first = pl.program_id(1) == 0

    @pl.when(first)
    def _():
        for kv_prev, u_buf, *_ in seqs:
            kv_prev[...] = jnp.zeros_like(kv_prev)
            u_buf[0:CONV_HALO, :] = jnp.zeros((CONV_HALO, u_buf.shape[1]), jnp.float32)

    _emit_pipelined([_Stream(_mix_tile(x_ref.at[i], o_ref.at[i], first, params, *seqs[i]))
                     for i in range(MIX_SEQS)])


class _Stream:
    def __init__(self, gen):
        self.gen = gen
        self.tag = next(gen, None)

    def step(self):
        self.tag = next(self.gen, None)

    def run_while(self, tag):
        while self.tag == tag:
            self.step()


def _emit_pipelined(streams):
    streams[0].run_while("head")
    for i, cur in enumerate(streams):
        prv = streams[i - 1] if i >= 1 else None
        nxt = streams[i + 1] if i + 1 < len(streams) else None
        n = 0
        while cur.tag == "body":
            cur.step()
            n += 1
            if n % MIX_FILL_EVERY == 0:
                if prv is not None and prv.tag == "tail":
                    prv.step()
                elif nxt is not None and nxt.tag == "head":
                    nxt.step()
        if prv is not None:
            prv.run_while("tail")
        if nxt is not None:
            nxt.run_while("head")
    streams[-1].run_while("tail")


def _mix_tile(x_ref, o_ref, first, params, kv_prev, u_buf, u_sh, conv_buf, pbuf, mbuf):
    gmix_ref, win_ref, bin_ref, sink_ref, wdw_ref, bdw_ref, lng_ref, lnb_ref, ga_ref, gc_ref, wout_ref, bout_ref = params
    ub = None
    for c in range(win_ref.shape[1] // MIX_COLS):
        yield "head"
        if ub is None:
            ub = _rms(x_ref[0], gmix_ref[...]).astype(jnp.bfloat16)
        cs = slice(c * MIX_COLS, (c + 1) * MIX_COLS)
        pbuf[:, cs] = _dot(ub, win_ref[:, cs]) + bin_ref[:, cs]

    yield "body"
    o_q, o_k, o_v, o_a, o_g = 0, ATTN_WIDTH, ATTN_WIDTH + KV_WIDTH, ATTN_WIDTH + 2 * KV_WIDTH, \
        ATTN_WIDTH + 2 * KV_WIDTH + ATTN_WIDTH
    q = pbuf[:, o_q:o_k] * (HEAD_DIM ** -0.5)
    k = pbuf[:, o_k:o_v]
    v = pbuf[:, o_v:o_a]

    lane = lax.broadcasted_iota(jnp.int32, (WINDOW + TS, LANES), 1)
    low = lane < HEAD_DIM
    kfull = jnp.concatenate([kv_prev[:, 0:KV_WIDTH], k], axis=0)
    vfull = jnp.concatenate([kv_prev[:, KV_WIDTH:], v], axis=0)
    kroll = pltpu.roll(kfull, HEAD_DIM, axis=1)
    vroll = pltpu.roll(vfull, HEAD_DIM, axis=1)
    kk = [jnp.where(low, kfull, kroll).astype(jnp.bfloat16), jnp.where(low, kroll, kfull).astype(jnp.bfloat16)]
    vv = [jnp.where(low, vfull, vroll).astype(jnp.bfloat16), jnp.where(low, vroll, vfull).astype(jnp.bfloat16)]
    kv_prev[:, 0:KV_WIDTH] = k[TS - WINDOW:, :]
    kv_prev[:, KV_WIDTH:] = v[TS - WINDOW:, :]

    rows = Q_PER_KV * WINDOW
    qi = lax.broadcasted_iota(jnp.int32, (rows, 2 * WINDOW), 0) & (WINDOW - 1)
    kj = lax.broadcasted_iota(jnp.int32, (rows, 2 * WINDOW), 1)
    band = (kj > qi) & (kj <= qi + WINDOW)
    lane_q = lax.broadcasted_iota(jnp.int32, (WINDOW, LANES), 1)
    low_q = lane_q < HEAD_DIM
    ones_kv = jnp.ones((2 * WINDOW, LANES), jnp.bfloat16)

    def attn_unit(b, g):
        qb = q[b * WINDOW:(b + 1) * WINDOW, :]
        valid = band & ((kj >= WINDOW) | jnp.logical_not(first & (b == 0)))
        parts = []
        for i in range(Q_PER_KV):
            h = g * Q_PER_KV + i
            q128 = qb[:, (h // 2) * LANES:(h // 2 + 1) * LANES]
            keep = low_q if h % 2 == 0 else jnp.logical_not(low_q)
            parts.append(jnp.where(keep, q128, 0.0))
        qs = jnp.concatenate(parts, axis=0).astype(jnp.bfloat16)
        kb = kk[g][b * WINDOW:b * WINDOW + 2 * WINDOW, :]
        vb = vv[g][b * WINDOW:b * WINDOW + 2 * WINDOW, :]
        sc = _dot_nt(qs, kb)
        sc = jnp.where(valid, sc, NEG_BIG)
        sink = sink_ref[g]
        m = jnp.maximum(jnp.max(sc, axis=-1, keepdims=True), sink)
        p = jnp.exp(sc - m).astype(jnp.bfloat16)
        den = _dot(p, ones_kv) + jnp.exp(sink - m)
        pv = _dot(p, vb)
        o = pv / den
        return [o[i * WINDOW:(i + 1) * WINDOW, :] for i in range(Q_PER_KV)]

    cw = u_buf.shape[1]
    u_buf[CONV_HALO:CONV_HALO + TS, :] = pbuf[:, o_a:o_g] * jax.nn.sigmoid(pbuf[:, o_g:])
    n_sh = u_sh.shape[1]
    for s in range(1, SUBLANES):
        u_sh[s - 1] = u_buf[s:s + n_sh, :]
    shift = CONV_HALO - (CONV_KERNEL - 1)

    def conv_strip(c, r0):
        cs = slice(c * LANES, (c + 1) * LANES)
        acc = jnp.zeros((CONV_ROWS, LANES), jnp.float32)
        for j in range(CONV_KERNEL):
            base, s = (shift + j) // SUBLANES * SUBLANES, (shift + j) % SUBLANES
            rs = slice(r0 + base, r0 + base + CONV_ROWS)
            win = u_buf[rs, cs] if s == 0 else u_sh[s - 1, rs, cs]
            acc = acc + wdw_ref[j:j + 1, cs] * win
        conv_buf[r0:r0 + CONV_ROWS, cs] = acc

    units = [(b, g) for b in range(TS // WINDOW) for g in range(N_KV_HEADS)]
    strips = [(c, r0) for c in range(cw // LANES) for r0 in range(0, TS, CONV_ROWS)]
    head_out = {}
    for n, (b, g) in enumerate(units):
        yield "body"
        head_out[b, g] = attn_unit(b, g)
    for c, r0 in strips:
        yield "body"
        conv_strip(c, r0)

    yield "body"
    attn_blocks = []
    for b in range(TS // WINDOW):
        heads = [o for g in range(N_KV_HEADS) for o in head_out[b, g]]
        cols = [jnp.where(low_q, heads[2 * j], heads[2 * j + 1]) for j in range(N_Q_HEADS // 2)]
        attn_blocks.append(jnp.concatenate(cols, axis=1))
    attn = jnp.concatenate(attn_blocks, axis=0)

    u_buf[0:CONV_HALO, :] = u_buf[TS:TS + CONV_HALO, :]
    cv = conv_buf[...] + bdw_ref[...]
    mu = jnp.mean(cv, axis=-1, keepdims=True)
    xc = cv - mu
    cv = xc * lax.rsqrt(jnp.mean(xc * xc, axis=-1, keepdims=True) + LN_EPS) * lng_ref[...] + lnb_ref[...]
    cv = cv * jax.nn.sigmoid(cv)

    mbuf[...] = jnp.concatenate([_rms(attn, ga_ref[...]), _rms(cv, gc_ref[...])], axis=1).astype(jnp.bfloat16)

    for c in range(wout_ref.shape[1] // MIX_COLS):
        yield "tail"
        cs = slice(c * MIX_COLS, (c + 1) * MIX_COLS)
        o_ref[0, :, cs] = x_ref[0, :, cs] + _dot(mbuf[...], wout_ref[:, cs]) + bout_ref[:, cs]


def _mix(x, g_mix, w_in, b_in, sink_cols, w_dw, b_dw, ln_g, ln_b, g_a, g_c, w_out, b_out):
    bsz, seq, d = x.shape
    assert seq % TS == 0 and TS % WINDOW == 0 and CONV_HALO >= CONV_KERNEL - 1 and bsz % MIX_SEQS == 0
    cw = w_dw.shape[1]
    const = lambda shape: pl.BlockSpec(shape, lambda b, s: (0,) * len(shape))
    x = x.reshape(MIX_SEQS, bsz // MIX_SEQS, seq, d)
    seq_scratch = [
        pltpu.VMEM((WINDOW, 2 * KV_WIDTH), jnp.float32),
        pltpu.VMEM((CONV_HALO + TS, cw), jnp.float32),
        pltpu.VMEM((SUBLANES - 1, CONV_HALO + TS - SUBLANES, cw), jnp.float32),
        pltpu.VMEM((TS, cw), jnp.float32),
        pltpu.VMEM((TS, w_in.shape[1]), jnp.float32),
        pltpu.VMEM((TS, w_out.shape[0]), jnp.bfloat16),
    ]
    return pl.pallas_call(
        _mix_kernel,
        grid=(bsz // MIX_SEQS, seq // TS),
        in_specs=[
            pl.BlockSpec((MIX_SEQS, 1, TS, d), lambda b, s: (0, b, s, 0)),
            const((1, d)), const(w_in.shape), const((1, w_in.shape[1])),
            const(sink_cols.shape), const(w_dw.shape), const((1, cw)), const((1, cw)), const((1, cw)),
            const((1, ATTN_WIDTH)), const((1, cw)), const(w_out.shape), const((1, d)),
        ],
        out_specs=pl.BlockSpec((MIX_SEQS, 1, TS, d), lambda b, s: (0, b, s, 0)),
        out_shape=jax.ShapeDtypeStruct(x.shape, jnp.float32),
        scratch_shapes=seq_scratch * MIX_SEQS,
        compiler_params=pltpu.CompilerParams(
            dimension_semantics=("arbitrary", "arbitrary"), vmem_limit_bytes=VMEM_LIMIT),
        name="mix",
    )(x, g_mix, w_in, b_in, sink_cols, w_dw, b_dw, ln_g, ln_b, g_a, g_c, w_out, b_out).reshape(bsz, seq, d)


def _route_kernel(x1_ref, gffn_ref, whi_ref, wlo_ref, brt_ref, xs_ref, slot_ref, gate_ref, pcc_ref):
    tiles = [_route_tile(i, x1_ref, gffn_ref, whi_ref, wlo_ref, brt_ref, xs_ref, slot_ref, gate_ref, pcc_ref)
             for i in range(ROUTE_TILES)]
    live = list(tiles)
    while live:
        live = [t for t in live if next(t, None) is not None]


def _route_tile(i, x1_ref, gffn_ref, whi_ref, wlo_ref, brt_ref, xs_ref, slot_ref, gate_ref, pcc_ref):
    cols = slice(i * TS, (i + 1) * TS)
    h = _rms(x1_ref[cols, :], gffn_ref[...])
    hb = h.astype(jnp.bfloat16)
    h_lo = (h - hb.astype(jnp.float32)).astype(jnp.bfloat16)
    yield "split"
    lg_t = _dot(hb, whi_ref[...]) + (_dot(hb, wlo_ref[...]) + _dot(h_lo, whi_ref[...]))
    lg = lg_t.T[0:N_EXPERTS, :] + brt_ref[...]
    yield "logits"
    iota_e = lax.broadcasted_iota(jnp.int32, (N_EXPERTS, TS), 0)
    vals, hots = [], []
    member = jnp.zeros((N_EXPERTS, TS), jnp.float32)
    for _ in range(TOP_K):
        mx = jnp.max(lg, axis=0, keepdims=True)
        idx = jnp.min(jnp.where(lg == mx, iota_e, N_EXPERTS), axis=0, keepdims=True)
        hot = iota_e == idx
        lg = jnp.where(hot, -jnp.inf, lg)
        member = member + hot.astype(jnp.float32)
        vals.append(mx)
        hots.append(hot)
    ex = [jnp.exp(vk - vals[0]) for vk in vals]
    den = ex[0] + ex[1] + ex[2] + ex[3]
    gate_ref[:, cols] = jnp.concatenate([e / den for e in ex], axis=0)
    yield "top4"

    ti = lax.broadcasted_iota(jnp.int32, (TS, TS), 0)
    tj = lax.broadcasted_iota(jnp.int32, (TS, TS), 1)
    upper = jnp.where(ti < tj, 1.0, 0.0).astype(jnp.bfloat16)
    cum = _dot(member.astype(jnp.bfloat16), upper)
    cnt = jnp.sum(member, axis=1, keepdims=True)
    pcc = jnp.floor((cnt + (CHUNK - 1)) * (1.0 / CHUNK))
    pcb = jnp.broadcast_to(pcc, (N_EXPERTS, LANES))
    row_e = lax.broadcasted_iota(jnp.int32, (N_EXPERTS, LANES), 0)
    inc = pcb
    sh = 1
    while sh < N_EXPERTS:
        inc = inc + jnp.where(row_e >= sh, pltpu.roll(inc, sh, axis=0), 0.0)
        sh *= 2
    run_start = (inc - pcb)[:, 0:1] * CHUNK
    pcc_ref[i] = pcb.astype(jnp.int32)

    pos = run_start + cum
    slots = [jnp.sum(jnp.where(hot, pos, 0.0), axis=0, keepdims=True).astype(jnp.int32) for hot in hots]
    slot_ref[:, cols] = jnp.concatenate(slots, axis=0)
    yield "slots"

    iota_r = lax.broadcasted_iota(jnp.int32, (R_CAP, TS), 0)
    sel = (iota_r == slots[0]) | (iota_r == slots[1]) | (iota_r == slots[2]) | (iota_r == slots[3])
    perm = jnp.where(sel, 1.0, 0.0).astype(jnp.bfloat16)
    xs = _dot(perm, hb)
    yield "permute"
    xs_ref[i] = _pack_halves(xs)


def _route(x1, g_ffn, w_router, b_router):
    t, d = x1.shape
    nt = t // TS
    assert nt % ROUTE_TILES == 0
    w_pad = jnp.pad(w_router.astype(jnp.float32), ((0, 0), (0, LANES - N_EXPERTS)))
    w_hi = w_pad.astype(jnp.bfloat16)
    w_lo = (w_pad - w_hi.astype(jnp.float32)).astype(jnp.bfloat16)
    br_t = b_router.astype(jnp.float32)[:, None]
    const = lambda shape: pl.BlockSpec(shape, lambda i: (0,) * len(shape))
    return pl.pallas_call(
        _route_kernel,
        grid=(nt // ROUTE_TILES,),
        in_specs=[pl.BlockSpec((ROUTE_TILES * TS, d), lambda i: (i, 0)), const((1, d)), const(w_hi.shape),
                  const(w_lo.shape), const(br_t.shape)],
        out_specs=[
            pl.BlockSpec((ROUTE_TILES, R_CAP, d // 2), lambda i: (i, 0, 0)),
            pl.BlockSpec((TOP_K, ROUTE_TILES * TS), lambda i: (0, i)),
            pl.BlockSpec((TOP_K, ROUTE_TILES * TS), lambda i: (0, i)),
            pl.BlockSpec((ROUTE_TILES, N_EXPERTS, LANES), lambda i: (i, 0, 0)),
        ],
        out_shape=[
            jax.ShapeDtypeStruct((nt, R_CAP, d // 2), jnp.uint32),
            jax.ShapeDtypeStruct((TOP_K, t), jnp.int32),
            jax.ShapeDtypeStruct((TOP_K, t), jnp.float32),
            jax.ShapeDtypeStruct((nt, N_EXPERTS, LANES), jnp.int32),
        ],
        compiler_params=pltpu.CompilerParams(
            dimension_semantics=("arbitrary",), vmem_limit_bytes=VMEM_LIMIT),
        name="route",
    )(x1, g_ffn, w_hi, w_lo, br_t)


def _experts_kernel(src_ref, dst_ref, bexp_ref, act_ref, nvc_ref,
                    xs_hbm, w1_ref, b1g_ref, b1l_ref, w2_ref, b2_ref, dei_ref,
                    y_hbm,
                    xbuf, ybuf, zbuf, w1g, w1l, w2b, sem_in, sem_out, sem_z):
    b = pl.program_id(0)
    nb = pl.num_programs(0)
    n_tiles = nvc_ref.shape[0]
    n_real = n_tiles * CPT
    slot = b % 2

    def zero_tail(tile, start):
        n = CPT - nvc_ref[tile]
        off = tile * CPT + nvc_ref[tile]
        for pc in _TAIL_PIECES:
            take = (n & pc) != 0

            @pl.when(take)
            def _(off=off, pc=pc):
                cp = pltpu.make_async_copy(zbuf.at[pl.ds(0, pc)], y_hbm.at[pl.ds(off, pc)], sem_z)
                if start:
                    cp.start()
                else:
                    cp.wait()
            off = off + jnp.where(take, pc, 0)

    @pl.when(b == 0)
    def _():
        zbuf[...] = jnp.zeros_like(zbuf)
        for k in range(2 * CPB // TAIL_MAX):
            cp = pltpu.make_async_copy(zbuf, y_hbm.at[pl.ds(n_real + k * TAIL_MAX, TAIL_MAX)], sem_z)
            cp.start()
            cp.wait()

    @pl.when(b < n_tiles)
    def _():
        zero_tail(b, True)

    def active(blk):
        return act_ref[blk] > 0

    def start_all(blk, sl, inbound):
        for c in range(CPB):
            if inbound:
                pltpu.make_async_copy(xs_hbm.at[src_ref[blk * CPB + c]], xbuf.at[sl, c], sem_in.at[sl]).start()
            else:
                pltpu.make_async_copy(ybuf.at[sl, c], y_hbm.at[dst_ref[blk * CPB + c]], sem_out.at[sl]).start()

    def wait_all(sl, inbound):
        if inbound:
            pltpu.make_async_copy(xs_hbm.at[pl.ds(0, CPB)], xbuf.at[sl], sem_in.at[sl]).wait()
        else:
            pltpu.make_async_copy(ybuf.at[sl], y_hbm.at[pl.ds(0, CPB)], sem_out.at[sl]).wait()

    def compute():
        xb = _unpack_halves(xbuf[slot].reshape(TM, xbuf.shape[3]))
        hg = _dot(xb, w1g[...]) + b1g_ref[0]
        hl = _dot(xb, w1l[...]) + b1l_ref[0]
        hg = jnp.minimum(hg, SWIGLU_LIMIT)
        hl = jnp.clip(hl, -SWIGLU_LIMIT, SWIGLU_LIMIT)
        act = hg * jax.nn.sigmoid(SWIGLU_ALPHA * hg) * (hl + 1.0)
        y = _dot(act.astype(jnp.bfloat16), w2b[...]) + b2_ref[0]
        ybuf[slot] = _pack_halves(y).reshape(CPB, CHUNK, y.shape[1] // 2)

    @pl.when((b == 0) & active(0))
    def _():
        start_all(0, 0, True)

    prev_e = bexp_ref[jnp.maximum(b - 1, 0)]

    @pl.when((b == 0) | (bexp_ref[b] != prev_e))
    def _():
        dei = dei_ref[...]
        for j in range(w1_ref.shape[2] // (2 * LANES)):
            wj = w1_ref[0, :, j * 2 * LANES:(j + 1) * 2 * LANES].astype(jnp.bfloat16)
            r = _dot(wj, dei)
            w1g[:, j * LANES:(j + 1) * LANES] = r[:, 0:LANES].astype(jnp.bfloat16)
            w1l[:, j * LANES:(j + 1) * LANES] = r[:, LANES:].astype(jnp.bfloat16)
        w2b[...] = w2_ref[0].astype(jnp.bfloat16)

    @pl.when((b >= 2) & active(jnp.maximum(b - 2, 0)))
    def _():
        wait_all(slot, False)

    @pl.when(active(b))
    def _():
        wait_all(slot, True)

    prev_on = (b >= 1) & active(jnp.maximum(b - 1, 0))
    next_on = (b + 1 < nb) & active(jnp.minimum(b + 1, nb - 1))
    fast = prev_on & next_on & active(b)

    @pl.when(fast)
    def _():
        start_all(b - 1, 1 - slot, False)
        start_all(b + 1, 1 - slot, True)
        compute()

    @pl.when(jnp.logical_not(fast))
    def _():
        @pl.when(prev_on)
        def _():
            start_all(b - 1, 1 - slot, False)

        @pl.when(next_on)
        def _():
            start_all(b + 1, 1 - slot, True)

        @pl.when(active(b))
        def _():
            compute()

    @pl.when(b < n_tiles)
    def _():
        zero_tail(b, False)

    @pl.when(b == nb - 1)
    def _():
        @pl.when(active(b))
        def _():
            start_all(b, slot, False)

        @pl.when(prev_on)
        def _():
            wait_all(1 - slot, False)

        @pl.when(active(b))
        def _():
            wait_all(slot, False)


def _experts(chunk_src, chunk_dst, block_expert, block_active, nvc, xs_chunks, w1, b1g, b1l, w2, b2, dei):
    n_chunks, _, dh = xs_chunks.shape
    nb = block_expert.shape[0]
    assert nb >= nvc.shape[0] and (2 * CPB) % TAIL_MAX == 0
    d, de2 = w1.shape[1], w1.shape[2]
    grid_spec = pltpu.PrefetchScalarGridSpec(
        num_scalar_prefetch=5,
        grid=(nb,),
        in_specs=[
            pl.BlockSpec(memory_space=pl.ANY),
            pl.BlockSpec((1, d, de2), lambda i, src, dst, be, on, nc: (be[i], 0, 0)),
            pl.BlockSpec((1, 1, de2 // 2), lambda i, src, dst, be, on, nc: (be[i], 0, 0)),
            pl.BlockSpec((1, 1, de2 // 2), lambda i, src, dst, be, on, nc: (be[i], 0, 0)),
            pl.BlockSpec((1, de2 // 2, d), lambda i, src, dst, be, on, nc: (be[i], 0, 0)),
            pl.BlockSpec((1, 1, d), lambda i, src, dst, be, on, nc: (be[i], 0, 0)),
            pl.BlockSpec(dei.shape, lambda i, src, dst, be, on, nc: (0, 0)),
        ],
        out_specs=pl.BlockSpec(memory_space=pl.ANY),
        scratch_shapes=[
            pltpu.VMEM((2, CPB, CHUNK, dh), jnp.uint32),
            pltpu.VMEM((2, CPB, CHUNK, dh), jnp.uint32),
            pltpu.VMEM((TAIL_MAX, CHUNK, dh), jnp.uint32),
            pltpu.VMEM((d, de2 // 2), jnp.bfloat16),
            pltpu.VMEM((d, de2 // 2), jnp.bfloat16),
            pltpu.VMEM((de2 // 2, d), jnp.bfloat16),
            pltpu.SemaphoreType.DMA((2,)),
            pltpu.SemaphoreType.DMA((2,)),
            pltpu.SemaphoreType.DMA(()),
        ],
    )
    return pl.pallas_call(
        _experts_kernel,
        grid_spec=grid_spec,
        out_shape=jax.ShapeDtypeStruct((n_chunks + 2 * CPB, CHUNK, dh), jnp.uint32),
        compiler_params=pltpu.CompilerParams(
            dimension_semantics=("arbitrary",), vmem_limit_bytes=VMEM_LIMIT),
        name="experts",
    )(chunk_src, chunk_dst, block_expert, block_active, nvc, xs_chunks, w1, b1g, b1l, w2, b2, dei)


_PIECES = tuple(1 << i for i in reversed(range((CPT).bit_length())))


def _combine_kernel(nvc_ref,
                    y_hbm, x1_ref, slot_ref, gate_ref, gfin_ref, o_ref, ybuf, sem):
    i = pl.program_id(0)
    nt = pl.num_programs(0)
    sl = i % 2

    def fetch(tile, s_, start):
        n = nvc_ref[tile]
        off = jnp.int32(0)
        for pc in _PIECES:
            if pc * CHUNK > R_CAP:
                continue
            take = (n & pc) != 0

            @pl.when(take)
            def _(off=off, pc=pc):
                cp = pltpu.make_async_copy(y_hbm.at[pl.ds(tile * CPT + off, pc)],
                                           ybuf.at[s_, pl.ds(off, pc)], sem.at[s_])
                if start:
                    cp.start()
                else:
                    cp.wait()
            off = off + jnp.where(take, pc, 0)

    @pl.when(i == 0)
    def _():
        ybuf[...] = jnp.zeros_like(ybuf)
        fetch(0, 0, True)

    @pl.when(i + 1 < nt)
    def _():
        fetch(i + 1, 1 - sl, True)

    fetch(i, sl, False)
    iota_r = lax.broadcasted_iota(jnp.int32, (R_CAP, TS), 0)
    gt = jnp.zeros((R_CAP, TS), jnp.float32)
    for k in range(TOP_K):
        gt = jnp.where(iota_r == slot_ref[k:k + 1, :], gate_ref[k:k + 1, :], gt)
    moe = lax.dot_general(gt.astype(jnp.bfloat16), _unpack_halves(ybuf[sl].reshape(R_CAP, ybuf.shape[3])),
                          (((0,), (0,)), ((), ())), preferred_element_type=jnp.float32)
    o_ref[...] = _rms(x1_ref[...] + moe, gfin_ref[...])


def _combine(nvc, y_chunks, x1, slot_t, gate_t, g_final):
    t, d = x1.shape
    nt = t // TS
    grid_spec = pltpu.PrefetchScalarGridSpec(
        num_scalar_prefetch=1,
        grid=(nt,),
        in_specs=[
            pl.BlockSpec(memory_space=pl.ANY),
            pl.BlockSpec((TS, d), lambda i, nv: (i, 0)),
            pl.BlockSpec((TOP_K, TS), lambda i, nv: (0, i)),
            pl.BlockSpec((TOP_K, TS), lambda i, nv: (0, i)),
            pl.BlockSpec((1, d), lambda i, nv: (0, 0)),
        ],
        out_specs=pl.BlockSpec((TS, d), lambda i, nv: (i, 0)),
        scratch_shapes=[pltpu.VMEM((2, CPT, CHUNK, d // 2), jnp.uint32), pltpu.SemaphoreType.DMA((2,))],
    )
    return pl.pallas_call(
        _combine_kernel,
        grid_spec=grid_spec,
        out_shape=jax.ShapeDtypeStruct((t, d), jnp.float32),
        compiler_params=pltpu.CompilerParams(
            dimension_semantics=("arbitrary",), vmem_limit_bytes=VMEM_LIMIT),
        name="combine",
    )(nvc, y_chunks, x1, slot_t, gate_t, g_final)


def _chunk_plan(pcc, nb):
    nt = pcc.shape[0]
    i32 = jnp.int32
    pcc_t = pcc.T
    run_start_t = (jnp.cumsum(pcc, axis=1) - pcc).T
    cum_incl = jnp.cumsum(pcc_t, axis=1)
    eblocks = (cum_incl[:, -1] + CPB - 1) // CPB
    bstart = jnp.cumsum(eblocks) - eblocks
    blk = jnp.arange(nb, dtype=i32)
    bexp = jnp.sum(bstart[None, :] <= blk[:, None], axis=1).astype(i32) - 1
    oh = bexp[:, None] == jnp.arange(N_EXPERTS, dtype=i32)[None, :]
    pick = lambda tab: jnp.sum(jnp.where(oh[:, :, None], tab[None], 0), axis=1)
    ci, pc_row, rs_row = pick(cum_incl), pick(pcc_t), pick(run_start_t)
    b0 = jnp.sum(jnp.where(oh, bstart[None, :], 0), axis=1)
    p = (blk - b0)[:, None] * CPB + jnp.arange(CPB, dtype=i32)[None, :]
    before = ci[:, None, :] <= p[:, :, None]
    tile = jnp.sum(before, axis=2).astype(i32)
    cum_excl = jnp.sum(jnp.where(before, pc_row[:, None, :], 0), axis=2)
    at_tile = jnp.arange(nt, dtype=i32)[None, None, :] == tile[:, :, None]
    rs = jnp.sum(jnp.where(at_tile, rs_row[:, None, :], 0), axis=2)
    valid = tile < nt
    chunk = tile * CPT + rs + (p - cum_excl)
    trash = nt * CPT + (blk % 2)[:, None] * CPB + jnp.arange(CPB, dtype=i32)[None, :]
    src = jnp.where(valid, chunk, 0).astype(i32).reshape(-1)
    dst = jnp.where(valid, chunk, trash).astype(i32).reshape(-1)
    active = jnp.any(valid, axis=1).astype(i32)
    return src, dst, bexp, active


def kernel(x, g_mix, w_in, b_in, sinks, w_dw, b_dw, ln_g, ln_b, g_attn_out, g_conv_out, w_out, b_out,
           g_ffn, w_router, b_router, w1, b1, w2, b2, g_final):
    bsz, seq, d = x.shape
    t = bsz * seq
    nt = t // TS
    depth = g_mix.shape[0]
    assert depth == 1, "combine fuses the final RMSNorm, so exactly one layer is supported"
    f32 = jnp.float32
    max_chunks = nt * ((TS * TOP_K + N_EXPERTS * (CHUNK - 1)) // CHUNK) + N_EXPERTS * (CPB - 1)
    nb = -(-max_chunks // CPB)
    ci = jnp.arange(2 * LANES)
    dei = (ci[:, None] == jnp.where(ci < LANES, 2 * ci, 2 * (ci - LANES) + 1)[None, :]).astype(jnp.bfloat16)

    for l in range(depth):
        sink_cols = jnp.repeat(sinks[l].astype(f32).reshape(N_KV_HEADS, Q_PER_KV), WINDOW, axis=1)[..., None]
        x1 = _mix(x, g_mix[l][None], w_in[l].astype(jnp.bfloat16), b_in[l][None], sink_cols,
                  w_dw[l], b_dw[l][None], ln_g[l][None], ln_b[l][None],
                  g_attn_out[l][None], g_conv_out[l][None], w_out[l].astype(jnp.bfloat16), b_out[l][None])
        x1 = x1.reshape(t, d)
        xs, slot_t, gate_t, pcc = _route(x1, g_ffn[l][None], w_router[l], b_router[l])
        pcc = pcc[:, :, 0]
        src, dst, bexp, active = _chunk_plan(pcc, nb)
        b1l = b1[l].reshape(N_EXPERTS, 1, -1, 2)
        nvc = jnp.sum(pcc, axis=1).astype(jnp.int32)
        y = _experts(src, dst, bexp, active, nvc, xs.reshape(nt * CPT, CHUNK, d // 2), w1[l], b1l[..., 0], b1l[..., 1],
                     w2[l], b2[l][:, None, :], dei)
        x = _combine(nvc, y, x1, slot_t, gate_t, g_final[None]).reshape(bsz, seq, d)
    return x
```

```python
import functools

import jax
import jax.numpy as jnp
from jax import lax
from jax.experimental import pallas as pl
from jax.experimental.pallas import tpu as pltpu

HEAD_DIM = 64
N_Q_HEADS = 8
N_KV_HEADS = 2
Q_PER_KV = N_Q_HEADS // N_KV_HEADS
ATTN_WIDTH = N_Q_HEADS * HEAD_DIM
KV_WIDTH = N_KV_HEADS * HEAD_DIM
WINDOW = 128
CONV_KERNEL = 31
N_EXPERTS = 32
TOP_K = 4
SWIGLU_LIMIT = 7.0
SWIGLU_ALPHA = 1.702
RMS_EPS = 1e-5
LN_EPS = 1e-5

LANES = 128
SUBLANES = 8

TS = 256
CHUNK = SUBLANES
R_CAP = -(-(TS * TOP_K + N_EXPERTS * (CHUNK - 1)) // LANES) * LANES
CPT = R_CAP // CHUNK
TAIL_MAX = CPT - TS * TOP_K // CHUNK
_TAIL_PIECES = tuple(1 << i for i in reversed(range(TAIL_MAX.bit_length())))
MIX_SEQS = 2
MIX_COLS = 256
MIX_FILL_EVERY = 2
ROUTE_TILES = 4
COMBINE_TILES = 2
TM = 512
CPB = TM // CHUNK
CONV_HALO = 32
CONV_ROWS = 64
NEG_BIG = -1e30
VMEM_LIMIT = 56 * 1024 * 1024


def _rms(x, g):
    return x * lax.rsqrt(jnp.mean(x * x, axis=-1, keepdims=True) + RMS_EPS) * g


def _dot(a, b):
    return jnp.dot(a, b, preferred_element_type=jnp.float32)


_HI16 = 0xFFFF0000


def _pack_halves(v):
    half = v.shape[1] // 2
    as_bits = lambda t: lax.bitcast_convert_type(t.astype(jnp.bfloat16).astype(jnp.float32), jnp.uint32)
    return (as_bits(v[:, half:]) & jnp.uint32(_HI16)) | (as_bits(v[:, :half]) >> 16)


def _unpack_halves(w):
    lo = lax.bitcast_convert_type(w << 16, jnp.float32).astype(jnp.bfloat16)
    hi = lax.bitcast_convert_type(w & jnp.uint32(_HI16), jnp.float32).astype(jnp.bfloat16)
    return jnp.concatenate([lo, hi], axis=1)


def _dot_nt(a, b, precision=None):
    return lax.dot_general(a, b, (((1,), (1,)), ((), ())), precision=precision,
                           preferred_element_type=jnp.float32)


def _mix_kernel(x_ref, gmix_ref, win_ref, bin_ref, sink_ref, wdw_ref, bdw_ref, lng_ref, lnb_ref,
                ga_ref, gc_ref, wout_ref, bout_ref, o_ref, *scratch):
    params = (gmix_ref, win_ref, bin_ref, sink_ref, wdw_ref, bdw_ref, lng_ref, lnb_ref,
              ga_ref, gc_ref, wout_ref, bout_ref)
    per_seq = len(scratch) // MIX_SEQS
    seqs = [scratch[i * per_seq:(i + 1) * per_seq] for i in range(MIX_SEQS)]
    first = pl.program_id(1) == 0

    @pl.when(first)
    def _():
        for kv_prev, u_buf, *_ in seqs:
            kv_prev[...] = jnp.zeros_like(kv_prev)
            u_buf[0:CONV_HALO, :] = jnp.zeros((CONV_HALO, u_buf.shape[1]), jnp.float32)

    _emit_pipelined([_Stream(_mix_tile(x_ref.at[i], o_ref.at[i], first, params, *seqs[i]))
                     for i in range(MIX_SEQS)])


class _Stream:
    def __init__(self, gen):
        self.gen = gen
        self.tag = next(gen, None)

    def step(self):
        self.tag = next(self.gen, None)

    def run_while(self, tag):
        while self.tag == tag:
            self.step()


def _emit_pipelined(streams):
    streams[0].run_while("head")
    for i, cur in enumerate(streams):
        prv = streams[i - 1] if i >= 1 else None
        nxt = streams[i + 1] if i + 1 < len(streams) else None
        n = 0
        while cur.tag == "body":
            cur.step()
            n += 1
            if n % MIX_FILL_EVERY == 0:
                if prv is not None and prv.tag == "tail":
                    prv.step()
                elif nxt is not None and nxt.tag == "head":
                    nxt.step()
        if prv is not None:
            prv.run_while("tail")
        if nxt is not None:
            nxt.run_while("head")
    streams[-1].run_while("tail")


def _mix_tile(x_ref, o_ref, first, params, kv_prev, u_buf, u_sh, conv_buf, pbuf, mbuf):
    gmix_ref, win_ref, bin_ref, sink_ref, wdw_ref, bdw_ref, lng_ref, lnb_ref, ga_ref, gc_ref, wout_ref, bout_ref = params
    ub = None
    for c in range(win_ref.shape[1] // MIX_COLS):
        yield "head"
        if ub is None:
            ub = _rms(x_ref[0], gmix_ref[...]).astype(jnp.bfloat16)
        cs = slice(c * MIX_COLS, (c + 1) * MIX_COLS)
        pbuf[:, cs] = _dot(ub, win_ref[:, cs]) + bin_ref[:, cs]

    yield "body"
    o_q, o_k, o_v, o_a, o_g = 0, ATTN_WIDTH, ATTN_WIDTH + KV_WIDTH, ATTN_WIDTH + 2 * KV_WIDTH, \
        ATTN_WIDTH + 2 * KV_WIDTH + ATTN_WIDTH
    q = pbuf[:, o_q:o_k] * (HEAD_DIM ** -0.5)
    k = pbuf[:, o_k:o_v]
    v = pbuf[:, o_v:o_a]

    lane = lax.broadcasted_iota(jnp.int32, (WINDOW + TS, LANES), 1)
    low = lane < HEAD_DIM
    kfull = jnp.concatenate([kv_prev[:, 0:KV_WIDTH], k], axis=0)
    vfull = jnp.concatenate([kv_prev[:, KV_WIDTH:], v], axis=0)
    kroll = pltpu.roll(kfull, HEAD_DIM, axis=1)
    vroll = pltpu.roll(vfull, HEAD_DIM, axis=1)
    kk = [jnp.where(low, kfull, kroll).astype(jnp.bfloat16), jnp.where(low, kroll, kfull).astype(jnp.bfloat16)]
    vv = [jnp.where(low, vfull, vroll).astype(jnp.bfloat16), jnp.where(low, vroll, vfull).astype(jnp.bfloat16)]
    kv_prev[:, 0:KV_WIDTH] = k[TS - WINDOW:, :]
    kv_prev[:, KV_WIDTH:] = v[TS - WINDOW:, :]

    rows = Q_PER_KV * WINDOW
    qi = lax.broadcasted_iota(jnp.int32, (rows, 2 * WINDOW), 0) & (WINDOW - 1)
    kj = lax.broadcasted_iota(jnp.int32, (rows, 2 * WINDOW), 1)
    band = (kj > qi) & (kj <= qi + WINDOW)
    lane_q = lax.broadcasted_iota(jnp.int32, (WINDOW, LANES), 1)
    low_q = lane_q < HEAD_DIM
    ones_kv = jnp.ones((2 * WINDOW, LANES), jnp.bfloat16)

    def attn_unit(b, g):
        qb = q[b * WINDOW:(b + 1) * WINDOW, :]
        valid = band & ((kj >= WINDOW) | jnp.logical_not(first & (b == 0)))
        parts = []
        for i in range(Q_PER_KV):
            h = g * Q_PER_KV + i
            q128 = qb[:, (h // 2) * LANES:(h // 2 + 1) * LANES]
            keep = low_q if h % 2 == 0 else jnp.logical_not(low_q)
            parts.append(jnp.where(keep, q128, 0.0))
        qs = jnp.concatenate(parts, axis=0).astype(jnp.bfloat16)
        kb = kk[g][b * WINDOW:b * WINDOW + 2 * WINDOW, :]
        vb = vv[g][b * WINDOW:b * WINDOW + 2 * WINDOW, :]
        sc = _dot_nt(qs, kb)
        sc = jnp.where(valid, sc, NEG_BIG)
        sink = sink_ref[g]
        m = jnp.maximum(jnp.max(sc, axis=-1, keepdims=True), sink)
        p = jnp.exp(sc - m).astype(jnp.bfloat16)
        den = _dot(p, ones_kv) + jnp.exp(sink - m)
        pv = _dot(p, vb)
        o = pv / den
        return [o[i * WINDOW:(i + 1) * WINDOW, :] for i in range(Q_PER_KV)]

    cw = u_buf.shape[1]
    u_buf[CONV_HALO:CONV_HALO + TS, :] = pbuf[:, o_a:o_g] * jax.nn.sigmoid(pbuf[:, o_g:])
    n_sh = u_sh.shape[1]
    for s in range(1, SUBLANES):
        u_sh[s - 1] = u_buf[s:s + n_sh, :]
    shift = CONV_HALO - (CONV_KERNEL - 1)

    def conv_strip(c, r0):
        cs = slice(c * LANES, (c + 1) * LANES)
        acc = jnp.zeros((CONV_ROWS, LANES), jnp.float32)
        for j in range(CONV_KERNEL):
            base, s = (shift + j) // SUBLANES * SUBLANES, (shift + j) % SUBLANES
            rs = slice(r0 + base, r0 + base + CONV_ROWS)
            win = u_buf[rs, cs] if s == 0 else u_sh[s - 1, rs, cs]
            acc = acc + wdw_ref[j:j + 1, cs] * win
        conv_buf[r0:r0 + CONV_ROWS, cs] = acc

    units = [(b, g) for b in range(TS // WINDOW) for g in range(N_KV_HEADS)]
    strips = [(c, r0) for c in range(cw // LANES) for r0 in range(0, TS, CONV_ROWS)]
    head_out = {}
    for n, (b, g) in enumerate(units):
        yield "body"
        head_out[b, g] = attn_unit(b, g)
    for c, r0 in strips:
        yield "body"
        conv_strip(c, r0)

    yield "body"
    attn_blocks = []
    for b in range(TS // WINDOW):
        heads = [o for g in range(N_KV_HEADS) for o in head_out[b, g]]
        cols = [jnp.where(low_q, heads[2 * j], heads[2 * j + 1]) for j in range(N_Q_HEADS // 2)]
        attn_blocks.append(jnp.concatenate(cols, axis=1))
    attn = jnp.concatenate(attn_blocks, axis=0)

    u_buf[0:CONV_HALO, :] = u_buf[TS:TS + CONV_HALO, :]
    cv = conv_buf[...] + bdw_ref[...]
    mu = jnp.mean(cv, axis=-1, keepdims=True)
    xc = cv - mu
    cv = xc * lax.rsqrt(jnp.mean(xc * xc, axis=-1, keepdims=True) + LN_EPS) * lng_ref[...] + lnb_ref[...]
    cv = cv * jax.nn.sigmoid(cv)

    mbuf[...] = jnp.concatenate([_rms(attn, ga_ref[...]), _rms(cv, gc_ref[...])], axis=1).astype(jnp.bfloat16)

    for c in range(wout_ref.shape[1] // MIX_COLS):
        yield "tail"
        cs = slice(c * MIX_COLS, (c + 1) * MIX_COLS)
        o_ref[0, :, cs] = x_ref[0, :, cs] + _dot(mbuf[...], wout_ref[:, cs]) + bout_ref[:, cs]


def _mix(x, g_mix, w_in, b_in, sink_cols, w_dw, b_dw, ln_g, ln_b, g_a, g_c, w_out, b_out):
    bsz, seq, d = x.shape
    assert seq % TS == 0 and TS % WINDOW == 0 and CONV_HALO >= CONV_KERNEL - 1 and bsz % MIX_SEQS == 0
    cw = w_dw.shape[1]
    const = lambda shape: pl.BlockSpec(shape, lambda b, s: (0,) * len(shape))
    x = x.reshape(MIX_SEQS, bsz // MIX_SEQS, seq, d)
    seq_scratch = [
        pltpu.VMEM((WINDOW, 2 * KV_WIDTH), jnp.float32),
        pltpu.VMEM((CONV_HALO + TS, cw), jnp.float32),
        pltpu.VMEM((SUBLANES - 1, CONV_HALO + TS - SUBLANES, cw), jnp.float32),
        pltpu.VMEM((TS, cw), jnp.float32),
        pltpu.VMEM((TS, w_in.shape[1]), jnp.float32),
        pltpu.VMEM((TS, w_out.shape[0]), jnp.bfloat16),
    ]
    return pl.pallas_call(
        _mix_kernel,
        grid=(bsz // MIX_SEQS, seq // TS),
        in_specs=[
            pl.BlockSpec((MIX_SEQS, 1, TS, d), lambda b, s: (0, b, s, 0)),
            const((1, d)), const(w_in.shape), const((1, w_in.shape[1])),
            const(sink_cols.shape), const(w_dw.shape), const((1, cw)), const((1, cw)), const((1, cw)),
            const((1, ATTN_WIDTH)), const((1, cw)), const(w_out.shape), const((1, d)),
        ],
        out_specs=pl.BlockSpec((MIX_SEQS, 1, TS, d), lambda b, s: (0, b, s, 0)),
        out_shape=jax.ShapeDtypeStruct(x.shape, jnp.float32),
        scratch_shapes=seq_scratch * MIX_SEQS,
        compiler_params=pltpu.CompilerParams(
            dimension_semantics=("arbitrary", "arbitrary"), vmem_limit_bytes=VMEM_LIMIT),
        name="mix",
    )(x, g_mix, w_in, b_in, sink_cols, w_dw, b_dw, ln_g, ln_b, g_a, g_c, w_out, b_out).reshape(bsz, seq, d)


def _route_kernel(x1_ref, gffn_ref, whi_ref, wlo_ref, brt_ref, xs_ref, slot_ref, gate_ref, pcc_ref):
    tiles = [_route_tile(i, x1_ref, gffn_ref, whi_ref, wlo_ref, brt_ref, xs_ref, slot_ref, gate_ref, pcc_ref)
             for i in range(ROUTE_TILES)]
    live = list(tiles)
    while live:
        live = [t for t in live if next(t, None) is not None]


def _route_tile(i, x1_ref, gffn_ref, whi_ref, wlo_ref, brt_ref, xs_ref, slot_ref, gate_ref, pcc_ref):
    cols = slice(i * TS, (i + 1) * TS)
    h = _rms(x1_ref[cols, :], gffn_ref[...])
    hb = h.astype(jnp.bfloat16)
    h_lo = (h - hb.astype(jnp.float32)).astype(jnp.bfloat16)
    yield "split"
    lg_t = _dot(hb, whi_ref[...]) + (_dot(hb, wlo_ref[...]) + _dot(h_lo, whi_ref[...]))
    lg = lg_t.T[0:N_EXPERTS, :] + brt_ref[...]
    yield "logits"
    iota_e = lax.broadcasted_iota(jnp.int32, (N_EXPERTS, TS), 0)
    vals, hots = [], []
    member = jnp.zeros((N_EXPERTS, TS), jnp.float32)
    for _ in range(TOP_K):
        mx = jnp.max(lg, axis=0, keepdims=True)
        idx = jnp.min(jnp.where(lg == mx, iota_e, N_EXPERTS), axis=0, keepdims=True)
        hot = iota_e == idx
        lg = jnp.where(hot, -jnp.inf, lg)
        member = member + hot.astype(jnp.float32)
        vals.append(mx)
        hots.append(hot)
    ex = [jnp.exp(vk - vals[0]) for vk in vals]
    den = ex[0] + ex[1] + ex[2] + ex[3]
    gate_ref[:, cols] = jnp.concatenate([e / den for e in ex], axis=0)
    yield "top4"

    ti = lax.broadcasted_iota(jnp.int32, (TS, TS), 0)
    tj = lax.broadcasted_iota(jnp.int32, (TS, TS), 1)
    upper = jnp.where(ti < tj, 1.0, 0.0).astype(jnp.bfloat16)
    cum = _dot(member.astype(jnp.bfloat16), upper)
    cnt = jnp.sum(member, axis=1, keepdims=True)
    pcc = jnp.floor((cnt + (CHUNK - 1)) * (1.0 / CHUNK))
    pcb = jnp.broadcast_to(pcc, (N_EXPERTS, LANES))
    row_e = lax.broadcasted_iota(jnp.int32, (N_EXPERTS, LANES), 0)
    inc = pcb
    sh = 1
    while sh < N_EXPERTS:
        inc = inc + jnp.where(row_e >= sh, pltpu.roll(inc, sh, axis=0), 0.0)
        sh *= 2
    run_start = (inc - pcb)[:, 0:1] * CHUNK
    pcc_ref[i] = pcb.astype(jnp.int32)

    pos = run_start + cum
    slots = [jnp.sum(jnp.where(hot, pos, 0.0), axis=0, keepdims=True).astype(jnp.int32) for hot in hots]
    slot_ref[:, cols] = jnp.concatenate(slots, axis=0)
    yield "slots"

    iota_r = lax.broadcasted_iota(jnp.int32, (R_CAP, TS), 0)
    sel = (iota_r == slots[0]) | (iota_r == slots[1]) | (iota_r == slots[2]) | (iota_r == slots[3])
    perm = jnp.where(sel, 1.0, 0.0).astype(jnp.bfloat16)
    xs = _dot(perm, hb)
    yield "permute"
    xs_ref[i] = _pack_halves(xs)


def _route(x1, g_ffn, w_router, b_router):
    t, d = x1.shape
    nt = t // TS
    assert nt % ROUTE_TILES == 0
    w_pad = jnp.pad(w_router.astype(jnp.float32), ((0, 0), (0, LANES - N_EXPERTS)))
    w_hi = w_pad.astype(jnp.bfloat16)
    w_lo = (w_pad - w_hi.astype(jnp.float32)).astype(jnp.bfloat16)
    br_t = b_router.astype(jnp.float32)[:, None]
    const = lambda shape: pl.BlockSpec(shape, lambda i: (0,) * len(shape))
    return pl.pallas_call(
        _route_kernel,
        grid=(nt // ROUTE_TILES,),
        in_specs=[pl.BlockSpec((ROUTE_TILES * TS, d), lambda i: (i, 0)), const((1, d)), const(w_hi.shape),
                  const(w_lo.shape), const(br_t.shape)],
        out_specs=[
            pl.BlockSpec((ROUTE_TILES, R_CAP, d // 2), lambda i: (i, 0, 0)),
            pl.BlockSpec((TOP_K, ROUTE_TILES * TS), lambda i: (0, i)),
            pl.BlockSpec((TOP_K, ROUTE_TILES * TS), lambda i: (0, i)),
            pl.BlockSpec((ROUTE_TILES, N_EXPERTS, LANES), lambda i: (i, 0, 0)),
        ],
        out_shape=[
            jax.ShapeDtypeStruct((nt, R_CAP, d // 2), jnp.uint32),
            jax.ShapeDtypeStruct((TOP_K, t), jnp.int32),
            jax.ShapeDtypeStruct((TOP_K, t), jnp.float32),
            jax.ShapeDtypeStruct((nt, N_EXPERTS, LANES), jnp.int32),
        ],
        compiler_params=pltpu.CompilerParams(
            dimension_semantics=("arbitrary",), vmem_limit_bytes=VMEM_LIMIT),
        name="route",
    )(x1, g_ffn, w_hi, w_lo, br_t)


def _experts_kernel(src_ref, dst_ref, bexp_ref, act_ref, nvc_ref,
                    xs_hbm, w1_ref, b1g_ref, b1l_ref, w2_ref, b2_ref, dei_ref,
                    y_hbm,
                    xbuf, ybuf, zbuf, w1g, w1l, w2b, sem_in, sem_out, sem_z):
    b = pl.program_id(0)
    nb = pl.num_programs(0)
    n_tiles = nvc_ref.shape[0]
    n_real = n_tiles * CPT
    slot = b % 2

    def zero_tail(tile, start):
        n = CPT - nvc_ref[tile]
        off = tile * CPT + nvc_ref[tile]
        for pc in _TAIL_PIECES:
            take = (n & pc) != 0

            @pl.when(take)
            def _(off=off, pc=pc):
                cp = pltpu.make_async_copy(zbuf.at[pl.ds(0, pc)], y_hbm.at[pl.ds(off, pc)], sem_z)
                if start:
                    cp.start()
                else:
                    cp.wait()
            off = off + jnp.where(take, pc, 0)

    @pl.when(b == 0)
    def _():
        zbuf[...] = jnp.zeros_like(zbuf)
        for k in range(2 * CPB // TAIL_MAX):
            cp = pltpu.make_async_copy(zbuf, y_hbm.at[pl.ds(n_real + k * TAIL_MAX, TAIL_MAX)], sem_z)
            cp.start()
            cp.wait()

    @pl.when(b < n_tiles)
    def _():
        zero_tail(b, True)

    def active(blk):
        return act_ref[blk] > 0

    def start_all(blk, sl, inbound):
        for c in range(CPB):
            if inbound:
                pltpu.make_async_copy(xs_hbm.at[src_ref[blk * CPB + c]], xbuf.at[sl, c], sem_in.at[sl]).start()
            else:
                pltpu.make_async_copy(ybuf.at[sl, c], y_hbm.at[dst_ref[blk * CPB + c]], sem_out.at[sl]).start()

    def wait_all(sl, inbound):
        if inbound:
            pltpu.make_async_copy(xs_hbm.at[pl.ds(0, CPB)], xbuf.at[sl], sem_in.at[sl]).wait()
        else:
            pltpu.make_async_copy(ybuf.at[sl], y_hbm.at[pl.ds(0, CPB)], sem_out.at[sl]).wait()

    def compute():
        xb = _unpack_halves(xbuf[slot].reshape(TM, xbuf.shape[3]))
        hg = _dot(xb, w1g[...]) + b1g_ref[0]
        hl = _dot(xb, w1l[...]) + b1l_ref[0]
        hg = jnp.minimum(hg, SWIGLU_LIMIT)
        hl = jnp.clip(hl, -SWIGLU_LIMIT, SWIGLU_LIMIT)
        act = hg * jax.nn.sigmoid(SWIGLU_ALPHA * hg) * (hl + 1.0)
        y = _dot(act.astype(jnp.bfloat16), w2b[...]) + b2_ref[0]
        ybuf[slot] = _pack_halves(y).reshape(CPB, CHUNK, y.shape[1] // 2)

    @pl.when((b == 0) & active(0))
    def _():
        start_all(0, 0, True)

    prev_e = bexp_ref[jnp.maximum(b - 1, 0)]

    @pl.when((b == 0) | (bexp_ref[b] != prev_e))
    def _():
        dei = dei_ref[...]
        for j in range(w1_ref.shape[2] // (2 * LANES)):
            wj = w1_ref[0, :, j * 2 * LANES:(j + 1) * 2 * LANES].astype(jnp.bfloat16)
            r = _dot(wj, dei)
            w1g[:, j * LANES:(j + 1) * LANES] = r[:, 0:LANES].astype(jnp.bfloat16)
            w1l[:, j * LANES:(j + 1) * LANES] = r[:, LANES:].astype(jnp.bfloat16)
        w2b[...] = w2_ref[0].astype(jnp.bfloat16)

    @pl.when((b >= 2) & active(jnp.maximum(b - 2, 0)))
    def _():
        wait_all(slot, False)

    @pl.when(active(b))
    def _():
        wait_all(slot, True)

    prev_on = (b >= 1) & active(jnp.maximum(b - 1, 0))
    next_on = (b + 1 < nb) & active(jnp.minimum(b + 1, nb - 1))
    fast = prev_on & next_on & active(b)

    @pl.when(fast)
    def _():
        start_all(b - 1, 1 - slot, False)
        start_all(b + 1, 1 - slot, True)
        compute()

    @pl.when(jnp.logical_not(fast))
    def _():
        @pl.when(prev_on)
        def _():
            start_all(b - 1, 1 - slot, False)

        @pl.when(next_on)
        def _():
            start_all(b + 1, 1 - slot, True)

        @pl.when(active(b))
        def _():
            compute()

    @pl.when(b < n_tiles)
    def _():
        zero_tail(b, False)

    @pl.when(b == nb - 1)
    def _():
        @pl.when(active(b))
        def _():
            start_all(b, slot, False)

        @pl.when(prev_on)
        def _():
            wait_all(1 - slot, False)

        @pl.when(active(b))
        def _():
            wait_all(slot, False)


def _experts(chunk_src, chunk_dst, block_expert, block_active, nvc, xs_chunks, w1, b1g, b1l, w2, b2, dei):
    n_chunks, _, dh = xs_chunks.shape
    nb = block_expert.shape[0]
    assert nb >= nvc.shape[0] and (2 * CPB) % TAIL_MAX == 0
    d, de2 = w1.shape[1], w1.shape[2]
    grid_spec = pltpu.PrefetchScalarGridSpec(
        num_scalar_prefetch=5,
        grid=(nb,),
        in_specs=[
            pl.BlockSpec(memory_space=pl.ANY),
            pl.BlockSpec((1, d, de2), lambda i, src, dst, be, on, nc: (be[i], 0, 0)),
            pl.BlockSpec((1, 1, de2 // 2), lambda i, src, dst, be, on, nc: (be[i], 0, 0)),
            pl.BlockSpec((1, 1, de2 // 2), lambda i, src, dst, be, on, nc: (be[i], 0, 0)),
            pl.BlockSpec((1, de2 // 2, d), lambda i, src, dst, be, on, nc: (be[i], 0, 0)),
            pl.BlockSpec((1, 1, d), lambda i, src, dst, be, on, nc: (be[i], 0, 0)),
            pl.BlockSpec(dei.shape, lambda i, src, dst, be, on, nc: (0, 0)),
        ],
        out_specs=pl.BlockSpec(memory_space=pl.ANY),
        scratch_shapes=[
            pltpu.VMEM((2, CPB, CHUNK, dh), jnp.uint32),
            pltpu.VMEM((2, CPB, CHUNK, dh), jnp.uint32),
            pltpu.VMEM((TAIL_MAX, CHUNK, dh), jnp.uint32),
            pltpu.VMEM((d, de2 // 2), jnp.bfloat16),
            pltpu.VMEM((d, de2 // 2), jnp.bfloat16),
            pltpu.VMEM((de2 // 2, d), jnp.bfloat16),
            pltpu.SemaphoreType.DMA((2,)),
            pltpu.SemaphoreType.DMA((2,)),
            pltpu.SemaphoreType.DMA(()),
        ],
    )
    return pl.pallas_call(
        _experts_kernel,
        grid_spec=grid_spec,
        out_shape=jax.ShapeDtypeStruct((n_chunks + 2 * CPB, CHUNK, dh), jnp.uint32),
        compiler_params=pltpu.CompilerParams(
            dimension_semantics=("arbitrary",), vmem_limit_bytes=VMEM_LIMIT),
        name="experts",
    )(chunk_src, chunk_dst, block_expert, block_active, nvc, xs_chunks, w1, b1g, b1l, w2, b2, dei)


_PIECES = tuple(1 << i for i in reversed(range((CPT).bit_length())))


def _combine_kernel(nvc_ref,
                    y_hbm, x1_ref, slot_ref, gate_ref, gfin_ref, o_ref, ybuf, sem):
    i = pl.program_id(0)
    n_steps = pl.num_programs(0)
    sl = i % 2

    def fetch(step, s_, start):
        for j in range(COMBINE_TILES):
            tile = step * COMBINE_TILES + j
            n = nvc_ref[tile]
            off = jnp.int32(0)
            for pc in _PIECES:
                if pc * CHUNK > R_CAP:
                    continue
                take = (n & pc) != 0

                @pl.when(take)
                def _(off=off, pc=pc, tile=tile, j=j):
                    cp = pltpu.make_async_copy(y_hbm.at[pl.ds(tile * CPT + off, pc)],
                                               ybuf.at[s_, j, pl.ds(off, pc)], sem.at[s_])
                    if start:
                        cp.start()
                    else:
                        cp.wait()
                off = off + jnp.where(take, pc, 0)

    @pl.when(i == 0)
    def _():
        ybuf[...] = jnp.zeros_like(ybuf)
        fetch(0, 0, True)

    @pl.when(i + 1 < n_steps)
    def _():
        fetch(i + 1, 1 - sl, True)

    fetch(i, sl, False)

    def tile_program(j):
        cols = slice(j * TS, (j + 1) * TS)
        iota_r = lax.broadcasted_iota(jnp.int32, (R_CAP, TS), 0)
        gt = jnp.zeros((R_CAP, TS), jnp.float32)
        for k in range(TOP_K):
            gt = jnp.where(iota_r == slot_ref[k:k + 1, cols], gate_ref[k:k + 1, cols], gt)
        gt = gt.astype(jnp.bfloat16)
        yield "gates"
        moe = lax.dot_general(gt, _unpack_halves(ybuf[sl, j].reshape(R_CAP, ybuf.shape[4])),
                              (((0,), (0,)), ((), ())), preferred_element_type=jnp.float32)
        yield "unpermute"
        o_ref[cols, :] = _rms(x1_ref[cols, :] + moe, gfin_ref[...])

    progs = [tile_program(j) for j in range(COMBINE_TILES)]
    for rnd in range(COMBINE_TILES + 2):
        for j, pr in enumerate(progs):
            if 0 <= rnd - j <= 2:
                next(pr, None)


def _combine(nvc, y_chunks, x1, slot_t, gate_t, g_final):
    t, d = x1.shape
    nt = t // TS
    assert nt % COMBINE_TILES == 0
    rows = COMBINE_TILES * TS
    grid_spec = pltpu.PrefetchScalarGridSpec(
        num_scalar_prefetch=1,
        grid=(nt // COMBINE_TILES,),
        in_specs=[
            pl.BlockSpec(memory_space=pl.ANY),
            pl.BlockSpec((rows, d), lambda i, nv: (i, 0)),
            pl.BlockSpec((TOP_K, rows), lambda i, nv: (0, i)),
            pl.BlockSpec((TOP_K, rows), lambda i, nv: (0, i)),
            pl.BlockSpec((1, d), lambda i, nv: (0, 0)),
        ],
        out_specs=pl.BlockSpec((rows, d), lambda i, nv: (i, 0)),
        scratch_shapes=[pltpu.VMEM((2, COMBINE_TILES, CPT, CHUNK, d // 2), jnp.uint32),
                        pltpu.SemaphoreType.DMA((2,))],
    )
    return pl.pallas_call(
        _combine_kernel,
        grid_spec=grid_spec,
        out_shape=jax.ShapeDtypeStruct((t, d), jnp.float32),
        compiler_params=pltpu.CompilerParams(
            dimension_semantics=("arbitrary",), vmem_limit_bytes=VMEM_LIMIT),
        name="combine",
    )(nvc, y_chunks, x1, slot_t, gate_t, g_final)


def _chunk_plan(pcc, nb):
    nt = pcc.shape[0]
    i32 = jnp.int32
    pcc_t = pcc.T
    run_start_t = (jnp.cumsum(pcc, axis=1) - pcc).T
    cum_incl = jnp.cumsum(pcc_t, axis=1)
    eblocks = (cum_incl[:, -1] + CPB - 1) // CPB
    bstart = jnp.cumsum(eblocks) - eblocks
    blk = jnp.arange(nb, dtype=i32)
    bexp = jnp.sum(bstart[None, :] <= blk[:, None], axis=1).astype(i32) - 1
    oh = bexp[:, None] == jnp.arange(N_EXPERTS, dtype=i32)[None, :]
    pick = lambda tab: jnp.sum(jnp.where(oh[:, :, None], tab[None], 0), axis=1)
    ci, pc_row, rs_row = pick(cum_incl), pick(pcc_t), pick(run_start_t)
    b0 = jnp.sum(jnp.where(oh, bstart[None, :], 0), axis=1)
    p = (blk - b0)[:, None] * CPB + jnp.arange(CPB, dtype=i32)[None, :]
    before = ci[:, None, :] <= p[:, :, None]
    tile = jnp.sum(before, axis=2).astype(i32)
    cum_excl = jnp.sum(jnp.where(before, pc_row[:, None, :], 0), axis=2)
    at_tile = jnp.arange(nt, dtype=i32)[None, None, :] == tile[:, :, None]
    rs = jnp.sum(jnp.where(at_tile, rs_row[:, None, :], 0), axis=2)
    valid = tile < nt
    chunk = tile * CPT + rs + (p - cum_excl)
    trash = nt * CPT + (blk % 2)[:, None] * CPB + jnp.arange(CPB, dtype=i32)[None, :]
    src = jnp.where(valid, chunk, 0).astype(i32).reshape(-1)
    dst = jnp.where(valid, chunk, trash).astype(i32).reshape(-1)
    active = jnp.any(valid, axis=1).astype(i32)
    return src, dst, bexp, active


def kernel(x, g_mix, w_in, b_in, sinks, w_dw, b_dw, ln_g, ln_b, g_attn_out, g_conv_out, w_out, b_out,
           g_ffn, w_router, b_router, w1, b1, w2, b2, g_final):
    bsz, seq, d = x.shape
    t = bsz * seq
    nt = t // TS
    depth = g_mix.shape[0]
    assert depth == 1, "combine fuses the final RMSNorm, so exactly one layer is supported"
    f32 = jnp.float32
    max_chunks = nt * ((TS * TOP_K + N_EXPERTS * (CHUNK - 1)) // CHUNK) + N_EXPERTS * (CPB - 1)
    nb = -(-max_chunks // CPB)
    ci = jnp.arange(2 * LANES)
    dei = (ci[:, None] == jnp.where(ci < LANES, 2 * ci, 2 * (ci - LANES) + 1)[None, :]).astype(jnp.bfloat16)

    for l in range(depth):
        sink_cols = jnp.repeat(sinks[l].astype(f32).reshape(N_KV_HEADS, Q_PER_KV), WINDOW, axis=1)[..., None]
        x1 = _mix(x, g_mix[l][None], w_in[l].astype(jnp.bfloat16), b_in[l][None], sink_cols,
                  w_dw[l], b_dw[l][None], ln_g[l][None], ln_b[l][None],
                  g_attn_out[l][None], g_conv_out[l][None], w_out[l].astype(jnp.bfloat16), b_out[l][None])
        x1 = x1.reshape(t, d)
        xs, slot_t, gate_t, pcc = _route(x1, g_ffn[l][None], w_router[l], b_router[l])
        pcc = pcc[:, :, 0]
        src, dst, bexp, active = _chunk_plan(pcc, nb)
        b1l = b1[l].reshape(N_EXPERTS, 1, -1, 2)
        nvc = jnp.sum(pcc, axis=1).astype(jnp.int32)
        y = _experts(src, dst, bexp, active, nvc, xs.reshape(nt * CPT, CHUNK, d // 2), w1[l], b1l[..., 0], b1l[..., 1],
                     w2[l], b2[l][:, None, :], dei)
        x = _combine(nvc, y, x1, slot_t, gate_t, g_final[None]).reshape(bsz, seq, d)
    return x
```

```python
import jax
import jax.numpy as jnp
from jax import lax
from jax.experimental import pallas as pl
from jax.experimental.pallas import tpu as pltpu

HEAD_DIM = 64
N_Q_HEADS = 8
N_KV_HEADS = 2
Q_PER_KV = N_Q_HEADS // N_KV_HEADS
ATTN_WIDTH = N_Q_HEADS * HEAD_DIM
KV_WIDTH = N_KV_HEADS * HEAD_DIM
WINDOW = 128
CONV_KERNEL = 31
N_EXPERTS = 32
TOP_K = 4
SWIGLU_LIMIT = 7.0
SWIGLU_ALPHA = 1.702
RMS_EPS = 1e-5
LN_EPS = 1e-5

LANES = 128
SUBLANES = 8

TS = 256
CHUNK = SUBLANES
R_CAP = -(-(TS * TOP_K + N_EXPERTS * (CHUNK - 1)) // LANES) * LANES
CPT = R_CAP // CHUNK
TAIL_MAX = CPT - TS * TOP_K // CHUNK
_TAIL_PIECES = tuple(1 << i for i in reversed(range(TAIL_MAX.bit_length())))
MIX_SEQS = 2
MIX_COLS = 256
MIX_FILL_EVERY = 2
ROUTE_TILES = 4
COMBINE_TILES = 2
TM = 512
CPB = TM // CHUNK
CONV_HALO = 32
CONV_ROWS = 64
NEG_BIG = -1e30
VMEM_LIMIT = 56 * 1024 * 1024


def _rms(x, g):
    return x * lax.rsqrt(jnp.mean(x * x, axis=-1, keepdims=True) + RMS_EPS) * g


def _dot(a, b):
    return jnp.dot(a, b, preferred_element_type=jnp.float32)


_HI16 = 0xFFFF0000


def _pack_halves(v):
    half = v.shape[1] // 2
    as_bits = lambda t: lax.bitcast_convert_type(t.astype(jnp.bfloat16).astype(jnp.float32), jnp.uint32)
    return (as_bits(v[:, half:]) & jnp.uint32(_HI16)) | (as_bits(v[:, :half]) >> 16)


def _unpack_halves(w):
    lo = lax.bitcast_convert_type(w << 16, jnp.float32).astype(jnp.bfloat16)
    hi = lax.bitcast_convert_type(w & jnp.uint32(_HI16), jnp.float32).astype(jnp.bfloat16)
    return jnp.concatenate([lo, hi], axis=1)


def _dot_nt(a, b, precision=None):
    return lax.dot_general(a, b, (((1,), (1,)), ((), ())), precision=precision,
                           preferred_element_type=jnp.float32)


def _mix_kernel(x_ref, gmix_ref, win_ref, bin_ref, sink_ref, wdw_ref, bdw_ref, lng_ref, lnb_ref,
                ga_ref, gc_ref, wout_ref, bout_ref, o_ref, *scratch):
    params = (gmix_ref, win_ref, bin_ref, sink_ref, wdw_ref, bdw_ref, lng_ref, lnb_ref,
              ga_ref, gc_ref, wout_ref, bout_ref)
    per_seq = len(scratch) // MIX_SEQS
    seqs = [scratch[i * per_seq:(i + 1) * per_seq] for i in range(MIX_SEQS)]
    first = pl.program_id(1) == 0

    @pl.when(first)
    def _():
        for kv_prev, u_buf, *_ in seqs:
            kv_prev[...] = jnp.zeros_like(kv_prev)
            u_buf[0:CONV_HALO, :] = jnp.zeros((CONV_HALO, u_buf.shape[1]), jnp.float32)

    _emit_pipelined([_Stream(_mix_tile(x_ref.at[i], o_ref.at[i], first, params, *seqs[i]))
                     for i in range(MIX_SEQS)])


class _Stream:
    def __init__(self, gen):
        self.gen = gen
        self.tag = next(gen, None)

    def step(self):
        self.tag = next(self.gen, None)

    def run_while(self, tag):
        while self.tag == tag:
            self.step()


def _emit_pipelined(streams):
    streams[0].run_while("head")
    for i, cur in enumerate(streams):
        prv = streams[i - 1] if i >= 1 else None
        nxt = streams[i + 1] if i + 1 < len(streams) else None
        n = 0
        while cur.tag == "body":
            cur.step()
            n += 1
            if n % MIX_FILL_EVERY == 0:
                if prv is not None and prv.tag == "tail":
                    prv.step()
                elif nxt is not None and nxt.tag == "head":
                    nxt.step()
        if prv is not None:
            prv.run_while("tail")
        if nxt is not None:
            nxt.run_while("head")
    streams[-1].run_while("tail")


def _mix_tile(x_ref, o_ref, first, params, kv_prev, u_buf, u_sh, conv_buf, pbuf, mbuf):
    gmix_ref, win_ref, bin_ref, sink_ref, wdw_ref, bdw_ref, lng_ref, lnb_ref, ga_ref, gc_ref, wout_ref, bout_ref = params
    ub = None
    for c in range(win_ref.shape[1] // MIX_COLS):
        yield "head"
        if ub is None:
            ub = _rms(x_ref[0], gmix_ref[...]).astype(jnp.bfloat16)
        cs = slice(c * MIX_COLS, (c + 1) * MIX_COLS)
        pbuf[:, cs] = _dot(ub, win_ref[:, cs]) + bin_ref[:, cs]

    yield "body"
    o_q, o_k, o_v, o_a, o_g = 0, ATTN_WIDTH, ATTN_WIDTH + KV_WIDTH, ATTN_WIDTH + 2 * KV_WIDTH, \
        ATTN_WIDTH + 2 * KV_WIDTH + ATTN_WIDTH
    q = pbuf[:, o_q:o_k] * (HEAD_DIM ** -0.5)
    k = pbuf[:, o_k:o_v]
    v = pbuf[:, o_v:o_a]

    lane = lax.broadcasted_iota(jnp.int32, (WINDOW + TS, LANES), 1)
    low = lane < HEAD_DIM
    kfull = jnp.concatenate([kv_prev[:, 0:KV_WIDTH], k], axis=0)
    vfull = jnp.concatenate([kv_prev[:, KV_WIDTH:], v], axis=0)
    kroll = pltpu.roll(kfull, HEAD_DIM, axis=1)
    vroll = pltpu.roll(vfull, HEAD_DIM, axis=1)
    kk = [jnp.where(low, kfull, kroll).astype(jnp.bfloat16), jnp.where(low, kroll, kfull).astype(jnp.bfloat16)]
    vv = [jnp.where(low, vfull, vroll).astype(jnp.bfloat16), jnp.where(low, vroll, vfull).astype(jnp.bfloat16)]
    kv_prev[:, 0:KV_WIDTH] = k[TS - WINDOW:, :]
    kv_prev[:, KV_WIDTH:] = v[TS - WINDOW:, :]

    rows = Q_PER_KV * WINDOW
    qi = lax.broadcasted_iota(jnp.int32, (rows, 2 * WINDOW), 0) & (WINDOW - 1)
    kj = lax.broadcasted_iota(jnp.int32, (rows, 2 * WINDOW), 1)
    band = (kj > qi) & (kj <= qi + WINDOW)
    lane_q = lax.broadcasted_iota(jnp.int32, (WINDOW, LANES), 1)
    low_q = lane_q < HEAD_DIM
    ones_kv = jnp.ones((2 * WINDOW, LANES), jnp.bfloat16)

    def attn_unit(b, g):
        qb = q[b * WINDOW:(b + 1) * WINDOW, :]
        valid = band & ((kj >= WINDOW) | jnp.logical_not(first & (b == 0)))
        parts = []
        for i in range(Q_PER_KV):
            h = g * Q_PER_KV + i
            q128 = qb[:, (h // 2) * LANES:(h // 2 + 1) * LANES]
            keep = low_q if h % 2 == 0 else jnp.logical_not(low_q)
            parts.append(jnp.where(keep, q128, 0.0))
        qs = jnp.concatenate(parts, axis=0).astype(jnp.bfloat16)
        kb = kk[g][b * WINDOW:b * WINDOW + 2 * WINDOW, :]
        vb = vv[g][b * WINDOW:b * WINDOW + 2 * WINDOW, :]
        sc = _dot_nt(qs, kb)
        sc = jnp.where(valid, sc, NEG_BIG)
        sink = sink_ref[g]
        m = jnp.maximum(jnp.max(sc, axis=-1, keepdims=True), sink)
        p = jnp.exp(sc - m).astype(jnp.bfloat16)
        den = _dot(p, ones_kv) + jnp.exp(sink - m)
        pv = _dot(p, vb)
        o = pv / den
        return [o[i * WINDOW:(i + 1) * WINDOW, :] for i in range(Q_PER_KV)]

    cw = u_buf.shape[1]
    u_buf[CONV_HALO:CONV_HALO + TS, :] = pbuf[:, o_a:o_g] * jax.nn.sigmoid(pbuf[:, o_g:])
    n_sh = u_sh.shape[1]
    for s in range(1, SUBLANES):
        u_sh[s - 1] = u_buf[s:s + n_sh, :]
    shift = CONV_HALO - (CONV_KERNEL - 1)

    def conv_strip(c, r0):
        cs = slice(c * LANES, (c + 1) * LANES)
        acc = jnp.zeros((CONV_ROWS, LANES), jnp.float32)
        for j in range(CONV_KERNEL):
            base, s = (shift + j) // SUBLANES * SUBLANES, (shift + j) % SUBLANES
            rs = slice(r0 + base, r0 + base + CONV_ROWS)
            win = u_buf[rs, cs] if s == 0 else u_sh[s - 1, rs, cs]
            acc = acc + wdw_ref[j:j + 1, cs] * win
        conv_buf[r0:r0 + CONV_ROWS, cs] = acc

    units = [(b, g) for b in range(TS // WINDOW) for g in range(N_KV_HEADS)]
    strips = [(c, r0) for c in range(cw // LANES) for r0 in range(0, TS, CONV_ROWS)]
    head_out = {}
    for n, (b, g) in enumerate(units):
        yield "body"
        head_out[b, g] = attn_unit(b, g)
    for c, r0 in strips:
        yield "body"
        conv_strip(c, r0)

    yield "body"
    attn_blocks = []
    for b in range(TS // WINDOW):
        heads = [o for g in range(N_KV_HEADS) for o in head_out[b, g]]
        cols = [jnp.where(low_q, heads[2 * j], heads[2 * j + 1]) for j in range(N_Q_HEADS // 2)]
        attn_blocks.append(jnp.concatenate(cols, axis=1))
    attn = jnp.concatenate(attn_blocks, axis=0)

    u_buf[0:CONV_HALO, :] = u_buf[TS:TS + CONV_HALO, :]
    cv = conv_buf[...] + bdw_ref[...]
    mu = jnp.mean(cv, axis=-1, keepdims=True)
    xc = cv - mu
    cv = xc * lax.rsqrt(jnp.mean(xc * xc, axis=-1, keepdims=True) + LN_EPS) * lng_ref[...] + lnb_ref[...]
    cv = cv * jax.nn.sigmoid(cv)

    mbuf[...] = jnp.concatenate([_rms(attn, ga_ref[...]), _rms(cv, gc_ref[...])], axis=1).astype(jnp.bfloat16)

    for c in range(wout_ref.shape[1] // MIX_COLS):
        yield "tail"
        cs = slice(c * MIX_COLS, (c + 1) * MIX_COLS)
        o_ref[0, :, cs] = x_ref[0, :, cs] + _dot(mbuf[...], wout_ref[:, cs]) + bout_ref[:, cs]


def _mix(x, g_mix, w_in, b_in, sink_cols, w_dw, b_dw, ln_g, ln_b, g_a, g_c, w_out, b_out):
    bsz, seq, d = x.shape
    assert seq % TS == 0 and TS % WINDOW == 0 and CONV_HALO >= CONV_KERNEL - 1 and bsz % MIX_SEQS == 0
    cw = w_dw.shape[1]
    const = lambda shape: pl.BlockSpec(shape, lambda b, s: (0,) * len(shape))
    x = x.reshape(MIX_SEQS, bsz // MIX_SEQS, seq, d)
    seq_scratch = [
        pltpu.VMEM((WINDOW, 2 * KV_WIDTH), jnp.float32),
        pltpu.VMEM((CONV_HALO + TS, cw), jnp.float32),
        pltpu.VMEM((SUBLANES - 1, CONV_HALO + TS - SUBLANES, cw), jnp.float32),
        pltpu.VMEM((TS, cw), jnp.float32),
        pltpu.VMEM((TS, w_in.shape[1]), jnp.float32),
        pltpu.VMEM((TS, w_out.shape[0]), jnp.bfloat16),
    ]
    return pl.pallas_call(
        _mix_kernel,
        grid=(bsz // MIX_SEQS, seq // TS),
        in_specs=[
            pl.BlockSpec((MIX_SEQS, 1, TS, d), lambda b, s: (0, b, s, 0)),
            const((1, d)), const(w_in.shape), const((1, w_in.shape[1])),
            const(sink_cols.shape), const(w_dw.shape), const((1, cw)), const((1, cw)), const((1, cw)),
            const((1, ATTN_WIDTH)), const((1, cw)), const(w_out.shape), const((1, d)),
        ],
        out_specs=pl.BlockSpec((MIX_SEQS, 1, TS, d), lambda b, s: (0, b, s, 0)),
        out_shape=jax.ShapeDtypeStruct(x.shape, jnp.float32),
        scratch_shapes=seq_scratch * MIX_SEQS,
        compiler_params=pltpu.CompilerParams(
            dimension_semantics=("arbitrary", "arbitrary"), vmem_limit_bytes=VMEM_LIMIT),
        name="mix",
    )(x, g_mix, w_in, b_in, sink_cols, w_dw, b_dw, ln_g, ln_b, g_a, g_c, w_out, b_out).reshape(bsz, seq, d)


def _route_kernel(x1_ref, gffn_ref, whi_ref, wlo_ref, brt_ref, xs_ref, slot_ref, gate_ref, pcc_ref):
    tiles = [_route_tile(i, x1_ref, gffn_ref, whi_ref, wlo_ref, brt_ref, xs_ref, slot_ref, gate_ref, pcc_ref)
             for i in range(ROUTE_TILES)]
    live = list(tiles)
    while live:
        live = [t for t in live if next(t, None) is not None]


def _route_tile(i, x1_ref, gffn_ref, whi_ref, wlo_ref, brt_ref, xs_ref, slot_ref, gate_ref, pcc_ref):
    cols = slice(i * TS, (i + 1) * TS)
    h = _rms(x1_ref[cols, :], gffn_ref[...])
    hb = h.astype(jnp.bfloat16)
    h_lo = (h - hb.astype(jnp.float32)).astype(jnp.bfloat16)
    yield "split"
    lg_t = _dot(hb, whi_ref[...]) + (_dot(hb, wlo_ref[...]) + _dot(h_lo, whi_ref[...]))
    lg = lg_t.T[0:N_EXPERTS, :] + brt_ref[...]
    yield "logits"
    iota_e = lax.broadcasted_iota(jnp.int32, (N_EXPERTS, TS), 0)
    vals, hots = [], []
    member = jnp.zeros((N_EXPERTS, TS), jnp.float32)
    for _ in range(TOP_K):
        mx = jnp.max(lg, axis=0, keepdims=True)
        idx = jnp.min(jnp.where(lg == mx, iota_e, N_EXPERTS), axis=0, keepdims=True)
        hot = iota_e == idx
        lg = jnp.where(hot, -jnp.inf, lg)
        member = member + hot.astype(jnp.float32)
        vals.append(mx)
        hots.append(hot)
    ex = [jnp.exp(vk - vals[0]) for vk in vals]
    den = ex[0] + ex[1] + ex[2] + ex[3]
    gate_ref[:, cols] = jnp.concatenate([e / den for e in ex], axis=0)
    yield "top4"

    ti = lax.broadcasted_iota(jnp.int32, (TS, TS), 0)
    tj = lax.broadcasted_iota(jnp.int32, (TS, TS), 1)
    upper = jnp.where(ti < tj, 1.0, 0.0).astype(jnp.bfloat16)
    cum = _dot(member.astype(jnp.bfloat16), upper)
    cnt = jnp.sum(member, axis=1, keepdims=True)
    pcc = jnp.floor((cnt + (CHUNK - 1)) * (1.0 / CHUNK))
    pcb = jnp.broadcast_to(pcc, (N_EXPERTS, LANES))
    row_e = lax.broadcasted_iota(jnp.int32, (N_EXPERTS, LANES), 0)
    inc = pcb
    sh = 1
    while sh < N_EXPERTS:
        inc = inc + jnp.where(row_e >= sh, pltpu.roll(inc, sh, axis=0), 0.0)
        sh *= 2
    run_start = (inc - pcb)[:, 0:1] * CHUNK
    pcc_ref[i] = pcb.astype(jnp.int32)

    pos = run_start + cum
    slots = [jnp.sum(jnp.where(hot, pos, 0.0), axis=0, keepdims=True).astype(jnp.int32) for hot in hots]
    slot_ref[:, cols] = jnp.concatenate(slots, axis=0)
    yield "slots"

    iota_r = lax.broadcasted_iota(jnp.int32, (R_CAP, TS), 0)
    sel = (iota_r == slots[0]) | (iota_r == slots[1]) | (iota_r == slots[2]) | (iota_r == slots[3])
    perm = jnp.where(sel, 1.0, 0.0).astype(jnp.bfloat16)
    xs = _dot(perm, hb)
    yield "permute"
    xs_ref[i] = _pack_halves(xs)


def _route(x1, g_ffn, w_router, b_router):
    t, d = x1.shape
    nt = t // TS
    assert nt % ROUTE_TILES == 0
    w_pad = jnp.pad(w_router.astype(jnp.float32), ((0, 0), (0, LANES - N_EXPERTS)))
    w_hi = w_pad.astype(jnp.bfloat16)
    w_lo = (w_pad - w_hi.astype(jnp.float32)).astype(jnp.bfloat16)
    br_t = b_router.astype(jnp.float32)[:, None]
    const = lambda shape: pl.BlockSpec(shape, lambda i: (0,) * len(shape))
    return pl.pallas_call(
        _route_kernel,
        grid=(nt // ROUTE_TILES,),
        in_specs=[pl.BlockSpec((ROUTE_TILES * TS, d), lambda i: (i, 0)), const((1, d)), const(w_hi.shape),
                  const(w_lo.shape), const(br_t.shape)],
        out_specs=[
            pl.BlockSpec((ROUTE_TILES, R_CAP, d // 2), lambda i: (i, 0, 0)),
            pl.BlockSpec((TOP_K, ROUTE_TILES * TS), lambda i: (0, i)),
            pl.BlockSpec((TOP_K, ROUTE_TILES * TS), lambda i: (0, i)),
            pl.BlockSpec((ROUTE_TILES, N_EXPERTS, LANES), lambda i: (i, 0, 0)),
        ],
        out_shape=[
            jax.ShapeDtypeStruct((nt, R_CAP, d // 2), jnp.uint32),
            jax.ShapeDtypeStruct((TOP_K, t), jnp.int32),
            jax.ShapeDtypeStruct((TOP_K, t), jnp.float32),
            jax.ShapeDtypeStruct((nt, N_EXPERTS, LANES), jnp.int32),
        ],
        compiler_params=pltpu.CompilerParams(
            dimension_semantics=("arbitrary",), vmem_limit_bytes=VMEM_LIMIT),
        name="route",
    )(x1, g_ffn, w_hi, w_lo, br_t)


def _experts_kernel(src_ref, dst_ref, bexp_ref, act_ref, nvc_ref,
                    xs_hbm, w1_ref, b1g_ref, b1l_ref, w2_ref, b2_ref, dei_ref,
                    y_hbm,
                    xbuf, ybuf, zbuf, w1g, w1l, w2b, sem_in, sem_out, sem_z):
    b = pl.program_id(0)
    nb = pl.num_programs(0)
    n_tiles = nvc_ref.shape[0]
    n_real = n_tiles * CPT
    slot = b % 2

    def zero_tail(tile, start):
        n = CPT - nvc_ref[tile]
        off = tile * CPT + nvc_ref[tile]
        for pc in _TAIL_PIECES:
            take = (n & pc) != 0

            @pl.when(take)
            def _(off=off, pc=pc):
                cp = pltpu.make_async_copy(zbuf.at[pl.ds(0, pc)], y_hbm.at[pl.ds(off, pc)], sem_z)
                if start:
                    cp.start()
                else:
                    cp.wait()
            off = off + jnp.where(take, pc, 0)

    @pl.when(b == 0)
    def _():
        zbuf[...] = jnp.zeros_like(zbuf)
        for k in range(2 * CPB // TAIL_MAX):
            cp = pltpu.make_async_copy(zbuf, y_hbm.at[pl.ds(n_real + k * TAIL_MAX, TAIL_MAX)], sem_z)
            cp.start()
            cp.wait()

    @pl.when(b < n_tiles)
    def _():
        zero_tail(b, True)

    def active(blk):
        return act_ref[blk] > 0

    def start_all(blk, sl, inbound):
        for c in range(CPB):
            if inbound:
                pltpu.make_async_copy(xs_hbm.at[src_ref[blk * CPB + c]], xbuf.at[sl, c], sem_in.at[sl]).start()
            else:
                pltpu.make_async_copy(ybuf.at[sl, c], y_hbm.at[dst_ref[blk * CPB + c]], sem_out.at[sl]).start()

    def wait_all(sl, inbound):
        if inbound:
            pltpu.make_async_copy(xs_hbm.at[pl.ds(0, CPB)], xbuf.at[sl], sem_in.at[sl]).wait()
        else:
            pltpu.make_async_copy(ybuf.at[sl], y_hbm.at[pl.ds(0, CPB)], sem_out.at[sl]).wait()

    def compute():
        xb = _unpack_halves(xbuf[slot].reshape(TM, xbuf.shape[3]))
        hg = _dot(xb, w1g[...]) + b1g_ref[0]
        hl = _dot(xb, w1l[...]) + b1l_ref[0]
        hg = jnp.minimum(hg, SWIGLU_LIMIT)
        hl = jnp.clip(hl, -SWIGLU_LIMIT, SWIGLU_LIMIT)
        act = hg * jax.nn.sigmoid(SWIGLU_ALPHA * hg) * (hl + 1.0)
        y = _dot(act.astype(jnp.bfloat16), w2b[...]) + b2_ref[0]
        ybuf[slot] = _pack_halves(y).reshape(CPB, CHUNK, y.shape[1] // 2)

    @pl.when((b == 0) & active(0))
    def _():
        start_all(0, 0, True)

    prev_e = bexp_ref[jnp.maximum(b - 1, 0)]

    @pl.when((b == 0) | (bexp_ref[b] != prev_e))
    def _():
        dei = dei_ref[...]
        for j in range(w1_ref.shape[2] // (2 * LANES)):
            wj = w1_ref[0, :, j * 2 * LANES:(j + 1) * 2 * LANES].astype(jnp.bfloat16)
            r = _dot(wj, dei)
            w1g[:, j * LANES:(j + 1) * LANES] = r[:, 0:LANES].astype(jnp.bfloat16)
            w1l[:, j * LANES:(j + 1) * LANES] = r[:, LANES:].astype(jnp.bfloat16)
        w2b[...] = w2_ref[0].astype(jnp.bfloat16)

    @pl.when((b >= 2) & active(jnp.maximum(b - 2, 0)))
    def _():
        wait_all(slot, False)

    @pl.when(active(b))
    def _():
        wait_all(slot, True)

    prev_on = (b >= 1) & active(jnp.maximum(b - 1, 0))
    next_on = (b + 1 < nb) & active(jnp.minimum(b + 1, nb - 1))
    fast = prev_on & next_on & active(b)

    @pl.when(fast)
    def _():
        start_all(b - 1, 1 - slot, False)
        start_all(b + 1, 1 - slot, True)
        compute()

    @pl.when(jnp.logical_not(fast))
    def _():
        @pl.when(prev_on)
        def _():
            start_all(b - 1, 1 - slot, False)

        @pl.when(next_on)
        def _():
            start_all(b + 1, 1 - slot, True)

        @pl.when(active(b))
        def _():
            compute()

    @pl.when(b < n_tiles)
    def _():
        zero_tail(b, False)

    @pl.when(b == nb - 1)
    def _():
        @pl.when(active(b))
        def _():
            start_all(b, slot, False)

        @pl.when(prev_on)
        def _():
            wait_all(1 - slot, False)

        @pl.when(active(b))
        def _():
            wait_all(slot, False)


def _experts(chunk_src, chunk_dst, block_expert, block_active, nvc, xs_chunks, w1, b1g, b1l, w2, b2, dei):
    n_chunks, _, dh = xs_chunks.shape
    nb = block_expert.shape[0]
    assert nb >= nvc.shape[0] and (2 * CPB) % TAIL_MAX == 0
    d, de2 = w1.shape[1], w1.shape[2]
    grid_spec = pltpu.PrefetchScalarGridSpec(
        num_scalar_prefetch=5,
        grid=(nb,),
        in_specs=[
            pl.BlockSpec(memory_space=pl.ANY),
            pl.BlockSpec((1, d, de2), lambda i, src, dst, be, on, nc: (be[i], 0, 0)),
            pl.BlockSpec((1, 1, de2 // 2), lambda i, src, dst, be, on, nc: (be[i], 0, 0)),
            pl.BlockSpec((1, 1, de2 // 2), lambda i, src, dst, be, on, nc: (be[i], 0, 0)),
            pl.BlockSpec((1, de2 // 2, d), lambda i, src, dst, be, on, nc: (be[i], 0, 0)),
            pl.BlockSpec((1, 1, d), lambda i, src, dst, be, on, nc: (be[i], 0, 0)),
            pl.BlockSpec(dei.shape, lambda i, src, dst, be, on, nc: (0, 0)),
        ],
        out_specs=pl.BlockSpec(memory_space=pl.ANY),
        scratch_shapes=[
            pltpu.VMEM((2, CPB, CHUNK, dh), jnp.uint32),
            pltpu.VMEM((2, CPB, CHUNK, dh), jnp.uint32),
            pltpu.VMEM((TAIL_MAX, CHUNK, dh), jnp.uint32),
            pltpu.VMEM((d, de2 // 2), jnp.bfloat16),
            pltpu.VMEM((d, de2 // 2), jnp.bfloat16),
            pltpu.VMEM((de2 // 2, d), jnp.bfloat16),
            pltpu.SemaphoreType.DMA((2,)),
            pltpu.SemaphoreType.DMA((2,)),
            pltpu.SemaphoreType.DMA(()),
        ],
    )
    return pl.pallas_call(
        _experts_kernel,
        grid_spec=grid_spec,
        out_shape=jax.ShapeDtypeStruct((n_chunks + 2 * CPB, CHUNK, dh), jnp.uint32),
        compiler_params=pltpu.CompilerParams(
            dimension_semantics=("arbitrary",), vmem_limit_bytes=VMEM_LIMIT),
        name="experts",
    )(chunk_src, chunk_dst, block_expert, block_active, nvc, xs_chunks, w1, b1g, b1l, w2, b2, dei)


_PIECES = tuple(1 << i for i in reversed(range((CPT).bit_length())))


def _combine_kernel(nvc_ref,
                    y_hbm, x1_ref, slot_ref, gate_ref, gfin_ref, o_ref, ybuf, sem):
    i = pl.program_id(0)
    n_steps = pl.num_programs(0)
    sl = i % 2

    def fetch(step, s_, start):
        for j in range(COMBINE_TILES):
            tile = step * COMBINE_TILES + j
            n = nvc_ref[tile]
            off = jnp.int32(0)
            for pc in _PIECES:
                if pc * CHUNK > R_CAP:
                    continue
                take = (n & pc) != 0

                @pl.when(take)
                def _(off=off, pc=pc, tile=tile, j=j):
                    cp = pltpu.make_async_copy(y_hbm.at[pl.ds(tile * CPT + off, pc)],
                                               ybuf.at[s_, j, pl.ds(off, pc)], sem.at[s_])
                    if start:
                        cp.start()
                    else:
                        cp.wait()
                off = off + jnp.where(take, pc, 0)

    @pl.when(i == 0)
    def _():
        ybuf[...] = jnp.zeros_like(ybuf)
        fetch(0, 0, True)

    @pl.when(i + 1 < n_steps)
    def _():
        fetch(i + 1, 1 - sl, True)

    fetch(i, sl, False)

    def tile_program(j):
        cols = slice(j * TS, (j + 1) * TS)
        iota_r = lax.broadcasted_iota(jnp.int32, (R_CAP, TS), 0)
        gt = jnp.zeros((R_CAP, TS), jnp.float32)
        for k in range(TOP_K):
            gt = jnp.where(iota_r == slot_ref[k:k + 1, cols], gate_ref[k:k + 1, cols], gt)
        gt = gt.astype(jnp.bfloat16)
        yield "gates"
        moe = lax.dot_general(gt, _unpack_halves(ybuf[sl, j].reshape(R_CAP, ybuf.shape[4])),
                              (((0,), (0,)), ((), ())), preferred_element_type=jnp.float32)
        yield "unpermute"
        o_ref[cols, :] = _rms(x1_ref[cols, :] + moe, gfin_ref[...])

    progs = [tile_program(j) for j in range(COMBINE_TILES)]
    for rnd in range(COMBINE_TILES + 2):
        for j, pr in enumerate(progs):
            if 0 <= rnd - j <= 2:
                next(pr, None)


def _combine(nvc, y_chunks, x1, slot_t, gate_t, g_final):
    t, d = x1.shape
    nt = t // TS
    assert nt % COMBINE_TILES == 0
    rows = COMBINE_TILES * TS
    grid_spec = pltpu.PrefetchScalarGridSpec(
        num_scalar_prefetch=1,
        grid=(nt // COMBINE_TILES,),
        in_specs=[
            pl.BlockSpec(memory_space=pl.ANY),
            pl.BlockSpec((rows, d), lambda i, nv: (i, 0)),
            pl.BlockSpec((TOP_K, rows), lambda i, nv: (0, i)),
            pl.BlockSpec((TOP_K, rows), lambda i, nv: (0, i)),
            pl.BlockSpec((1, d), lambda i, nv: (0, 0)),
        ],
        out_specs=pl.BlockSpec((rows, d), lambda i, nv: (i, 0)),
        scratch_shapes=[pltpu.VMEM((2, COMBINE_TILES, CPT, CHUNK, d // 2), jnp.uint32),
                        pltpu.SemaphoreType.DMA((2,))],
    )
    return pl.pallas_call(
        _combine_kernel,
        grid_spec=grid_spec,
        out_shape=jax.ShapeDtypeStruct((t, d), jnp.float32),
        compiler_params=pltpu.CompilerParams(
            dimension_semantics=("arbitrary",), vmem_limit_bytes=VMEM_LIMIT),
        name="combine",
    )(nvc, y_chunks, x1, slot_t, gate_t, g_final)


def _chunk_plan(pcc, nb):
    nt = pcc.shape[0]
    i32 = jnp.int32
    pcc_t = pcc.T
    run_start_t = (jnp.cumsum(pcc, axis=1) - pcc).T
    cum_incl = jnp.cumsum(pcc_t, axis=1)
    eblocks = (cum_incl[:, -1] + CPB - 1) // CPB
    bstart = jnp.cumsum(eblocks) - eblocks
    blk = jnp.arange(nb, dtype=i32)
    bexp = jnp.sum(bstart[None, :] <= blk[:, None], axis=1).astype(i32) - 1
    oh = bexp[:, None] == jnp.arange(N_EXPERTS, dtype=i32)[None, :]
    pick = lambda tab: jnp.sum(jnp.where(oh[:, :, None], tab[None], 0), axis=1)
    ci, pc_row, rs_row = pick(cum_incl), pick(pcc_t), pick(run_start_t)
    b0 = jnp.sum(jnp.where(oh, bstart[None, :], 0), axis=1)
    p = (blk - b0)[:, None] * CPB + jnp.arange(CPB, dtype=i32)[None, :]
    before = ci[:, None, :] <= p[:, :, None]
    tile = jnp.sum(before, axis=2).astype(i32)
    cum_excl = jnp.sum(jnp.where(before, pc_row[:, None, :], 0), axis=2)
    at_tile = jnp.arange(nt, dtype=i32)[None, None, :] == tile[:, :, None]
    rs = jnp.sum(jnp.where(at_tile, rs_row[:, None, :], 0), axis=2)
    valid = tile < nt
    chunk = tile * CPT + rs + (p - cum_excl)
    trash = nt * CPT + (blk % 2)[:, None] * CPB + jnp.arange(CPB, dtype=i32)[None, :]
    src = jnp.where(valid, chunk, 0).astype(i32).reshape(-1)
    dst = jnp.where(valid, chunk, trash).astype(i32).reshape(-1)
    active = jnp.any(valid, axis=1).astype(i32)
    return src, dst, bexp, active


def kernel(x, g_mix, w_in, b_in, sinks, w_dw, b_dw, ln_g, ln_b, g_attn_out, g_conv_out, w_out, b_out,
           g_ffn, w_router, b_router, w1, b1, w2, b2, g_final):
    bsz, seq, d = x.shape
    t = bsz * seq
    nt = t // TS
    depth = g_mix.shape[0]
    assert depth == 1, "combine fuses the final RMSNorm, so exactly one layer is supported"
    f32 = jnp.float32
    max_chunks = nt * ((TS * TOP_K + N_EXPERTS * (CHUNK - 1)) // CHUNK) + N_EXPERTS * (CPB - 1)
    nb = -(-max_chunks // CPB)
    ci = jnp.arange(2 * LANES)
    dei = (ci[:, None] == jnp.where(ci < LANES, 2 * ci, 2 * (ci - LANES) + 1)[None, :]).astype(jnp.bfloat16)

    for l in range(depth):
        sink_cols = jnp.repeat(sinks[l].astype(f32).reshape(N_KV_HEADS, Q_PER_KV), WINDOW, axis=1)[..., None]
        x1 = _mix(x, g_mix[l][None], w_in[l].astype(jnp.bfloat16), b_in[l][None], sink_cols,
                  w_dw[l], b_dw[l][None], ln_g[l][None], ln_b[l][None],
                  g_attn_out[l][None], g_conv_out[l][None], w_out[l].astype(jnp.bfloat16), b_out[l][None])
        x1 = x1.reshape(t, d)
        xs, slot_t, gate_t, pcc = _route(x1, g_ffn[l][None], w_router[l], b_router[l])
        pcc = pcc[:, :, 0]
        src, dst, bexp, active = _chunk_plan(pcc, nb)
        b1l = b1[l].reshape(N_EXPERTS, 1, -1, 2)
        nvc = jnp.sum(pcc, axis=1).astype(jnp.int32)
        y = _experts(src, dst, bexp, active, nvc, xs.reshape(nt * CPT, CHUNK, d // 2), w1[l], b1l[..., 0], b1l[..., 1],
                     w2[l], b2[l][:, None, :], dei)
        x = _combine(nvc, y, x1, slot_t, gate_t, g_final[None]).reshape(bsz, seq, d)
    return x
```

```python
import jax
import jax.numpy as jnp
from jax import lax
from jax.experimental import pallas as pl
from jax.experimental.pallas import tpu as pltpu

HEAD_DIM = 64
N_Q_HEADS = 8
N_KV_HEADS = 2
Q_PER_KV = N_Q_HEADS // N_KV_HEADS
ATTN_WIDTH = N_Q_HEADS * HEAD_DIM
KV_WIDTH = N_KV_HEADS * HEAD_DIM
WINDOW = 128
CONV_KERNEL = 31
N_EXPERTS = 32
TOP_K = 4
SWIGLU_LIMIT = 7.0
SWIGLU_ALPHA = 1.702
RMS_EPS = 1e-5
LN_EPS = 1e-5

LANES = 128
SUBLANES = 8

TS = 256
CHUNK = SUBLANES
R_CAP = -(-(TS * TOP_K + N_EXPERTS * (CHUNK - 1)) // LANES) * LANES
CPT = R_CAP // CHUNK
TAIL_MAX = CPT - TS * TOP_K // CHUNK
_TAIL_PIECES = tuple(1 << i for i in reversed(range(TAIL_MAX.bit_length())))
MIX_SEQS = 2
MIX_COLS = 256
MIX_FILL_EVERY = 2
ROUTE_TILES = 4
COMBINE_TILES = 2
TM = 512
CPB = TM // CHUNK
CONV_HALO = 32
CONV_ROWS = 64
NEG_BIG = -1e30
VMEM_LIMIT = 56 * 1024 * 1024


def _rms(x, g):
    return x * lax.rsqrt(jnp.mean(x * x, axis=-1, keepdims=True) + RMS_EPS) * g


def _dot(a, b):
    return jnp.dot(a, b, preferred_element_type=jnp.float32)


_HI16 = 0xFFFF0000


def _pack_halves(v):
    half = v.shape[1] // 2
    as_bits = lambda t: lax.bitcast_convert_type(t.astype(jnp.bfloat16).astype(jnp.float32), jnp.uint32)
    return (as_bits(v[:, half:]) & jnp.uint32(_HI16)) | (as_bits(v[:, :half]) >> 16)


def _unpack_halves(w):
    lo = lax.bitcast_convert_type(w << 16, jnp.float32).astype(jnp.bfloat16)
    hi = lax.bitcast_convert_type(w & jnp.uint32(_HI16), jnp.float32).astype(jnp.bfloat16)
    return jnp.concatenate([lo, hi], axis=1)


def _dot_nt(a, b, precision=None):
    return lax.dot_general(a, b, (((1,), (1,)), ((), ())), precision=precision,
                           preferred_element_type=jnp.float32)


def _mix_kernel(x_ref, gmix_ref, win_ref, bin_ref, sink_ref, wdw_ref, bdw_ref, lng_ref, lnb_ref,
                ga_ref, gc_ref, wout_ref, bout_ref, o_ref, *scratch):
    params = (gmix_ref, win_ref, bin_ref, sink_ref, wdw_ref, bdw_ref, lng_ref, lnb_ref,
              ga_ref, gc_ref, wout_ref, bout_ref)
    per_seq = len(scratch) // MIX_SEQS
    seqs = [scratch[i * per_seq:(i + 1) * per_seq] for i in range(MIX_SEQS)]
    first = pl.program_id(1) == 0

    @pl.when(first)
    def _():
        for kv_prev, u_buf, *_ in seqs:
            kv_prev[...] = jnp.zeros_like(kv_prev)
            u_buf[0:CONV_HALO, :] = jnp.zeros((CONV_HALO, u_buf.shape[1]), jnp.float32)

    _emit_pipelined([_Stream(_mix_tile(x_ref.at[i], o_ref.at[i], first, params, *seqs[i]))
                     for i in range(MIX_SEQS)])


class _Stream:
    def __init__(self, gen):
        self.gen = gen
        self.tag = next(gen, None)

    def step(self):
        self.tag = next(self.gen, None)

    def run_while(self, tag):
        while self.tag == tag:
            self.step()


def _emit_pipelined(streams):
    streams[0].run_while("head")
    for i, cur in enumerate(streams):
        prv = streams[i - 1] if i >= 1 else None
        nxt = streams[i + 1] if i + 1 < len(streams) else None
        n = 0
        while cur.tag == "body":
            cur.step()
            n += 1
            if n % MIX_FILL_EVERY == 0:
                if prv is not None and prv.tag == "tail":
                    prv.step()
                elif nxt is not None and nxt.tag == "head":
                    nxt.step()
        if prv is not None:
            prv.run_while("tail")
        if nxt is not None:
            nxt.run_while("head")
    streams[-1].run_while("tail")


def _mix_tile(x_ref, o_ref, first, params, kv_prev, u_buf, u_sh, conv_buf, pbuf, mbuf):
    gmix_ref, win_ref, bin_ref, sink_ref, wdw_ref, bdw_ref, lng_ref, lnb_ref, ga_ref, gc_ref, wout_ref, bout_ref = params
    ub = None
    for c in range(win_ref.shape[1] // MIX_COLS):
        yield "head"
        if ub is None:
            ub = _rms(x_ref[0], gmix_ref[...]).astype(jnp.bfloat16)
        cs = slice(c * MIX_COLS, (c + 1) * MIX_COLS)
        pbuf[:, cs] = _dot(ub, win_ref[:, cs]) + bin_ref[:, cs]

    yield "body"
    o_q, o_k, o_v, o_a, o_g = 0, ATTN_WIDTH, ATTN_WIDTH + KV_WIDTH, ATTN_WIDTH + 2 * KV_WIDTH, \
        ATTN_WIDTH + 2 * KV_WIDTH + ATTN_WIDTH
    q = pbuf[:, o_q:o_k] * (HEAD_DIM ** -0.5)
    k = pbuf[:, o_k:o_v]
    v = pbuf[:, o_v:o_a]

    lane = lax.broadcasted_iota(jnp.int32, (WINDOW + TS, LANES), 1)
    low = lane < HEAD_DIM
    kfull = jnp.concatenate([kv_prev[:, 0:KV_WIDTH], k], axis=0)
    vfull = jnp.concatenate([kv_prev[:, KV_WIDTH:], v], axis=0)
    kroll = pltpu.roll(kfull, HEAD_DIM, axis=1)
    vroll = pltpu.roll(vfull, HEAD_DIM, axis=1)
    kk = [jnp.where(low, kfull, kroll).astype(jnp.bfloat16), jnp.where(low, kroll, kfull).astype(jnp.bfloat16)]
    vv = [jnp.where(low, vfull, vroll).astype(jnp.bfloat16), jnp.where(low, vroll, vfull).astype(jnp.bfloat16)]
    kv_prev[:, 0:KV_WIDTH] = k[TS - WINDOW:, :]
    kv_prev[:, KV_WIDTH:] = v[TS - WINDOW:, :]

    rows = Q_PER_KV * WINDOW
    qi = lax.broadcasted_iota(jnp.int32, (rows, 2 * WINDOW), 0) & (WINDOW - 1)
    kj = lax.broadcasted_iota(jnp.int32, (rows, 2 * WINDOW), 1)
    band = (kj > qi) & (kj <= qi + WINDOW)
    lane_q = lax.broadcasted_iota(jnp.int32, (WINDOW, LANES), 1)
    low_q = lane_q < HEAD_DIM
    ones_kv = jnp.ones((2 * WINDOW, LANES), jnp.bfloat16)

    def attn_scores(b, g):
        qb = q[b * WINDOW:(b + 1) * WINDOW, :]
        valid = band & ((kj >= WINDOW) | jnp.logical_not(first & (b == 0)))
        parts = []
        for i in range(Q_PER_KV):
            h = g * Q_PER_KV + i
            q128 = qb[:, (h // 2) * LANES:(h // 2 + 1) * LANES]
            keep = low_q if h % 2 == 0 else jnp.logical_not(low_q)
            parts.append(jnp.where(keep, q128, 0.0))
        qs = jnp.concatenate(parts, axis=0).astype(jnp.bfloat16)
        kb = kk[g][b * WINDOW:b * WINDOW + 2 * WINDOW, :]
        return jnp.where(valid, _dot_nt(qs, kb), NEG_BIG)

    def attn_probs(g, sc):
        sink = sink_ref[g]
        m = jnp.maximum(jnp.max(sc, axis=-1, keepdims=True), sink)
        return jnp.exp(sc - m).astype(jnp.bfloat16), jnp.exp(sink - m)

    def attn_out(b, g, p, sink_term):
        vb = vv[g][b * WINDOW:b * WINDOW + 2 * WINDOW, :]
        den = _dot(p, ones_kv) + sink_term
        pv = _dot(p, vb)
        o = pv / den
        return [o[i * WINDOW:(i + 1) * WINDOW, :] for i in range(Q_PER_KV)]

    cw = u_buf.shape[1]
    u_buf[CONV_HALO:CONV_HALO + TS, :] = pbuf[:, o_a:o_g] * jax.nn.sigmoid(pbuf[:, o_g:])
    n_sh = u_sh.shape[1]
    for s in range(1, SUBLANES):
        u_sh[s - 1] = u_buf[s:s + n_sh, :]
    shift = CONV_HALO - (CONV_KERNEL - 1)

    def conv_strip(c, r0):
        cs = slice(c * LANES, (c + 1) * LANES)
        acc = jnp.zeros((CONV_ROWS, LANES), jnp.float32)
        for j in range(CONV_KERNEL):
            base, s = (shift + j) // SUBLANES * SUBLANES, (shift + j) % SUBLANES
            rs = slice(r0 + base, r0 + base + CONV_ROWS)
            win = u_buf[rs, cs] if s == 0 else u_sh[s - 1, rs, cs]
            acc = acc + wdw_ref[j:j + 1, cs] * win
        conv_buf[r0:r0 + CONV_ROWS, cs] = acc

    units = [(b, g) for b in range(TS // WINDOW) for g in range(N_KV_HEADS)]
    strips = [(c, r0) for c in range(cw // LANES) for r0 in range(0, TS, CONV_ROWS)]
    head_out = {}
    yield "body"
    scores = {u: attn_scores(*u) for u in units}
    yield "body"
    probs = {u: attn_probs(u[1], scores[u]) for u in units}
    yield "body"
    for u in units:
        head_out[u] = attn_out(*u, *probs[u])
    for c, r0 in strips:
        yield "body"
        conv_strip(c, r0)

    yield "body"
    attn_blocks = []
    for b in range(TS // WINDOW):
        heads = [o for g in range(N_KV_HEADS) for o in head_out[b, g]]
        cols = [jnp.where(low_q, heads[2 * j], heads[2 * j + 1]) for j in range(N_Q_HEADS // 2)]
        attn_blocks.append(jnp.concatenate(cols, axis=1))
    attn = jnp.concatenate(attn_blocks, axis=0)

    u_buf[0:CONV_HALO, :] = u_buf[TS:TS + CONV_HALO, :]
    cv = conv_buf[...] + bdw_ref[...]
    mu = jnp.mean(cv, axis=-1, keepdims=True)
    xc = cv - mu
    cv = xc * lax.rsqrt(jnp.mean(xc * xc, axis=-1, keepdims=True) + LN_EPS) * lng_ref[...] + lnb_ref[...]
    cv = cv * jax.nn.sigmoid(cv)

    mbuf[...] = jnp.concatenate([_rms(attn, ga_ref[...]), _rms(cv, gc_ref[...])], axis=1).astype(jnp.bfloat16)

    for c in range(wout_ref.shape[1] // MIX_COLS):
        yield "tail"
        cs = slice(c * MIX_COLS, (c + 1) * MIX_COLS)
        o_ref[0, :, cs] = x_ref[0, :, cs] + _dot(mbuf[...], wout_ref[:, cs]) + bout_ref[:, cs]


def _mix(x, g_mix, w_in, b_in, sink_cols, w_dw, b_dw, ln_g, ln_b, g_a, g_c, w_out, b_out):
    bsz, seq, d = x.shape
    assert seq % TS == 0 and TS % WINDOW == 0 and CONV_HALO >= CONV_KERNEL - 1 and bsz % MIX_SEQS == 0
    cw = w_dw.shape[1]
    const = lambda shape: pl.BlockSpec(shape, lambda b, s: (0,) * len(shape))
    x = x.reshape(MIX_SEQS, bsz // MIX_SEQS, seq, d)
    seq_scratch = [
        pltpu.VMEM((WINDOW, 2 * KV_WIDTH), jnp.float32),
        pltpu.VMEM((CONV_HALO + TS, cw), jnp.float32),
        pltpu.VMEM((SUBLANES - 1, CONV_HALO + TS - SUBLANES, cw), jnp.float32),
        pltpu.VMEM((TS, cw), jnp.float32),
        pltpu.VMEM((TS, w_in.shape[1]), jnp.float32),
        pltpu.VMEM((TS, w_out.shape[0]), jnp.bfloat16),
    ]
    return pl.pallas_call(
        _mix_kernel,
        grid=(bsz // MIX_SEQS, seq // TS),
        in_specs=[
            pl.BlockSpec((MIX_SEQS, 1, TS, d), lambda b, s: (0, b, s, 0)),
            const((1, d)), const(w_in.shape), const((1, w_in.shape[1])),
            const(sink_cols.shape), const(w_dw.shape), const((1, cw)), const((1, cw)), const((1, cw)),
            const((1, ATTN_WIDTH)), const((1, cw)), const(w_out.shape), const((1, d)),
        ],
        out_specs=pl.BlockSpec((MIX_SEQS, 1, TS, d), lambda b, s: (0, b, s, 0)),
        out_shape=jax.ShapeDtypeStruct(x.shape, jnp.float32),
        scratch_shapes=seq_scratch * MIX_SEQS,
        compiler_params=pltpu.CompilerParams(
            dimension_semantics=("arbitrary", "arbitrary"), vmem_limit_bytes=VMEM_LIMIT),
        name="mix",
    )(x, g_mix, w_in, b_in, sink_cols, w_dw, b_dw, ln_g, ln_b, g_a, g_c, w_out, b_out).reshape(bsz, seq, d)


def _route_kernel(x1_ref, gffn_ref, whi_ref, wlo_ref, brt_ref, xs_ref, slot_ref, gate_ref, pcc_ref):
    tiles = [_route_tile(i, x1_ref, gffn_ref, whi_ref, wlo_ref, brt_ref, xs_ref, slot_ref, gate_ref, pcc_ref)
             for i in range(ROUTE_TILES)]
    live = list(tiles)
    while live:
        live = [t for t in live if next(t, None) is not None]


def _route_tile(i, x1_ref, gffn_ref, whi_ref, wlo_ref, brt_ref, xs_ref, slot_ref, gate_ref, pcc_ref):
    cols = slice(i * TS, (i + 1) * TS)
    h = _rms(x1_ref[cols, :], gffn_ref[...])
    hb = h.astype(jnp.bfloat16)
    h_lo = (h - hb.astype(jnp.float32)).astype(jnp.bfloat16)
    yield "split"
    lg_t = _dot(hb, whi_ref[...]) + (_dot(hb, wlo_ref[...]) + _dot(h_lo, whi_ref[...]))
    lg = lg_t.T[0:N_EXPERTS, :] + brt_ref[...]
    yield "logits"
    iota_e = lax.broadcasted_iota(jnp.int32, (N_EXPERTS, TS), 0)
    vals, hots = [], []
    member = jnp.zeros((N_EXPERTS, TS), jnp.float32)
    for _ in range(TOP_K):
        mx = jnp.max(lg, axis=0, keepdims=True)
        idx = jnp.min(jnp.where(lg == mx, iota_e, N_EXPERTS), axis=0, keepdims=True)
        hot = iota_e == idx
        lg = jnp.where(hot, -jnp.inf, lg)
        member = member + hot.astype(jnp.float32)
        vals.append(mx)
        hots.append(hot)
    ex = [jnp.exp(vk - vals[0]) for vk in vals]
    den = ex[0] + ex[1] + ex[2] + ex[3]
    gate_ref[:, cols] = jnp.concatenate([e / den for e in ex], axis=0)
    yield "top4"

    ti = lax.broadcasted_iota(jnp.int32, (TS, TS), 0)
    tj = lax.broadcasted_iota(jnp.int32, (TS, TS), 1)
    upper = jnp.where(ti < tj, 1.0, 0.0).astype(jnp.bfloat16)
    cum = _dot(member.astype(jnp.bfloat16), upper)
    cnt = jnp.sum(member, axis=1, keepdims=True)
    pcc = jnp.floor((cnt + (CHUNK - 1)) * (1.0 / CHUNK))
    pcb = jnp.broadcast_to(pcc, (N_EXPERTS, LANES))
    row_e = lax.broadcasted_iota(jnp.int32, (N_EXPERTS, LANES), 0)
    inc = pcb
    sh = 1
    while sh < N_EXPERTS:
        inc = inc + jnp.where(row_e >= sh, pltpu.roll(inc, sh, axis=0), 0.0)
        sh *= 2
    run_start = (inc - pcb)[:, 0:1] * CHUNK
    pcc_ref[i] = pcb.astype(jnp.int32)

    pos = run_start + cum
    slots = [jnp.sum(jnp.where(hot, pos, 0.0), axis=0, keepdims=True).astype(jnp.int32) for hot in hots]
    slot_ref[:, cols] = jnp.concatenate(slots, axis=0)
    yield "slots"

    iota_r = lax.broadcasted_iota(jnp.int32, (R_CAP, TS), 0)
    sel = (iota_r == slots[0]) | (iota_r == slots[1]) | (iota_r == slots[2]) | (iota_r == slots[3])
    perm = jnp.where(sel, 1.0, 0.0).astype(jnp.bfloat16)
    xs = _dot(perm, hb)
    yield "permute"
    xs_ref[i] = _pack_halves(xs)


def _route(x1, g_ffn, w_router, b_router):
    t, d = x1.shape
    nt = t // TS
    assert nt % ROUTE_TILES == 0
    w_pad = jnp.pad(w_router.astype(jnp.float32), ((0, 0), (0, LANES - N_EXPERTS)))
    w_hi = w_pad.astype(jnp.bfloat16)
    w_lo = (w_pad - w_hi.astype(jnp.float32)).astype(jnp.bfloat16)
    br_t = b_router.astype(jnp.float32)[:, None]
    const = lambda shape: pl.BlockSpec(shape, lambda i: (0,) * len(shape))
    return pl.pallas_call(
        _route_kernel,
        grid=(nt // ROUTE_TILES,),
        in_specs=[pl.BlockSpec((ROUTE_TILES * TS, d), lambda i: (i, 0)), const((1, d)), const(w_hi.shape),
                  const(w_lo.shape), const(br_t.shape)],
        out_specs=[
            pl.BlockSpec((ROUTE_TILES, R_CAP, d // 2), lambda i: (i, 0, 0)),
            pl.BlockSpec((TOP_K, ROUTE_TILES * TS), lambda i: (0, i)),
            pl.BlockSpec((TOP_K, ROUTE_TILES * TS), lambda i: (0, i)),
            pl.BlockSpec((ROUTE_TILES, N_EXPERTS, LANES), lambda i: (i, 0, 0)),
        ],
        out_shape=[
            jax.ShapeDtypeStruct((nt, R_CAP, d // 2), jnp.uint32),
            jax.ShapeDtypeStruct((TOP_K, t), jnp.int32),
            jax.ShapeDtypeStruct((TOP_K, t), jnp.float32),
            jax.ShapeDtypeStruct((nt, N_EXPERTS, LANES), jnp.int32),
        ],
        compiler_params=pltpu.CompilerParams(
            dimension_semantics=("arbitrary",), vmem_limit_bytes=VMEM_LIMIT),
        name="route",
    )(x1, g_ffn, w_hi, w_lo, br_t)


def _experts_kernel(src_ref, dst_ref, bexp_ref, act_ref, nvc_ref,
                    xs_hbm, w1_ref, b1g_ref, b1l_ref, w2_ref, b2_ref, dei_ref,
                    y_hbm,
                    xbuf, ybuf, zbuf, w1g, w1l, w2b, sem_in, sem_out, sem_z):
    b = pl.program_id(0)
    nb = pl.num_programs(0)
    n_tiles = nvc_ref.shape[0]
    n_real = n_tiles * CPT
    slot = b % 2

    def zero_tail(tile, start):
        n = CPT - nvc_ref[tile]
        off = tile * CPT + nvc_ref[tile]
        for pc in _TAIL_PIECES:
            take = (n & pc) != 0

            @pl.when(take)
            def _(off=off, pc=pc):
                cp = pltpu.make_async_copy(zbuf.at[pl.ds(0, pc)], y_hbm.at[pl.ds(off, pc)], sem_z)
                if start:
                    cp.start()
                else:
                    cp.wait()
            off = off + jnp.where(take, pc, 0)

    @pl.when(b == 0)
    def _():
        zbuf[...] = jnp.zeros_like(zbuf)
        for k in range(2 * CPB // TAIL_MAX):
            cp = pltpu.make_async_copy(zbuf, y_hbm.at[pl.ds(n_real + k * TAIL_MAX, TAIL_MAX)], sem_z)
            cp.start()
            cp.wait()

    @pl.when(b < n_tiles)
    def _():
        zero_tail(b, True)

    def active(blk):
        return act_ref[blk] > 0

    def start_all(blk, sl, inbound):
        for c in range(CPB):
            if inbound:
                pltpu.make_async_copy(xs_hbm.at[src_ref[blk * CPB + c]], xbuf.at[sl, c], sem_in.at[sl]).start()
            else:
                pltpu.make_async_copy(ybuf.at[sl, c], y_hbm.at[dst_ref[blk * CPB + c]], sem_out.at[sl]).start()

    def wait_all(sl, inbound):
        if inbound:
            pltpu.make_async_copy(xs_hbm.at[pl.ds(0, CPB)], xbuf.at[sl], sem_in.at[sl]).wait()
        else:
            pltpu.make_async_copy(ybuf.at[sl], y_hbm.at[pl.ds(0, CPB)], sem_out.at[sl]).wait()

    def compute():
        xb = _unpack_halves(xbuf[slot].reshape(TM, xbuf.shape[3]))
        hg = _dot(xb, w1g[...]) + b1g_ref[0]
        hl = _dot(xb, w1l[...]) + b1l_ref[0]
        hg = jnp.minimum(hg, SWIGLU_LIMIT)
        hl = jnp.clip(hl, -SWIGLU_LIMIT, SWIGLU_LIMIT)
        act = hg * jax.nn.sigmoid(SWIGLU_ALPHA * hg) * (hl + 1.0)
        y = _dot(act.astype(jnp.bfloat16), w2b[...]) + b2_ref[0]
        ybuf[slot] = _pack_halves(y).reshape(CPB, CHUNK, y.shape[1] // 2)

    @pl.when((b == 0) & active(0))
    def _():
        start_all(0, 0, True)

    prev_e = bexp_ref[jnp.maximum(b - 1, 0)]

    @pl.when((b == 0) | (bexp_ref[b] != prev_e))
    def _():
        dei = dei_ref[...]
        for j in range(w1_ref.shape[2] // (2 * LANES)):
            wj = w1_ref[0, :, j * 2 * LANES:(j + 1) * 2 * LANES].astype(jnp.bfloat16)
            r = _dot(wj, dei)
            w1g[:, j * LANES:(j + 1) * LANES] = r[:, 0:LANES].astype(jnp.bfloat16)
            w1l[:, j * LANES:(j + 1) * LANES] = r[:, LANES:].astype(jnp.bfloat16)
        w2b[...] = w2_ref[0].astype(jnp.bfloat16)

    @pl.when((b >= 2) & active(jnp.maximum(b - 2, 0)))
    def _():
        wait_all(slot, False)

    @pl.when(active(b))
    def _():
        wait_all(slot, True)

    prev_on = (b >= 1) & active(jnp.maximum(b - 1, 0))
    next_on = (b + 1 < nb) & active(jnp.minimum(b + 1, nb - 1))
    fast = prev_on & next_on & active(b)

    @pl.when(fast)
    def _():
        start_all(b - 1, 1 - slot, False)
        start_all(b + 1, 1 - slot, True)
        compute()

    @pl.when(jnp.logical_not(fast))
    def _():
        @pl.when(prev_on)
        def _():
            start_all(b - 1, 1 - slot, False)

        @pl.when(next_on)
        def _():
            start_all(b + 1, 1 - slot, True)

        @pl.when(active(b))
        def _():
            compute()

    @pl.when(b < n_tiles)
    def _():
        zero_tail(b, False)

    @pl.when(b == nb - 1)
    def _():
        @pl.when(active(b))
        def _():
            start_all(b, slot, False)

        @pl.when(prev_on)
        def _():
            wait_all(1 - slot, False)

        @pl.when(active(b))
        def _():
            wait_all(slot, False)


def _experts(chunk_src, chunk_dst, block_expert, block_active, nvc, xs_chunks, w1, b1g, b1l, w2, b2, dei):
    n_chunks, _, dh = xs_chunks.shape
    nb = block_expert.shape[0]
    assert nb >= nvc.shape[0] and (2 * CPB) % TAIL_MAX == 0
    d, de2 = w1.shape[1], w1.shape[2]
    grid_spec = pltpu.PrefetchScalarGridSpec(
        num_scalar_prefetch=5,
        grid=(nb,),
        in_specs=[
            pl.BlockSpec(memory_space=pl.ANY),
            pl.BlockSpec((1, d, de2), lambda i, src, dst, be, on, nc: (be[i], 0, 0)),
            pl.BlockSpec((1, 1, de2 // 2), lambda i, src, dst, be, on, nc: (be[i], 0, 0)),
            pl.BlockSpec((1, 1, de2 // 2), lambda i, src, dst, be, on, nc: (be[i], 0, 0)),
            pl.BlockSpec((1, de2 // 2, d), lambda i, src, dst, be, on, nc: (be[i], 0, 0)),
            pl.BlockSpec((1, 1, d), lambda i, src, dst, be, on, nc: (be[i], 0, 0)),
            pl.BlockSpec(dei.shape, lambda i, src, dst, be, on, nc: (0, 0)),
        ],
        out_specs=pl.BlockSpec(memory_space=pl.ANY),
        scratch_shapes=[
            pltpu.VMEM((2, CPB, CHUNK, dh), jnp.uint32),
            pltpu.VMEM((2, CPB, CHUNK, dh), jnp.uint32),
            pltpu.VMEM((TAIL_MAX, CHUNK, dh), jnp.uint32),
            pltpu.VMEM((d, de2 // 2), jnp.bfloat16),
            pltpu.VMEM((d, de2 // 2), jnp.bfloat16),
            pltpu.VMEM((de2 // 2, d), jnp.bfloat16),
            pltpu.SemaphoreType.DMA((2,)),
            pltpu.SemaphoreType.DMA((2,)),
            pltpu.SemaphoreType.DMA(()),
        ],
    )
    return pl.pallas_call(
        _experts_kernel,
        grid_spec=grid_spec,
        out_shape=jax.ShapeDtypeStruct((n_chunks + 2 * CPB, CHUNK, dh), jnp.uint32),
        compiler_params=pltpu.CompilerParams(
            dimension_semantics=("arbitrary",), vmem_limit_bytes=VMEM_LIMIT),
        name="experts",
    )(chunk_src, chunk_dst, block_expert, block_active, nvc, xs_chunks, w1, b1g, b1l, w2, b2, dei)


_PIECES = tuple(1 << i for i in reversed(range((CPT).bit_length())))


def _combine_kernel(nvc_ref,
                    y_hbm, x1_ref, slot_ref, gate_ref, gfin_ref, o_ref, ybuf, sem):
    i = pl.program_id(0)
    n_steps = pl.num_programs(0)
    sl = i % 2

    def fetch(step, s_, start):
        for j in range(COMBINE_TILES):
            tile = step * COMBINE_TILES + j
            n = nvc_ref[tile]
            off = jnp.int32(0)
            for pc in _PIECES:
                if pc * CHUNK > R_CAP:
                    continue
                take = (n & pc) != 0

                @pl.when(take)
                def _(off=off, pc=pc, tile=tile, j=j):
                    cp = pltpu.make_async_copy(y_hbm.at[pl.ds(tile * CPT + off, pc)],
                                               ybuf.at[s_, j, pl.ds(off, pc)], sem.at[s_])
                    if start:
                        cp.start()
                    else:
                        cp.wait()
                off = off + jnp.where(take, pc, 0)

    @pl.when(i == 0)
    def _():
        ybuf[...] = jnp.zeros_like(ybuf)
        fetch(0, 0, True)

    @pl.when(i + 1 < n_steps)
    def _():
        fetch(i + 1, 1 - sl, True)

    fetch(i, sl, False)

    def tile_program(j):
        cols = slice(j * TS, (j + 1) * TS)
        iota_r = lax.broadcasted_iota(jnp.int32, (R_CAP, TS), 0)
        gt = jnp.zeros((R_CAP, TS), jnp.float32)
        for k in range(TOP_K):
            gt = jnp.where(iota_r == slot_ref[k:k + 1, cols], gate_ref[k:k + 1, cols], gt)
        gt = gt.astype(jnp.bfloat16)
        yield "gates"
        moe = lax.dot_general(gt, _unpack_halves(ybuf[sl, j].reshape(R_CAP, ybuf.shape[4])),
                              (((0,), (0,)), ((), ())), preferred_element_type=jnp.float32)
        yield "unpermute"
        o_ref[cols, :] = _rms(x1_ref[cols, :] + moe, gfin_ref[...])

    progs = [tile_program(j) for j in range(COMBINE_TILES)]
    for rnd in range(COMBINE_TILES + 2):
        for j, pr in enumerate(progs):
            if 0 <= rnd - j <= 2:
                next(pr, None)


def _combine(nvc, y_chunks, x1, slot_t, gate_t, g_final):
    t, d = x1.shape
    nt = t // TS
    assert nt % COMBINE_TILES == 0
    rows = COMBINE_TILES * TS
    grid_spec = pltpu.PrefetchScalarGridSpec(
        num_scalar_prefetch=1,
        grid=(nt // COMBINE_TILES,),
        in_specs=[
            pl.BlockSpec(memory_space=pl.ANY),
            pl.BlockSpec((rows, d), lambda i, nv: (i, 0)),
            pl.BlockSpec((TOP_K, rows), lambda i, nv: (0, i)),
            pl.BlockSpec((TOP_K, rows), lambda i, nv: (0, i)),
            pl.BlockSpec((1, d), lambda i, nv: (0, 0)),
        ],
        out_specs=pl.BlockSpec((rows, d), lambda i, nv: (i, 0)),
        scratch_shapes=[pltpu.VMEM((2, COMBINE_TILES, CPT, CHUNK, d // 2), jnp.uint32),
                        pltpu.SemaphoreType.DMA((2,))],
    )
    return pl.pallas_call(
        _combine_kernel,
        grid_spec=grid_spec,
        out_shape=jax.ShapeDtypeStruct((t, d), jnp.float32),
        compiler_params=pltpu.CompilerParams(
            dimension_semantics=("arbitrary",), vmem_limit_bytes=VMEM_LIMIT),
        name="combine",
    )(nvc, y_chunks, x1, slot_t, gate_t, g_final)


def _chunk_plan(pcc, nb):
    nt = pcc.shape[0]
    i32 = jnp.int32
    pcc_t = pcc.T
    run_start_t = (jnp.cumsum(pcc, axis=1) - pcc).T
    cum_incl = jnp.cumsum(pcc_t, axis=1)
    eblocks = (cum_incl[:, -1] + CPB - 1) // CPB
    bstart = jnp.cumsum(eblocks) - eblocks
    blk = jnp.arange(nb, dtype=i32)
    bexp = jnp.sum(bstart[None, :] <= blk[:, None], axis=1).astype(i32) - 1
    oh = bexp[:, None] == jnp.arange(N_EXPERTS, dtype=i32)[None, :]
    pick = lambda tab: jnp.sum(jnp.where(oh[:, :, None], tab[None], 0), axis=1)
    ci, pc_row, rs_row = pick(cum_incl), pick(pcc_t), pick(run_start_t)
    b0 = jnp.sum(jnp.where(oh, bstart[None, :], 0), axis=1)
    p = (blk - b0)[:, None] * CPB + jnp.arange(CPB, dtype=i32)[None, :]
    before = ci[:, None, :] <= p[:, :, None]
    tile = jnp.sum(before, axis=2).astype(i32)
    cum_excl = jnp.sum(jnp.where(before, pc_row[:, None, :], 0), axis=2)
    at_tile = jnp.arange(nt, dtype=i32)[None, None, :] == tile[:, :, None]
    rs = jnp.sum(jnp.where(at_tile, rs_row[:, None, :], 0), axis=2)
    valid = tile < nt
    chunk = tile * CPT + rs + (p - cum_excl)
    trash = nt * CPT + (blk % 2)[:, None] * CPB + jnp.arange(CPB, dtype=i32)[None, :]
    src = jnp.where(valid, chunk, 0).astype(i32).reshape(-1)
    dst = jnp.where(valid, chunk, trash).astype(i32).reshape(-1)
    active = jnp.any(valid, axis=1).astype(i32)
    return src, dst, bexp, active


def kernel(x, g_mix, w_in, b_in, sinks, w_dw, b_dw, ln_g, ln_b, g_attn_out, g_conv_out, w_out, b_out,
           g_ffn, w_router, b_router, w1, b1, w2, b2, g_final):
    bsz, seq, d = x.shape
    t = bsz * seq
    nt = t // TS
    depth = g_mix.shape[0]
    assert depth == 1, "combine fuses the final RMSNorm, so exactly one layer is supported"
    f32 = jnp.float32
    max_chunks = nt * ((TS * TOP_K + N_EXPERTS * (CHUNK - 1)) // CHUNK) + N_EXPERTS * (CPB - 1)
    nb = -(-max_chunks // CPB)
    ci = jnp.arange(2 * LANES)
    dei = (ci[:, None] == jnp.where(ci < LANES, 2 * ci, 2 * (ci - LANES) + 1)[None, :]).astype(jnp.bfloat16)

    for l in range(depth):
        sink_cols = jnp.repeat(sinks[l].astype(f32).reshape(N_KV_HEADS, Q_PER_KV), WINDOW, axis=1)[..., None]
        x1 = _mix(x, g_mix[l][None], w_in[l].astype(jnp.bfloat16), b_in[l][None], sink_cols,
                  w_dw[l], b_dw[l][None], ln_g[l][None], ln_b[l][None],
                  g_attn_out[l][None], g_conv_out[l][None], w_out[l].astype(jnp.bfloat16), b_out[l][None])
        x1 = x1.reshape(t, d)
        xs, slot_t, gate_t, pcc = _route(x1, g_ffn[l][None], w_router[l], b_router[l])
        pcc = pcc[:, :, 0]
        src, dst, bexp, active = _chunk_plan(pcc, nb)
        b1l = b1[l].reshape(N_EXPERTS, 1, -1, 2)
        nvc = jnp.sum(pcc, axis=1).astype(jnp.int32)
        y = _experts(src, dst, bexp, active, nvc, xs.reshape(nt * CPT, CHUNK, d // 2), w1[l], b1l[..., 0], b1l[..., 1],
                     w2[l], b2[l][:, None, :], dei)
        x = _combine(nvc, y, x1, slot_t, gate_t, g_final[None]).reshape(bsz, seq, d)
    return x
```

```python
import jax
import jax.numpy as jnp
from jax import lax
from jax.experimental import pallas as pl
from jax.experimental.pallas import tpu as pltpu

HEAD_DIM = 64
N_Q_HEADS = 8
N_KV_HEADS = 2
Q_PER_KV = N_Q_HEADS // N_KV_HEADS
ATTN_WIDTH = N_Q_HEADS * HEAD_DIM
KV_WIDTH = N_KV_HEADS * HEAD_DIM
WINDOW = 128
CONV_KERNEL = 31
N_EXPERTS = 32
TOP_K = 4
SWIGLU_LIMIT = 7.0
SWIGLU_ALPHA = 1.702
RMS_EPS = 1e-5
LN_EPS = 1e-5

LANES = 128
SUBLANES = 8

TS = 256
CHUNK = SUBLANES
R_CAP = -(-(TS * TOP_K + N_EXPERTS * (CHUNK - 1)) // LANES) * LANES
CPT = R_CAP // CHUNK
TAIL_MAX = CPT - TS * TOP_K // CHUNK
_TAIL_PIECES = tuple(1 << i for i in reversed(range(TAIL_MAX.bit_length())))
MIX_SEQS = 2
MIX_COLS = 256
MIX_FILL_EVERY = 2
ROUTE_TILES = 4
COMBINE_TILES = 2
TM = 512
CPB = TM // CHUNK
CONV_HALO = 32
CONV_ROWS = 64
NEG_BIG = -1e30
VMEM_LIMIT = 56 * 1024 * 1024


def _rms(x, g):
    return x * lax.rsqrt(jnp.mean(x * x, axis=-1, keepdims=True) + RMS_EPS) * g


def _dot(a, b):
    return jnp.dot(a, b, preferred_element_type=jnp.float32)


_HI16 = 0xFFFF0000


def _pack_halves(v):
    half = v.shape[1] // 2
    as_bits = lambda t: lax.bitcast_convert_type(t.astype(jnp.bfloat16).astype(jnp.float32), jnp.uint32)
    return (as_bits(v[:, half:]) & jnp.uint32(_HI16)) | (as_bits(v[:, :half]) >> 16)


def _unpack_halves(w):
    lo = lax.bitcast_convert_type(w << 16, jnp.float32).astype(jnp.bfloat16)
    hi = lax.bitcast_convert_type(w & jnp.uint32(_HI16), jnp.float32).astype(jnp.bfloat16)
    return jnp.concatenate([lo, hi], axis=1)


def _dot_nt(a, b, precision=None):
    return lax.dot_general(a, b, (((1,), (1,)), ((), ())), precision=precision,
                           preferred_element_type=jnp.float32)


def _mix_kernel(x_ref, gmix_ref, win_ref, bin_ref, sink_ref, wdw_ref, bdw_ref, lng_ref, lnb_ref,
                ga_ref, gc_ref, wout_ref, bout_ref, o_ref, *scratch):
    params = (gmix_ref, win_ref, bin_ref, sink_ref, wdw_ref, bdw_ref, lng_ref, lnb_ref,
              ga_ref, gc_ref, wout_ref, bout_ref)
    per_seq = len(scratch) // MIX_SEQS
    seqs = [scratch[i * per_seq:(i + 1) * per_seq] for i in range(MIX_SEQS)]
    first = pl.program_id(1) == 0

    @pl.when(first)
    def _():
        for kv_prev, u_buf, *_ in seqs:
            kv_prev[...] = jnp.zeros_like(kv_prev)
            u_buf[0:CONV_HALO, :] = jnp.zeros((CONV_HALO, u_buf.shape[1]), jnp.float32)

    _emit_pipelined([_Stream(_mix_tile(x_ref.at[i], o_ref.at[i], first, params, *seqs[i]))
                     for i in range(MIX_SEQS)])


class _Stream:
    def __init__(self, gen):
        self.gen = gen
        self.tag = next(gen, None)

    def step(self):
        self.tag = next(self.gen, None)

    def run_while(self, tag):
        while self.tag == tag:
            self.step()


def _emit_pipelined(streams):
    streams[0].run_while("head")
    for i, cur in enumerate(streams):
        prv = streams[i - 1] if i >= 1 else None
        nxt = streams[i + 1] if i + 1 < len(streams) else None
        n = 0
        while cur.tag == "body":
            cur.step()
            n += 1
            if n % MIX_FILL_EVERY == 0:
                if prv is not None and prv.tag == "tail":
                    prv.step()
                elif nxt is not None and nxt.tag == "head":
                    nxt.step()
        if prv is not None:
            prv.run_while("tail")
        if nxt is not None:
            nxt.run_while("head")
    streams[-1].run_while("tail")


def _mix_tile(x_ref, o_ref, first, params, kv_prev, u_buf, u_sh, conv_buf, pbuf, mbuf):
    gmix_ref, win_ref, bin_ref, sink_ref, wdw_ref, bdw_ref, lng_ref, lnb_ref, ga_ref, gc_ref, wout_ref, bout_ref = params
    ub = None
    for c in range(win_ref.shape[1] // MIX_COLS):
        yield "head"
        if ub is None:
            ub = _rms(x_ref[0], gmix_ref[...]).astype(jnp.bfloat16)
        cs = slice(c * MIX_COLS, (c + 1) * MIX_COLS)
        pbuf[:, cs] = _dot(ub, win_ref[:, cs]) + bin_ref[:, cs]

    yield "body"
    o_q, o_k, o_v, o_a, o_g = 0, ATTN_WIDTH, ATTN_WIDTH + KV_WIDTH, ATTN_WIDTH + 2 * KV_WIDTH, \
        ATTN_WIDTH + 2 * KV_WIDTH + ATTN_WIDTH
    q = pbuf[:, o_q:o_k] * (HEAD_DIM ** -0.5)
    k = pbuf[:, o_k:o_v]
    v = pbuf[:, o_v:o_a]

    lane = lax.broadcasted_iota(jnp.int32, (WINDOW + TS, LANES), 1)
    low = lane < HEAD_DIM
    kfull = jnp.concatenate([kv_prev[:, 0:KV_WIDTH], k], axis=0)
    vfull = jnp.concatenate([kv_prev[:, KV_WIDTH:], v], axis=0)
    kroll = pltpu.roll(kfull, HEAD_DIM, axis=1)
    vroll = pltpu.roll(vfull, HEAD_DIM, axis=1)
    kk = [jnp.where(low, kfull, kroll).astype(jnp.bfloat16), jnp.where(low, kroll, kfull).astype(jnp.bfloat16)]
    vv = [jnp.where(low, vfull, vroll).astype(jnp.bfloat16), jnp.where(low, vroll, vfull).astype(jnp.bfloat16)]
    kv_prev[:, 0:KV_WIDTH] = k[TS - WINDOW:, :]
    kv_prev[:, KV_WIDTH:] = v[TS - WINDOW:, :]

    rows = Q_PER_KV * WINDOW
    qi = lax.broadcasted_iota(jnp.int32, (rows, 2 * WINDOW), 0) & (WINDOW - 1)
    kj = lax.broadcasted_iota(jnp.int32, (rows, 2 * WINDOW), 1)
    band = (kj > qi) & (kj <= qi + WINDOW)
    lane_q = lax.broadcasted_iota(jnp.int32, (WINDOW, LANES), 1)
    low_q = lane_q < HEAD_DIM
    ones_kv = jnp.ones((2 * WINDOW, LANES), jnp.bfloat16)

    def attn_scores(b, g):
        qb = q[b * WINDOW:(b + 1) * WINDOW, :]
        valid = band & ((kj >= WINDOW) | jnp.logical_not(first & (b == 0)))
        parts = []
        for i in range(Q_PER_KV):
            h = g * Q_PER_KV + i
            q128 = qb[:, (h // 2) * LANES:(h // 2 + 1) * LANES]
            keep = low_q if h % 2 == 0 else jnp.logical_not(low_q)
            parts.append(jnp.where(keep, q128, 0.0))
        qs = jnp.concatenate(parts, axis=0).astype(jnp.bfloat16)
        kb = kk[g][b * WINDOW:b * WINDOW + 2 * WINDOW, :]
        return jnp.where(valid, _dot_nt(qs, kb), NEG_BIG)

    def attn_probs(g, sc):
        sink = sink_ref[g]
        m = jnp.maximum(jnp.max(sc, axis=-1, keepdims=True), sink)
        return jnp.exp(sc - m).astype(jnp.bfloat16), jnp.exp(sink - m)

    def attn_out(b, g, p, sink_term):
        vb = vv[g][b * WINDOW:b * WINDOW + 2 * WINDOW, :]
        den = _dot(p, ones_kv) + sink_term
        pv = _dot(p, vb)
        o = pv / den
        return [o[i * WINDOW:(i + 1) * WINDOW, :] for i in range(Q_PER_KV)]

    cw = u_buf.shape[1]
    u_buf[CONV_HALO:CONV_HALO + TS, :] = pbuf[:, o_a:o_g] * jax.nn.sigmoid(pbuf[:, o_g:])
    n_sh = u_sh.shape[1]
    for s in range(1, SUBLANES):
        u_sh[s - 1] = u_buf[s:s + n_sh, :]
    shift = CONV_HALO - (CONV_KERNEL - 1)

    def conv_strip(c, r0):
        cs = slice(c * LANES, (c + 1) * LANES)
        acc = jnp.zeros((CONV_ROWS, LANES), jnp.float32)
        for j in range(CONV_KERNEL):
            base, s = (shift + j) // SUBLANES * SUBLANES, (shift + j) % SUBLANES
            rs = slice(r0 + base, r0 + base + CONV_ROWS)
            win = u_buf[rs, cs] if s == 0 else u_sh[s - 1, rs, cs]
            acc = acc + wdw_ref[j:j + 1, cs] * win
        conv_buf[r0:r0 + CONV_ROWS, cs] = acc

    units = [(b, g) for b in range(TS // WINDOW) for g in range(N_KV_HEADS)]
    strips = [(c, r0) for c in range(cw // LANES) for r0 in range(0, TS, CONV_ROWS)]
    head_out = {}
    yield "body"
    scores = {u: attn_scores(*u) for u in units}
    yield "body"
    probs = {u: attn_probs(u[1], scores[u]) for u in units}
    yield "body"
    for u in units:
        head_out[u] = attn_out(*u, *probs[u])
    for c, r0 in strips:
        yield "body"
        conv_strip(c, r0)

    yield "body"
    attn_blocks = []
    for b in range(TS // WINDOW):
        heads = [o for g in range(N_KV_HEADS) for o in head_out[b, g]]
        cols = [jnp.where(low_q, heads[2 * j], heads[2 * j + 1]) for j in range(N_Q_HEADS // 2)]
        attn_blocks.append(jnp.concatenate(cols, axis=1))
    attn = jnp.concatenate(attn_blocks, axis=0)

    u_buf[0:CONV_HALO, :] = u_buf[TS:TS + CONV_HALO, :]
    cv = conv_buf[...] + bdw_ref[...]
    mu = jnp.mean(cv, axis=-1, keepdims=True)
    xc = cv - mu
    cv = xc * lax.rsqrt(jnp.mean(xc * xc, axis=-1, keepdims=True) + LN_EPS) * lng_ref[...] + lnb_ref[...]
    cv = cv * jax.nn.sigmoid(cv)

    mbuf[...] = jnp.concatenate([_rms(attn, ga_ref[...]), _rms(cv, gc_ref[...])], axis=1).astype(jnp.bfloat16)

    for c in range(wout_ref.shape[1] // MIX_COLS):
        yield "tail"
        cs = slice(c * MIX_COLS, (c + 1) * MIX_COLS)
        o_ref[0, :, cs] = x_ref[0, :, cs] + _dot(mbuf[...], wout_ref[:, cs]) + bout_ref[:, cs]


def _mix(x, g_mix, w_in, b_in, sink_cols, w_dw, b_dw, ln_g, ln_b, g_a, g_c, w_out, b_out):
    bsz, seq, d = x.shape
    assert seq % TS == 0 and TS % WINDOW == 0 and CONV_HALO >= CONV_KERNEL - 1 and bsz % MIX_SEQS == 0
    cw = w_dw.shape[1]
    const = lambda shape: pl.BlockSpec(shape, lambda b, s: (0,) * len(shape))
    x = x.reshape(MIX_SEQS, bsz // MIX_SEQS, seq, d)
    seq_scratch = [
        pltpu.VMEM((WINDOW, 2 * KV_WIDTH), jnp.float32),
        pltpu.VMEM((CONV_HALO + TS, cw), jnp.float32),
        pltpu.VMEM((SUBLANES - 1, CONV_HALO + TS - SUBLANES, cw), jnp.float32),
        pltpu.VMEM((TS, cw), jnp.float32),
        pltpu.VMEM((TS, w_in.shape[1]), jnp.float32),
        pltpu.VMEM((TS, w_out.shape[0]), jnp.bfloat16),
    ]
    return pl.pallas_call(
        _mix_kernel,
        grid=(bsz // MIX_SEQS, seq // TS),
        in_specs=[
            pl.BlockSpec((MIX_SEQS, 1, TS, d), lambda b, s: (0, b, s, 0)),
            const((1, d)), const(w_in.shape), const((1, w_in.shape[1])),
            const(sink_cols.shape), const(w_dw.shape), const((1, cw)), const((1, cw)), const((1, cw)),
            const((1, ATTN_WIDTH)), const((1, cw)), const(w_out.shape), const((1, d)),
        ],
        out_specs=pl.BlockSpec((MIX_SEQS, 1, TS, d), lambda b, s: (0, b, s, 0)),
        out_shape=jax.ShapeDtypeStruct(x.shape, jnp.float32),
        scratch_shapes=seq_scratch * MIX_SEQS,
        compiler_params=pltpu.CompilerParams(
            dimension_semantics=("arbitrary", "arbitrary"), vmem_limit_bytes=VMEM_LIMIT),
        name="mix",
    )(x, g_mix, w_in, b_in, sink_cols, w_dw, b_dw, ln_g, ln_b, g_a, g_c, w_out, b_out).reshape(bsz, seq, d)


def _route_kernel(x1_ref, gffn_ref, whi_ref, wlo_ref, brt_ref, xs_ref, slot_ref, gate_ref, pcc_ref):
    tiles = [_route_tile(i, x1_ref, gffn_ref, whi_ref, wlo_ref, brt_ref, xs_ref, slot_ref, gate_ref, pcc_ref)
             for i in range(ROUTE_TILES)]
    live = list(tiles)
    while live:
        live = [t for t in live if next(t, None) is not None]


def _route_tile(i, x1_ref, gffn_ref, whi_ref, wlo_ref, brt_ref, xs_ref, slot_ref, gate_ref, pcc_ref):
    cols = slice(i * TS, (i + 1) * TS)
    h = _rms(x1_ref[cols, :], gffn_ref[...])
    hb = h.astype(jnp.bfloat16)
    h_lo = (h - hb.astype(jnp.float32)).astype(jnp.bfloat16)
    yield "split"
    lg_t = _dot(hb, whi_ref[...]) + (_dot(hb, wlo_ref[...]) + _dot(h_lo, whi_ref[...]))
    lg = lg_t.T[0:N_EXPERTS, :] + brt_ref[...]
    yield "logits"
    iota_e = lax.broadcasted_iota(jnp.int32, (N_EXPERTS, TS), 0)
    vals, hots = [], []
    member = jnp.zeros((N_EXPERTS, TS), jnp.float32)
    for _ in range(TOP_K):
        mx = jnp.max(lg, axis=0, keepdims=True)
        idx = jnp.min(jnp.where(lg == mx, iota_e, N_EXPERTS), axis=0, keepdims=True)
        hot = iota_e == idx
        lg = jnp.where(hot, -jnp.inf, lg)
        member = member + hot.astype(jnp.float32)
        vals.append(mx)
        hots.append(hot)
    ex = [jnp.exp(vk - vals[0]) for vk in vals]
    den = ex[0] + ex[1] + ex[2] + ex[3]
    gate_ref[:, cols] = jnp.concatenate([e / den for e in ex], axis=0)
    yield "top4"

    ti = lax.broadcasted_iota(jnp.int32, (TS, TS), 0)
    tj = lax.broadcasted_iota(jnp.int32, (TS, TS), 1)
    upper = jnp.where(ti < tj, 1.0, 0.0).astype(jnp.bfloat16)
    cum = _dot(member.astype(jnp.bfloat16), upper)
    cnt = jnp.sum(member, axis=1, keepdims=True)
    pcc = jnp.floor((cnt + (CHUNK - 1)) * (1.0 / CHUNK))
    pcb = jnp.broadcast_to(pcc, (N_EXPERTS, LANES))
    row_e = lax.broadcasted_iota(jnp.int32, (N_EXPERTS, LANES), 0)
    inc = pcb
    sh = 1
    while sh < N_EXPERTS:
        inc = inc + jnp.where(row_e >= sh, pltpu.roll(inc, sh, axis=0), 0.0)
        sh *= 2
    run_start = (inc - pcb)[:, 0:1] * CHUNK
    pcc_ref[i] = pcb.astype(jnp.int32)

    pos = run_start + cum
    slots = [jnp.sum(jnp.where(hot, pos, 0.0), axis=0, keepdims=True).astype(jnp.int32) for hot in hots]
    slot_ref[:, cols] = jnp.concatenate(slots, axis=0)
    yield "slots"

    iota_r = lax.broadcasted_iota(jnp.int32, (R_CAP, TS), 0)
    sel = (iota_r == slots[0]) | (iota_r == slots[1]) | (iota_r == slots[2]) | (iota_r == slots[3])
    perm = jnp.where(sel, 1.0, 0.0).astype(jnp.bfloat16)
    xs = _dot(perm, hb)
    yield "permute"
    xs_ref[i] = _pack_halves(xs)


def _route(x1, g_ffn, w_router, b_router):
    t, d = x1.shape
    nt = t // TS
    assert nt % ROUTE_TILES == 0
    w_pad = jnp.pad(w_router.astype(jnp.float32), ((0, 0), (0, LANES - N_EXPERTS)))
    w_hi = w_pad.astype(jnp.bfloat16)
    w_lo = (w_pad - w_hi.astype(jnp.float32)).astype(jnp.bfloat16)
    br_t = b_router.astype(jnp.float32)[:, None]
    const = lambda shape: pl.BlockSpec(shape, lambda i: (0,) * len(shape))
    return pl.pallas_call(
        _route_kernel,
        grid=(nt // ROUTE_TILES,),
        in_specs=[pl.BlockSpec((ROUTE_TILES * TS, d), lambda i: (i, 0)), const((1, d)), const(w_hi.shape),
                  const(w_lo.shape), const(br_t.shape)],
        out_specs=[
            pl.BlockSpec((ROUTE_TILES, R_CAP, d // 2), lambda i: (i, 0, 0)),
            pl.BlockSpec((TOP_K, ROUTE_TILES * TS), lambda i: (0, i)),
            pl.BlockSpec((TOP_K, ROUTE_TILES * TS), lambda i: (0, i)),
            pl.BlockSpec((ROUTE_TILES, N_EXPERTS, LANES), lambda i: (i, 0, 0)),
        ],
        out_shape=[
            jax.ShapeDtypeStruct((nt, R_CAP, d // 2), jnp.uint32),
            jax.ShapeDtypeStruct((TOP_K, t), jnp.int32),
            jax.ShapeDtypeStruct((TOP_K, t), jnp.float32),
            jax.ShapeDtypeStruct((nt, N_EXPERTS, LANES), jnp.int32),
        ],
        compiler_params=pltpu.CompilerParams(
            dimension_semantics=("arbitrary",), vmem_limit_bytes=VMEM_LIMIT),
        name="route",
    )(x1, g_ffn, w_hi, w_lo, br_t)


def _experts_kernel(src_ref, dst_ref, bexp_ref, act_ref, nvc_ref,
                    xs_hbm, w1_ref, b1g_ref, b1l_ref, w2_ref, b2_ref, dei_ref,
                    y_hbm,
                    xbuf, ybuf, zbuf, w1g, w1l, w2b, sem_in, sem_out, sem_z):
    b = pl.program_id(0)
    nb = pl.num_programs(0)
    n_tiles = nvc_ref.shape[0]
    n_real = n_tiles * CPT
    slot = b % 2

    def zero_tail(tile, start):
        n = CPT - nvc_ref[tile]
        off = tile * CPT + nvc_ref[tile]
        for pc in _TAIL_PIECES:
            take = (n & pc) != 0

            @pl.when(take)
            def _(off=off, pc=pc):
                cp = pltpu.make_async_copy(zbuf.at[pl.ds(0, pc)], y_hbm.at[pl.ds(off, pc)], sem_z)
                if start:
                    cp.start()
                else:
                    cp.wait()
            off = off + jnp.where(take, pc, 0)

    @pl.when(b == 0)
    def _():
        zbuf[...] = jnp.zeros_like(zbuf)
        for k in range(2 * CPB // TAIL_MAX):
            cp = pltpu.make_async_copy(zbuf, y_hbm.at[pl.ds(n_real + k * TAIL_MAX, TAIL_MAX)], sem_z)
            cp.start()
            cp.wait()

    @pl.when(b < n_tiles)
    def _():
        zero_tail(b, True)

    def active(blk):
        return act_ref[blk] > 0

    def start_all(blk, sl, inbound):
        for c in range(CPB):
            if inbound:
                pltpu.make_async_copy(xs_hbm.at[src_ref[blk * CPB + c]], xbuf.at[sl, c],
                                      sem_in.at[sl]).start(priority=c % 2)
            else:
                pltpu.make_async_copy(ybuf.at[sl, c], y_hbm.at[dst_ref[blk * CPB + c]],
                                      sem_out.at[sl]).start(priority=c % 2)

    def wait_all(sl, inbound):
        if inbound:
            pltpu.make_async_copy(xs_hbm.at[pl.ds(0, CPB)], xbuf.at[sl], sem_in.at[sl]).wait()
        else:
            pltpu.make_async_copy(ybuf.at[sl], y_hbm.at[pl.ds(0, CPB)], sem_out.at[sl]).wait()

    def compute():
        xb = _unpack_halves(xbuf[slot].reshape(TM, xbuf.shape[3]))
        hg = _dot(xb, w1g[...]) + b1g_ref[0]
        hl = _dot(xb, w1l[...]) + b1l_ref[0]
        hg = jnp.minimum(hg, SWIGLU_LIMIT)
        hl = jnp.clip(hl, -SWIGLU_LIMIT, SWIGLU_LIMIT)
        act = hg * jax.nn.sigmoid(SWIGLU_ALPHA * hg) * (hl + 1.0)
        y = _dot(act.astype(jnp.bfloat16), w2b[...]) + b2_ref[0]
        ybuf[slot] = _pack_halves(y).reshape(CPB, CHUNK, y.shape[1] // 2)

    @pl.when((b == 0) & active(0))
    def _():
        start_all(0, 0, True)

    prev_e = bexp_ref[jnp.maximum(b - 1, 0)]

    @pl.when((b == 0) | (bexp_ref[b] != prev_e))
    def _():
        dei = dei_ref[...]
        for j in range(w1_ref.shape[2] // (2 * LANES)):
            wj = w1_ref[0, :, j * 2 * LANES:(j + 1) * 2 * LANES].astype(jnp.bfloat16)
            r = _dot(wj, dei)
            w1g[:, j * LANES:(j + 1) * LANES] = r[:, 0:LANES].astype(jnp.bfloat16)
            w1l[:, j * LANES:(j + 1) * LANES] = r[:, LANES:].astype(jnp.bfloat16)
        w2b[...] = w2_ref[0].astype(jnp.bfloat16)

    @pl.when((b >= 2) & active(jnp.maximum(b - 2, 0)))
    def _():
        wait_all(slot, False)

    @pl.when(active(b))
    def _():
        wait_all(slot, True)

    prev_on = (b >= 1) & active(jnp.maximum(b - 1, 0))
    next_on = (b + 1 < nb) & active(jnp.minimum(b + 1, nb - 1))
    fast = prev_on & next_on & active(b)

    @pl.when(fast)
    def _():
        start_all(b - 1, 1 - slot, False)
        start_all(b + 1, 1 - slot, True)
        compute()

    @pl.when(jnp.logical_not(fast))
    def _():
        @pl.when(prev_on)
        def _():
            start_all(b - 1, 1 - slot, False)

        @pl.when(next_on)
        def _():
            start_all(b + 1, 1 - slot, True)

        @pl.when(active(b))
        def _():
            compute()

    @pl.when(b < n_tiles)
    def _():
        zero_tail(b, False)

    @pl.when(b == nb - 1)
    def _():
        @pl.when(active(b))
        def _():
            start_all(b, slot, False)

        @pl.when(prev_on)
        def _():
            wait_all(1 - slot, False)

        @pl.when(active(b))
        def _():
            wait_all(slot, False)


def _experts(chunk_src, chunk_dst, block_expert, block_active, nvc, xs_chunks, w1, b1g, b1l, w2, b2, dei):
    n_chunks, _, dh = xs_chunks.shape
    nb = block_expert.shape[0]
    assert nb >= nvc.shape[0] and (2 * CPB) % TAIL_MAX == 0
    d, de2 = w1.shape[1], w1.shape[2]
    grid_spec = pltpu.PrefetchScalarGridSpec(
        num_scalar_prefetch=5,
        grid=(nb,),
        in_specs=[
            pl.BlockSpec(memory_space=pl.ANY),
            pl.BlockSpec((1, d, de2), lambda i, src, dst, be, on, nc: (be[i], 0, 0)),
            pl.BlockSpec((1, 1, de2 // 2), lambda i, src, dst, be, on, nc: (be[i], 0, 0)),
            pl.BlockSpec((1, 1, de2 // 2), lambda i, src, dst, be, on, nc: (be[i], 0, 0)),
            pl.BlockSpec((1, de2 // 2, d), lambda i, src, dst, be, on, nc: (be[i], 0, 0)),
            pl.BlockSpec((1, 1, d), lambda i, src, dst, be, on, nc: (be[i], 0, 0)),
            pl.BlockSpec(dei.shape, lambda i, src, dst, be, on, nc: (0, 0)),
        ],
        out_specs=pl.BlockSpec(memory_space=pl.ANY),
        scratch_shapes=[
            pltpu.VMEM((2, CPB, CHUNK, dh), jnp.uint32),
            pltpu.VMEM((2, CPB, CHUNK, dh), jnp.uint32),
            pltpu.VMEM((TAIL_MAX, CHUNK, dh), jnp.uint32),
            pltpu.VMEM((d, de2 // 2), jnp.bfloat16),
            pltpu.VMEM((d, de2 // 2), jnp.bfloat16),
            pltpu.VMEM((de2 // 2, d), jnp.bfloat16),
            pltpu.SemaphoreType.DMA((2,)),
            pltpu.SemaphoreType.DMA((2,)),
            pltpu.SemaphoreType.DMA(()),
        ],
    )
    return pl.pallas_call(
        _experts_kernel,
        grid_spec=grid_spec,
        out_shape=jax.ShapeDtypeStruct((n_chunks + 2 * CPB, CHUNK, dh), jnp.uint32),
        compiler_params=pltpu.CompilerParams(
            dimension_semantics=("arbitrary",), vmem_limit_bytes=VMEM_LIMIT),
        name="experts",
    )(chunk_src, chunk_dst, block_expert, block_active, nvc, xs_chunks, w1, b1g, b1l, w2, b2, dei)


_PIECES = tuple(1 << i for i in reversed(range((CPT).bit_length())))


def _combine_kernel(nvc_ref,
                    y_hbm, x1_ref, slot_ref, gate_ref, gfin_ref, o_ref, ybuf, sem):
    i = pl.program_id(0)
    n_steps = pl.num_programs(0)
    sl = i % 2

    def fetch(step, s_, start):
        for j in range(COMBINE_TILES):
            tile = step * COMBINE_TILES + j
            n = nvc_ref[tile]
            off = jnp.int32(0)
            for pc in _PIECES:
                if pc * CHUNK > R_CAP:
                    continue
                take = (n & pc) != 0

                @pl.when(take)
                def _(off=off, pc=pc, tile=tile, j=j):
                    cp = pltpu.make_async_copy(y_hbm.at[pl.ds(tile * CPT + off, pc)],
                                               ybuf.at[s_, j, pl.ds(off, pc)], sem.at[s_])
                    if start:
                        cp.start()
                    else:
                        cp.wait()
                off = off + jnp.where(take, pc, 0)

    @pl.when(i == 0)
    def _():
        ybuf[...] = jnp.zeros_like(ybuf)
        fetch(0, 0, True)

    @pl.when(i + 1 < n_steps)
    def _():
        fetch(i + 1, 1 - sl, True)

    fetch(i, sl, False)

    def tile_program(j):
        cols = slice(j * TS, (j + 1) * TS)
        iota_r = lax.broadcasted_iota(jnp.int32, (R_CAP, TS), 0)
        gt = jnp.zeros((R_CAP, TS), jnp.float32)
        for k in range(TOP_K):
            gt = jnp.where(iota_r == slot_ref[k:k + 1, cols], gate_ref[k:k + 1, cols], gt)
        gt = gt.astype(jnp.bfloat16)
        yield "gates"
        moe = lax.dot_general(gt, _unpack_halves(ybuf[sl, j].reshape(R_CAP, ybuf.shape[4])),
                              (((0,), (0,)), ((), ())), preferred_element_type=jnp.float32)
        yield "unpermute"
        o_ref[cols, :] = _rms(x1_ref[cols, :] + moe, gfin_ref[...])

    progs = [tile_program(j) for j in range(COMBINE_TILES)]
    for rnd in range(COMBINE_TILES + 2):
        for j, pr in enumerate(progs):
            if 0 <= rnd - j <= 2:
                next(pr, None)


def _combine(nvc, y_chunks, x1, slot_t, gate_t, g_final):
    t, d = x1.shape
    nt = t // TS
    assert nt % COMBINE_TILES == 0
    rows = COMBINE_TILES * TS
    grid_spec = pltpu.PrefetchScalarGridSpec(
        num_scalar_prefetch=1,
        grid=(nt // COMBINE_TILES,),
        in_specs=[
            pl.BlockSpec(memory_space=pl.ANY),
            pl.BlockSpec((rows, d), lambda i, nv: (i, 0)),
            pl.BlockSpec((TOP_K, rows), lambda i, nv: (0, i)),
            pl.BlockSpec((TOP_K, rows), lambda i, nv: (0, i)),
            pl.BlockSpec((1, d), lambda i, nv: (0, 0)),
        ],
        out_specs=pl.BlockSpec((rows, d), lambda i, nv: (i, 0)),
        scratch_shapes=[pltpu.VMEM((2, COMBINE_TILES, CPT, CHUNK, d // 2), jnp.uint32),
                        pltpu.SemaphoreType.DMA((2,))],
    )
    return pl.pallas_call(
        _combine_kernel,
        grid_spec=grid_spec,
        out_shape=jax.ShapeDtypeStruct((t, d), jnp.float32),
        compiler_params=pltpu.CompilerParams(
            dimension_semantics=("arbitrary",), vmem_limit_bytes=VMEM_LIMIT),
        name="combine",
    )(nvc, y_chunks, x1, slot_t, gate_t, g_final)


def _chunk_plan(pcc, nb):
    nt = pcc.shape[0]
    i32 = jnp.int32
    pcc_t = pcc.T
    run_start_t = (jnp.cumsum(pcc, axis=1) - pcc).T
    cum_incl = jnp.cumsum(pcc_t, axis=1)
    eblocks = (cum_incl[:, -1] + CPB - 1) // CPB
    bstart = jnp.cumsum(eblocks) - eblocks
    blk = jnp.arange(nb, dtype=i32)
    bexp = jnp.sum(bstart[None, :] <= blk[:, None], axis=1).astype(i32) - 1
    oh = bexp[:, None] == jnp.arange(N_EXPERTS, dtype=i32)[None, :]
    pick = lambda tab: jnp.sum(jnp.where(oh[:, :, None], tab[None], 0), axis=1)
    ci, pc_row, rs_row = pick(cum_incl), pick(pcc_t), pick(run_start_t)
    b0 = jnp.sum(jnp.where(oh, bstart[None, :], 0), axis=1)
    p = (blk - b0)[:, None] * CPB + jnp.arange(CPB, dtype=i32)[None, :]
    before = ci[:, None, :] <= p[:, :, None]
    tile = jnp.sum(before, axis=2).astype(i32)
    cum_excl = jnp.sum(jnp.where(before, pc_row[:, None, :], 0), axis=2)
    at_tile = jnp.arange(nt, dtype=i32)[None, None, :] == tile[:, :, None]
    rs = jnp.sum(jnp.where(at_tile, rs_row[:, None, :], 0), axis=2)
    valid = tile < nt
    chunk = tile * CPT + rs + (p - cum_excl)
    trash = nt * CPT + (blk % 2)[:, None] * CPB + jnp.arange(CPB, dtype=i32)[None, :]
    src = jnp.where(valid, chunk, 0).astype(i32).reshape(-1)
    dst = jnp.where(valid, chunk, trash).astype(i32).reshape(-1)
    active = jnp.any(valid, axis=1).astype(i32)
    return src, dst, bexp, active


def kernel(x, g_mix, w_in, b_in, sinks, w_dw, b_dw, ln_g, ln_b, g_attn_out, g_conv_out, w_out, b_out,
           g_ffn, w_router, b_router, w1, b1, w2, b2, g_final):
    bsz, seq, d = x.shape
    t = bsz * seq
    nt = t // TS
    depth = g_mix.shape[0]
    assert depth == 1, "combine fuses the final RMSNorm, so exactly one layer is supported"
    f32 = jnp.float32
    max_chunks = nt * ((TS * TOP_K + N_EXPERTS * (CHUNK - 1)) // CHUNK) + N_EXPERTS * (CPB - 1)
    nb = -(-max_chunks // CPB)
    ci = jnp.arange(2 * LANES)
    dei = (ci[:, None] == jnp.where(ci < LANES, 2 * ci, 2 * (ci - LANES) + 1)[None, :]).astype(jnp.bfloat16)

    for l in range(depth):
        sink_cols = jnp.repeat(sinks[l].astype(f32).reshape(N_KV_HEADS, Q_PER_KV), WINDOW, axis=1)[..., None]
        x1 = _mix(x, g_mix[l][None], w_in[l].astype(jnp.bfloat16), b_in[l][None], sink_cols,
                  w_dw[l], b_dw[l][None], ln_g[l][None], ln_b[l][None],
                  g_attn_out[l][None], g_conv_out[l][None], w_out[l].astype(jnp.bfloat16), b_out[l][None])
        x1 = x1.reshape(t, d)
        xs, slot_t, gate_t, pcc = _route(x1, g_ffn[l][None], w_router[l], b_router[l])
        pcc = pcc[:, :, 0]
        src, dst, bexp, active = _chunk_plan(pcc, nb)
        b1l = b1[l].reshape(N_EXPERTS, 1, -1, 2)
        nvc = jnp.sum(pcc, axis=1).astype(jnp.int32)
        y = _experts(src, dst, bexp, active, nvc, xs.reshape(nt * CPT, CHUNK, d // 2), w1[l], b1l[..., 0], b1l[..., 1],
                     w2[l], b2[l][:, None, :], dei)
        x = _combine(nvc, y, x1, slot_t, gate_t, g_final[None]).reshape(bsz, seq, d)
    return x
```

```python
import jax
import jax.numpy as jnp
from jax import lax
from jax.experimental import pallas as pl
from jax.experimental.pallas import tpu as pltpu

HEAD_DIM = 64
N_Q_HEADS = 8
N_KV_HEADS = 2
Q_PER_KV = N_Q_HEADS // N_KV_HEADS
ATTN_WIDTH = N_Q_HEADS * HEAD_DIM
KV_WIDTH = N_KV_HEADS * HEAD_DIM
WINDOW = 128
CONV_KERNEL = 31
N_EXPERTS = 32
TOP_K = 4
SWIGLU_LIMIT = 7.0
SWIGLU_ALPHA = 1.702
RMS_EPS = 1e-5
LN_EPS = 1e-5

LANES = 128
SUBLANES = 8

TS = 256
CHUNK = SUBLANES
R_CAP = -(-(TS * TOP_K + N_EXPERTS * (CHUNK - 1)) // LANES) * LANES
CPT = R_CAP // CHUNK
TAIL_MAX = CPT - TS * TOP_K // CHUNK
_TAIL_PIECES = tuple(1 << i for i in reversed(range(TAIL_MAX.bit_length())))
MIX_SEQS = 2
MIX_COLS = 256
MIX_FILL_EVERY = 2
ROUTE_TILES = 4
COMBINE_TILES = 2
TM = 512
CPB = TM // CHUNK
CONV_HALO = 32
CONV_ROWS = 64
NEG_BIG = -1e30
VMEM_LIMIT = 56 * 1024 * 1024


def _rms(x, g):
    return x * lax.rsqrt(jnp.mean(x * x, axis=-1, keepdims=True) + RMS_EPS) * g


def _dot(a, b):
    return jnp.dot(a, b, preferred_element_type=jnp.float32)


_HI16 = 0xFFFF0000


def _pack_halves(v):
    half = v.shape[1] // 2
    as_bits = lambda t: lax.bitcast_convert_type(t.astype(jnp.bfloat16).astype(jnp.float32), jnp.uint32)
    return (as_bits(v[:, half:]) & jnp.uint32(_HI16)) | (as_bits(v[:, :half]) >> 16)


def _unpack_halves(w):
    lo = lax.bitcast_convert_type(w << 16, jnp.float32).astype(jnp.bfloat16)
    hi = lax.bitcast_convert_type(w & jnp.uint32(_HI16), jnp.float32).astype(jnp.bfloat16)
    return jnp.concatenate([lo, hi], axis=1)


def _dot_nt(a, b, precision=None):
    return lax.dot_general(a, b, (((1,), (1,)), ((), ())), precision=precision,
                           preferred_element_type=jnp.float32)


def _mix_kernel(x_ref, gmix_ref, win_ref, bin_ref, sink_ref, wdw_ref, bdw_ref, lng_ref, lnb_ref,
                ga_ref, gc_ref, wout_ref, bout_ref, o_ref, *scratch):
    params = (gmix_ref, win_ref, bin_ref, sink_ref, wdw_ref, bdw_ref, lng_ref, lnb_ref,
              ga_ref, gc_ref, wout_ref, bout_ref)
    per_seq = len(scratch) // MIX_SEQS
    seqs = [scratch[i * per_seq:(i + 1) * per_seq] for i in range(MIX_SEQS)]
    first = pl.program_id(1) == 0

    @pl.when(first)
    def _():
        for kv_prev, u_buf, *_ in seqs:
            kv_prev[...] = jnp.zeros_like(kv_prev)
            u_buf[0:CONV_HALO, :] = jnp.zeros((CONV_HALO, u_buf.shape[1]), jnp.float32)

    _emit_pipelined([_Stream(_mix_tile(x_ref.at[i], o_ref.at[i], first, params, *seqs[i]))
                     for i in range(MIX_SEQS)])


class _Stream:
    def __init__(self, gen):
        self.gen = gen
        self.tag = next(gen, None)

    def step(self):
        self.tag = next(self.gen, None)

    def run_while(self, tag):
        while self.tag == tag:
            self.step()


def _emit_pipelined(streams):
    streams[0].run_while("head")
    for i, cur in enumerate(streams):
        prv = streams[i - 1] if i >= 1 else None
        nxt = streams[i + 1] if i + 1 < len(streams) else None
        n = 0
        while cur.tag == "body":
            cur.step()
            n += 1
            if n % MIX_FILL_EVERY == 0:
                if prv is not None and prv.tag == "tail":
                    prv.step()
                elif nxt is not None and nxt.tag == "head":
                    nxt.step()
        if prv is not None:
            prv.run_while("tail")
        if nxt is not None:
            nxt.run_while("head")
    streams[-1].run_while("tail")


def _mix_tile(x_ref, o_ref, first, params, kv_prev, u_buf, u_sh, conv_buf, pbuf, mbuf):
    gmix_ref, win_ref, bin_ref, sink_ref, wdw_ref, bdw_ref, lng_ref, lnb_ref, ga_ref, gc_ref, wout_ref, bout_ref = params
    ub = None
    for c in range(win_ref.shape[1] // MIX_COLS):
        yield "head"
        if ub is None:
            ub = _rms(x_ref[0], gmix_ref[...]).astype(jnp.bfloat16)
        cs = slice(c * MIX_COLS, (c + 1) * MIX_COLS)
        pbuf[:, cs] = _dot(ub, win_ref[:, cs]) + bin_ref[:, cs]

    yield "body"
    o_q, o_k, o_v, o_a, o_g = 0, ATTN_WIDTH, ATTN_WIDTH + KV_WIDTH, ATTN_WIDTH + 2 * KV_WIDTH, \
        ATTN_WIDTH + 2 * KV_WIDTH + ATTN_WIDTH
    q = pbuf[:, o_q:o_k] * (HEAD_DIM ** -0.5)
    k = pbuf[:, o_k:o_v]
    v = pbuf[:, o_v:o_a]

    lane = lax.broadcasted_iota(jnp.int32, (WINDOW + TS, LANES), 1)
    low = lane < HEAD_DIM
    kfull = jnp.concatenate([kv_prev[:, 0:KV_WIDTH], k], axis=0)
    vfull = jnp.concatenate([kv_prev[:, KV_WIDTH:], v], axis=0)
    kroll = pltpu.roll(kfull, HEAD_DIM, axis=1)
    vroll = pltpu.roll(vfull, HEAD_DIM, axis=1)
    kk = [jnp.where(low, kfull, kroll).astype(jnp.bfloat16), jnp.where(low, kroll, kfull).astype(jnp.bfloat16)]
    vv = [jnp.where(low, vfull, vroll).astype(jnp.bfloat16), jnp.where(low, vroll, vfull).astype(jnp.bfloat16)]
    kv_prev[:, 0:KV_WIDTH] = k[TS - WINDOW:, :]
    kv_prev[:, KV_WIDTH:] = v[TS - WINDOW:, :]

    rows = Q_PER_KV * WINDOW
    qi = lax.broadcasted_iota(jnp.int32, (rows, 2 * WINDOW), 0) & (WINDOW - 1)
    kj = lax.broadcasted_iota(jnp.int32, (rows, 2 * WINDOW), 1)
    band = (kj > qi) & (kj <= qi + WINDOW)
    lane_q = lax.broadcasted_iota(jnp.int32, (WINDOW, LANES), 1)
    low_q = lane_q < HEAD_DIM
    ones_kv = jnp.ones((2 * WINDOW, LANES), jnp.bfloat16)

    def attn_scores(b, g):
        qb = q[b * WINDOW:(b + 1) * WINDOW, :]
        valid = band & ((kj >= WINDOW) | jnp.logical_not(first & (b == 0)))
        parts = []
        for i in range(Q_PER_KV):
            h = g * Q_PER_KV + i
            q128 = qb[:, (h // 2) * LANES:(h // 2 + 1) * LANES]
            keep = low_q if h % 2 == 0 else jnp.logical_not(low_q)
            parts.append(jnp.where(keep, q128, 0.0))
        qs = jnp.concatenate(parts, axis=0).astype(jnp.bfloat16)
        kb = kk[g][b * WINDOW:b * WINDOW + 2 * WINDOW, :]
        return jnp.where(valid, _dot_nt(qs, kb), NEG_BIG)

    def attn_probs(g, sc):
        sink = sink_ref[g]
        m = jnp.maximum(jnp.max(sc, axis=-1, keepdims=True), sink)
        return jnp.exp(sc - m).astype(jnp.bfloat16), jnp.exp(sink - m)

    def attn_out(b, g, p, sink_term):
        vb = vv[g][b * WINDOW:b * WINDOW + 2 * WINDOW, :]
        den = _dot(p, ones_kv) + sink_term
        pv = _dot(p, vb)
        o = pv / den
        return [o[i * WINDOW:(i + 1) * WINDOW, :] for i in range(Q_PER_KV)]

    cw = u_buf.shape[1]
    u_buf[CONV_HALO:CONV_HALO + TS, :] = pbuf[:, o_a:o_g] * jax.nn.sigmoid(pbuf[:, o_g:])
    n_sh = u_sh.shape[1]
    for s in range(1, SUBLANES):
        u_sh[s - 1] = u_buf[s:s + n_sh, :]
    shift = CONV_HALO - (CONV_KERNEL - 1)

    def conv_strip(c, r0):
        cs = slice(c * LANES, (c + 1) * LANES)
        acc = jnp.zeros((CONV_ROWS, LANES), jnp.float32)
        for j in range(CONV_KERNEL):
            base, s = (shift + j) // SUBLANES * SUBLANES, (shift + j) % SUBLANES
            rs = slice(r0 + base, r0 + base + CONV_ROWS)
            win = u_buf[rs, cs] if s == 0 else u_sh[s - 1, rs, cs]
            acc = acc + wdw_ref[j:j + 1, cs] * win
        conv_buf[r0:r0 + CONV_ROWS, cs] = acc

    units = [(b, g) for b in range(TS // WINDOW) for g in range(N_KV_HEADS)]
    strips = [(c, r0) for c in range(cw // LANES) for r0 in range(0, TS, CONV_ROWS)]
    head_out = {}
    yield "body"
    scores = {u: attn_scores(*u) for u in units}
    yield "body"
    probs = {u: attn_probs(u[1], scores[u]) for u in units}
    yield "body"
    for u in units:
        head_out[u] = attn_out(*u, *probs[u])
    for c, r0 in strips:
        yield "body"
        conv_strip(c, r0)

    yield "body"
    attn_blocks = []
    for b in range(TS // WINDOW):
        heads = [o for g in range(N_KV_HEADS) for o in head_out[b, g]]
        cols = [jnp.where(low_q, heads[2 * j], heads[2 * j + 1]) for j in range(N_Q_HEADS // 2)]
        attn_blocks.append(jnp.concatenate(cols, axis=1))
    attn = jnp.concatenate(attn_blocks, axis=0)

    u_buf[0:CONV_HALO, :] = u_buf[TS:TS + CONV_HALO, :]
    cv = conv_buf[...] + bdw_ref[...]
    mu = jnp.mean(cv, axis=-1, keepdims=True)
    xc = cv - mu
    cv = xc * lax.rsqrt(jnp.mean(xc * xc, axis=-1, keepdims=True) + LN_EPS) * lng_ref[...] + lnb_ref[...]
    cv = cv * jax.nn.sigmoid(cv)

    mbuf[...] = jnp.concatenate([_rms(attn, ga_ref[...]), _rms(cv, gc_ref[...])], axis=1).astype(jnp.bfloat16)

    for c in range(wout_ref.shape[1] // MIX_COLS):
        yield "tail"
        cs = slice(c * MIX_COLS, (c + 1) * MIX_COLS)
        o_ref[0, :, cs] = x_ref[0, :, cs] + _dot(mbuf[...], wout_ref[:, cs]) + bout_ref[:, cs]


def _mix(x, g_mix, w_in, b_in, sink_cols, w_dw, b_dw, ln_g, ln_b, g_a, g_c, w_out, b_out):
    bsz, seq, d = x.shape
    assert seq % TS == 0 and TS % WINDOW == 0 and CONV_HALO >= CONV_KERNEL - 1 and bsz % MIX_SEQS == 0
    cw = w_dw.shape[1]
    const = lambda shape: pl.BlockSpec(shape, lambda b, s: (0,) * len(shape))
    x = x.reshape(MIX_SEQS, bsz // MIX_SEQS, seq, d)
    seq_scratch = [
        pltpu.VMEM((WINDOW, 2 * KV_WIDTH), jnp.float32),
        pltpu.VMEM((CONV_HALO + TS, cw), jnp.float32),
        pltpu.VMEM((SUBLANES - 1, CONV_HALO + TS - SUBLANES, cw), jnp.float32),
        pltpu.VMEM((TS, cw), jnp.float32),
        pltpu.VMEM((TS, w_in.shape[1]), jnp.float32),
        pltpu.VMEM((TS, w_out.shape[0]), jnp.bfloat16),
    ]
    return pl.pallas_call(
        _mix_kernel,
        grid=(bsz // MIX_SEQS, seq // TS),
        in_specs=[
            pl.BlockSpec((MIX_SEQS, 1, TS, d), lambda b, s: (0, b, s, 0)),
            const((1, d)), const(w_in.shape), const((1, w_in.shape[1])),
            const(sink_cols.shape), const(w_dw.shape), const((1, cw)), const((1, cw)), const((1, cw)),
            const((1, ATTN_WIDTH)), const((1, cw)), const(w_out.shape), const((1, d)),
        ],
        out_specs=pl.BlockSpec((MIX_SEQS, 1, TS, d), lambda b, s: (0, b, s, 0)),
        out_shape=jax.ShapeDtypeStruct(x.shape, jnp.float32),
        scratch_shapes=seq_scratch * MIX_SEQS,
        compiler_params=pltpu.CompilerParams(
            dimension_semantics=("arbitrary", "arbitrary"), vmem_limit_bytes=VMEM_LIMIT),
        name="mix",
    )(x, g_mix, w_in, b_in, sink_cols, w_dw, b_dw, ln_g, ln_b, g_a, g_c, w_out, b_out).reshape(bsz, seq, d)


def _route_kernel(x1_ref, gffn_ref, whi_ref, wlo_ref, brt_ref, xs_ref, slot_ref, gate_ref, pcc_ref):
    tiles = [_route_tile(i, x1_ref, gffn_ref, whi_ref, wlo_ref, brt_ref, xs_ref, slot_ref, gate_ref, pcc_ref)
             for i in range(ROUTE_TILES)]
    live = list(tiles)
    while live:
        live = [t for t in live if next(t, None) is not None]


def _route_tile(i, x1_ref, gffn_ref, whi_ref, wlo_ref, brt_ref, xs_ref, slot_ref, gate_ref, pcc_ref):
    cols = slice(i * TS, (i + 1) * TS)
    h = _rms(x1_ref[cols, :], gffn_ref[...])
    hb = h.astype(jnp.bfloat16)
    h_lo = (h - hb.astype(jnp.float32)).astype(jnp.bfloat16)
    yield "split"
    lg_t = _dot(hb, whi_ref[...]) + (_dot(hb, wlo_ref[...]) + _dot(h_lo, whi_ref[...]))
    lg = lg_t.T[0:N_EXPERTS, :] + brt_ref[...]
    yield "logits"
    iota_e = lax.broadcasted_iota(jnp.int32, (N_EXPERTS, TS), 0)
    vals, hots = [], []
    member = jnp.zeros((N_EXPERTS, TS), jnp.float32)
    for _ in range(TOP_K):
        mx = jnp.max(lg, axis=0, keepdims=True)
        idx = jnp.min(jnp.where(lg == mx, iota_e, N_EXPERTS), axis=0, keepdims=True)
        hot = iota_e == idx
        lg = jnp.where(hot, -jnp.inf, lg)
        member = member + hot.astype(jnp.float32)
        vals.append(mx)
        hots.append(hot)
    ex = [jnp.exp(vk - vals[0]) for vk in vals]
    den = ex[0] + ex[1] + ex[2] + ex[3]
    gate_ref[:, cols] = jnp.concatenate([e / den for e in ex], axis=0)
    yield "top4"

    ti = lax.broadcasted_iota(jnp.int32, (TS, TS), 0)
    tj = lax.broadcasted_iota(jnp.int32, (TS, TS), 1)
    upper = jnp.where(ti < tj, 1.0, 0.0).astype(jnp.bfloat16)
    cum = _dot(member.astype(jnp.bfloat16), upper)
    cnt = jnp.sum(member, axis=1, keepdims=True)
    pcc = jnp.floor((cnt + (CHUNK - 1)) * (1.0 / CHUNK))
    pcb = jnp.broadcast_to(pcc, (N_EXPERTS, LANES))
    row_e = lax.broadcasted_iota(jnp.int32, (N_EXPERTS, LANES), 0)
    inc = pcb
    sh = 1
    while sh < N_EXPERTS:
        inc = inc + jnp.where(row_e >= sh, pltpu.roll(inc, sh, axis=0), 0.0)
        sh *= 2
    run_start = (inc - pcb)[:, 0:1] * CHUNK
    pcc_ref[i] = pcb.astype(jnp.int32)

    pos = run_start + cum
    slots = [jnp.sum(jnp.where(hot, pos, 0.0), axis=0, keepdims=True).astype(jnp.int32) for hot in hots]
    slot_ref[:, cols] = jnp.concatenate(slots, axis=0)
    yield "slots"

    iota_r = lax.broadcasted_iota(jnp.int32, (R_CAP, TS), 0)
    sel = (iota_r == slots[0]) | (iota_r == slots[1]) | (iota_r == slots[2]) | (iota_r == slots[3])
    perm = jnp.where(sel, 1.0, 0.0).astype(jnp.bfloat16)
    xs = _dot(perm, hb)
    yield "permute"
    xs_ref[i] = _pack_halves(xs)


def _route(x1, g_ffn, w_router, b_router):
    t, d = x1.shape
    nt = t // TS
    assert nt % ROUTE_TILES == 0
    w_pad = jnp.pad(w_router.astype(jnp.float32), ((0, 0), (0, LANES - N_EXPERTS)))
    w_hi = w_pad.astype(jnp.bfloat16)
    w_lo = (w_pad - w_hi.astype(jnp.float32)).astype(jnp.bfloat16)
    br_t = b_router.astype(jnp.float32)[:, None]
    const = lambda shape: pl.BlockSpec(shape, lambda i: (0,) * len(shape))
    return pl.pallas_call(
        _route_kernel,
        grid=(nt // ROUTE_TILES,),
        in_specs=[pl.BlockSpec((ROUTE_TILES * TS, d), lambda i: (i, 0)), const((1, d)), const(w_hi.shape),
                  const(w_lo.shape), const(br_t.shape)],
        out_specs=[
            pl.BlockSpec((ROUTE_TILES, R_CAP, d // 2), lambda i: (i, 0, 0)),
            pl.BlockSpec((TOP_K, ROUTE_TILES * TS), lambda i: (0, i)),
            pl.BlockSpec((TOP_K, ROUTE_TILES * TS), lambda i: (0, i)),
            pl.BlockSpec((ROUTE_TILES, N_EXPERTS, LANES), lambda i: (i, 0, 0)),
        ],
        out_shape=[
            jax.ShapeDtypeStruct((nt, R_CAP, d // 2), jnp.uint32),
            jax.ShapeDtypeStruct((TOP_K, t), jnp.int32),
            jax.ShapeDtypeStruct((TOP_K, t), jnp.float32),
            jax.ShapeDtypeStruct((nt, N_EXPERTS, LANES), jnp.int32),
        ],
        compiler_params=pltpu.CompilerParams(
            dimension_semantics=("arbitrary",), vmem_limit_bytes=VMEM_LIMIT),
        name="route",
    )(x1, g_ffn, w_hi, w_lo, br_t)


def _experts_kernel(src_ref, dst_ref, bexp_ref, act_ref, nvc_ref,
                    xs_hbm, w1_ref, b1g_ref, b1l_ref, w2_ref, b2_ref, dei_ref,
                    y_hbm,
                    xbuf, ybuf, zbuf, w1g, w1l, w2b, sem_in, sem_out, sem_z):
    b = pl.program_id(0)
    nb = pl.num_programs(0)
    n_tiles = nvc_ref.shape[0]
    n_real = n_tiles * CPT
    slot = b % 2

    def zero_tail(tile, start):
        n = CPT - nvc_ref[tile]
        off = tile * CPT + nvc_ref[tile]
        for pc in _TAIL_PIECES:
            take = (n & pc) != 0

            @pl.when(take)
            def _(off=off, pc=pc):
                cp = pltpu.make_async_copy(zbuf.at[pl.ds(0, pc)], y_hbm.at[pl.ds(off, pc)], sem_z)
                if start:
                    cp.start()
                else:
                    cp.wait()
            off = off + jnp.where(take, pc, 0)

    @pl.when(b == 0)
    def _():
        zbuf[...] = jnp.zeros_like(zbuf)
        for k in range(2 * CPB // TAIL_MAX):
            cp = pltpu.make_async_copy(zbuf, y_hbm.at[pl.ds(n_real + k * TAIL_MAX, TAIL_MAX)], sem_z)
            cp.start()
            cp.wait()

    @pl.when(b < n_tiles)
    def _():
        zero_tail(b, True)

    def active(blk):
        return act_ref[blk] > 0

    def start_all(blk, sl, inbound):
        for c in range(CPB):
            if inbound:
                pltpu.make_async_copy(xs_hbm.at[src_ref[blk * CPB + c]], xbuf.at[sl, c], sem_in.at[sl]).start()
            else:
                pltpu.make_async_copy(ybuf.at[sl, c], y_hbm.at[dst_ref[blk * CPB + c]], sem_out.at[sl]).start()

    def wait_all(sl, inbound):
        if inbound:
            pltpu.make_async_copy(xs_hbm.at[pl.ds(0, CPB)], xbuf.at[sl], sem_in.at[sl]).wait()
        else:
            pltpu.make_async_copy(ybuf.at[sl], y_hbm.at[pl.ds(0, CPB)], sem_out.at[sl]).wait()

    def load_x():
        return _unpack_halves(xbuf[slot].reshape(TM, xbuf.shape[3]))

    def compute(xb=None):
        xb = load_x() if xb is None else xb
        hg = _dot(xb, w1g[...]) + b1g_ref[0]
        hl = _dot(xb, w1l[...]) + b1l_ref[0]
        hg = jnp.minimum(hg, SWIGLU_LIMIT)
        hl = jnp.clip(hl, -SWIGLU_LIMIT, SWIGLU_LIMIT)
        act = hg * jax.nn.sigmoid(SWIGLU_ALPHA * hg) * (hl + 1.0)
        y = _dot(act.astype(jnp.bfloat16), w2b[...]) + b2_ref[0]
        ybuf[slot] = _pack_halves(y).reshape(CPB, CHUNK, y.shape[1] // 2)

    @pl.when((b == 0) & active(0))
    def _():
        start_all(0, 0, True)

    prev_e = bexp_ref[jnp.maximum(b - 1, 0)]

    @pl.when((b == 0) | (bexp_ref[b] != prev_e))
    def _():
        dei = dei_ref[...]
        for j in range(w1_ref.shape[2] // (2 * LANES)):
            wj = w1_ref[0, :, j * 2 * LANES:(j + 1) * 2 * LANES].astype(jnp.bfloat16)
            r = _dot(wj, dei)
            w1g[:, j * LANES:(j + 1) * LANES] = r[:, 0:LANES].astype(jnp.bfloat16)
            w1l[:, j * LANES:(j + 1) * LANES] = r[:, LANES:].astype(jnp.bfloat16)
        w2b[...] = w2_ref[0].astype(jnp.bfloat16)

    @pl.when((b >= 2) & active(jnp.maximum(b - 2, 0)))
    def _():
        wait_all(slot, False)

    @pl.when(active(b))
    def _():
        wait_all(slot, True)

    prev_on = (b >= 1) & active(jnp.maximum(b - 1, 0))
    next_on = (b + 1 < nb) & active(jnp.minimum(b + 1, nb - 1))
    fast = prev_on & next_on & active(b)

    @pl.when(fast)
    def _():
        xb = load_x()
        start_all(b - 1, 1 - slot, False)
        start_all(b + 1, 1 - slot, True)
        compute(xb)

    @pl.when(jnp.logical_not(fast))
    def _():
        @pl.when(prev_on)
        def _():
            start_all(b - 1, 1 - slot, False)

        @pl.when(next_on)
        def _():
            start_all(b + 1, 1 - slot, True)

        @pl.when(active(b))
        def _():
            compute()

    @pl.when(b < n_tiles)
    def _():
        zero_tail(b, False)

    @pl.when(b == nb - 1)
    def _():
        @pl.when(active(b))
        def _():
            start_all(b, slot, False)

        @pl.when(prev_on)
        def _():
            wait_all(1 - slot, False)

        @pl.when(active(b))
        def _():
            wait_all(slot, False)


def _experts(chunk_src, chunk_dst, block_expert, block_active, nvc, xs_chunks, w1, b1g, b1l, w2, b2, dei):
    n_chunks, _, dh = xs_chunks.shape
    nb = block_expert.shape[0]
    assert nb >= nvc.shape[0] and (2 * CPB) % TAIL_MAX == 0
    d, de2 = w1.shape[1], w1.shape[2]
    grid_spec = pltpu.PrefetchScalarGridSpec(
        num_scalar_prefetch=5,
        grid=(nb,),
        in_specs=[
            pl.BlockSpec(memory_space=pl.ANY),
            pl.BlockSpec((1, d, de2), lambda i, src, dst, be, on, nc: (be[i], 0, 0)),
            pl.BlockSpec((1, 1, de2 // 2), lambda i, src, dst, be, on, nc: (be[i], 0, 0)),
            pl.BlockSpec((1, 1, de2 // 2), lambda i, src, dst, be, on, nc: (be[i], 0, 0)),
            pl.BlockSpec((1, de2 // 2, d), lambda i, src, dst, be, on, nc: (be[i], 0, 0)),
            pl.BlockSpec((1, 1, d), lambda i, src, dst, be, on, nc: (be[i], 0, 0)),
            pl.BlockSpec(dei.shape, lambda i, src, dst, be, on, nc: (0, 0)),
        ],
        out_specs=pl.BlockSpec(memory_space=pl.ANY),
        scratch_shapes=[
            pltpu.VMEM((2, CPB, CHUNK, dh), jnp.uint32),
            pltpu.VMEM((2, CPB, CHUNK, dh), jnp.uint32),
            pltpu.VMEM((TAIL_MAX, CHUNK, dh), jnp.uint32),
            pltpu.VMEM((d, de2 // 2), jnp.bfloat16),
            pltpu.VMEM((d, de2 // 2), jnp.bfloat16),
            pltpu.VMEM((de2 // 2, d), jnp.bfloat16),
            pltpu.SemaphoreType.DMA((2,)),
            pltpu.SemaphoreType.DMA((2,)),
            pltpu.SemaphoreType.DMA(()),
        ],
    )
    return pl.pallas_call(
        _experts_kernel,
        grid_spec=grid_spec,
        out_shape=jax.ShapeDtypeStruct((n_chunks + 2 * CPB, CHUNK, dh), jnp.uint32),
        compiler_params=pltpu.CompilerParams(
            dimension_semantics=("arbitrary",), vmem_limit_bytes=VMEM_LIMIT),
        name="experts",
    )(chunk_src, chunk_dst, block_expert, block_active, nvc, xs_chunks, w1, b1g, b1l, w2, b2, dei)


_PIECES = tuple(1 << i for i in reversed(range((CPT).bit_length())))


def _combine_kernel(nvc_ref,
                    y_hbm, x1_ref, slot_ref, gate_ref, gfin_ref, o_ref, ybuf, sem):
    i = pl.program_id(0)
    n_steps = pl.num_programs(0)
    sl = i % 2

    def fetch(step, s_, start):
        for j in range(COMBINE_TILES):
            tile = step * COMBINE_TILES + j
            n = nvc_ref[tile]
            off = jnp.int32(0)
            for pc in _PIECES:
                if pc * CHUNK > R_CAP:
                    continue
                take = (n & pc) != 0

                @pl.when(take)
                def _(off=off, pc=pc, tile=tile, j=j):
                    cp = pltpu.make_async_copy(y_hbm.at[pl.ds(tile * CPT + off, pc)],
                                               ybuf.at[s_, j, pl.ds(off, pc)], sem.at[s_])
                    if start:
                        cp.start()
                    else:
                        cp.wait()
                off = off + jnp.where(take, pc, 0)

    @pl.when(i == 0)
    def _():
        ybuf[...] = jnp.zeros_like(ybuf)
        fetch(0, 0, True)

    @pl.when(i + 1 < n_steps)
    def _():
        fetch(i + 1, 1 - sl, True)

    fetch(i, sl, False)

    def tile_program(j):
        cols = slice(j * TS, (j + 1) * TS)
        iota_r = lax.broadcasted_iota(jnp.int32, (R_CAP, TS), 0)
        gt = jnp.zeros((R_CAP, TS), jnp.float32)
        for k in range(TOP_K):
            gt = jnp.where(iota_r == slot_ref[k:k + 1, cols], gate_ref[k:k + 1, cols], gt)
        gt = gt.astype(jnp.bfloat16)
        yield "gates"
        moe = lax.dot_general(gt, _unpack_halves(ybuf[sl, j].reshape(R_CAP, ybuf.shape[4])),
                              (((0,), (0,)), ((), ())), preferred_element_type=jnp.float32)
        yield "unpermute"
        o_ref[cols, :] = _rms(x1_ref[cols, :] + moe, gfin_ref[...])

    progs = [tile_program(j) for j in range(COMBINE_TILES)]
    for rnd in range(COMBINE_TILES + 2):
        for j, pr in enumerate(progs):
            if 0 <= rnd - j <= 2:
                next(pr, None)


def _combine(nvc, y_chunks, x1, slot_t, gate_t, g_final):
    t, d = x1.shape
    nt = t // TS
    assert nt % COMBINE_TILES == 0
    rows = COMBINE_TILES * TS
    grid_spec = pltpu.PrefetchScalarGridSpec(
        num_scalar_prefetch=1,
        grid=(nt // COMBINE_TILES,),
        in_specs=[
            pl.BlockSpec(memory_space=pl.ANY),
            pl.BlockSpec((rows, d), lambda i, nv: (i, 0)),
            pl.BlockSpec((TOP_K, rows), lambda i, nv: (0, i)),
            pl.BlockSpec((TOP_K, rows), lambda i, nv: (0, i)),
            pl.BlockSpec((1, d), lambda i, nv: (0, 0)),
        ],
        out_specs=pl.BlockSpec((rows, d), lambda i, nv: (i, 0)),
        scratch_shapes=[pltpu.VMEM((2, COMBINE_TILES, CPT, CHUNK, d // 2), jnp.uint32),
                        pltpu.SemaphoreType.DMA((2,))],
    )
    return pl.pallas_call(
        _combine_kernel,
        grid_spec=grid_spec,
        out_shape=jax.ShapeDtypeStruct((t, d), jnp.float32),
        compiler_params=pltpu.CompilerParams(
            dimension_semantics=("arbitrary",), vmem_limit_bytes=VMEM_LIMIT),
        name="combine",
    )(nvc, y_chunks, x1, slot_t, gate_t, g_final)


def _chunk_plan(pcc, nb):
    nt = pcc.shape[0]
    i32 = jnp.int32
    pcc_t = pcc.T
    run_start_t = (jnp.cumsum(pcc, axis=1) - pcc).T
    cum_incl = jnp.cumsum(pcc_t, axis=1)
    eblocks = (cum_incl[:, -1] + CPB - 1) // CPB
    bstart = jnp.cumsum(eblocks) - eblocks
    blk = jnp.arange(nb, dtype=i32)
    bexp = jnp.sum(bstart[None, :] <= blk[:, None], axis=1).astype(i32) - 1
    oh = bexp[:, None] == jnp.arange(N_EXPERTS, dtype=i32)[None, :]
    pick = lambda tab: jnp.sum(jnp.where(oh[:, :, None], tab[None], 0), axis=1)
    ci, pc_row, rs_row = pick(cum_incl), pick(pcc_t), pick(run_start_t)
    b0 = jnp.sum(jnp.where(oh, bstart[None, :], 0), axis=1)
    p = (blk - b0)[:, None] * CPB + jnp.arange(CPB, dtype=i32)[None, :]
    before = ci[:, None, :] <= p[:, :, None]
    tile = jnp.sum(before, axis=2).astype(i32)
    cum_excl = jnp.sum(jnp.where(before, pc_row[:, None, :], 0), axis=2)
    at_tile = jnp.arange(nt, dtype=i32)[None, None, :] == tile[:, :, None]
    rs = jnp.sum(jnp.where(at_tile, rs_row[:, None, :], 0), axis=2)
    valid = tile < nt
    chunk = tile * CPT + rs + (p - cum_excl)
    trash = nt * CPT + (blk % 2)[:, None] * CPB + jnp.arange(CPB, dtype=i32)[None, :]
    src = jnp.where(valid, chunk, 0).astype(i32).reshape(-1)
    dst = jnp.where(valid, chunk, trash).astype(i32).reshape(-1)
    active = jnp.any(valid, axis=1).astype(i32)
    return src, dst, bexp, active


def kernel(x, g_mix, w_in, b_in, sinks, w_dw, b_dw, ln_g, ln_b, g_attn_out, g_conv_out, w_out, b_out,
           g_ffn, w_router, b_router, w1, b1, w2, b2, g_final):
    bsz, seq, d = x.shape
    t = bsz * seq
    nt = t // TS
    depth = g_mix.shape[0]
    assert depth == 1, "combine fuses the final RMSNorm, so exactly one layer is supported"
    f32 = jnp.float32
    max_chunks = nt * ((TS * TOP_K + N_EXPERTS * (CHUNK - 1)) // CHUNK) + N_EXPERTS * (CPB - 1)
    nb = -(-max_chunks // CPB)
    ci = jnp.arange(2 * LANES)
    dei = (ci[:, None] == jnp.where(ci < LANES, 2 * ci, 2 * (ci - LANES) + 1)[None, :]).astype(jnp.bfloat16)

    for l in range(depth):
        sink_cols = jnp.repeat(sinks[l].astype(f32).reshape(N_KV_HEADS, Q_PER_KV), WINDOW, axis=1)[..., None]
        x1 = _mix(x, g_mix[l][None], w_in[l].astype(jnp.bfloat16), b_in[l][None], sink_cols,
                  w_dw[l], b_dw[l][None], ln_g[l][None], ln_b[l][None],
                  g_attn_out[l][None], g_conv_out[l][None], w_out[l].astype(jnp.bfloat16), b_out[l][None])
        x1 = x1.reshape(t, d)
        xs, slot_t, gate_t, pcc = _route(x1, g_ffn[l][None], w_router[l], b_router[l])
        pcc = pcc[:, :, 0]
        src, dst, bexp, active = _chunk_plan(pcc, nb)
        b1l = b1[l].reshape(N_EXPERTS, 1, -1, 2)
        nvc = jnp.sum(pcc, axis=1).astype(jnp.int32)
        y = _experts(src, dst, bexp, active, nvc, xs.reshape(nt * CPT, CHUNK, d // 2), w1[l], b1l[..., 0], b1l[..., 1],
                     w2[l], b2[l][:, None, :], dei)
        x = _combine(nvc, y, x1, slot_t, gate_t, g_final[None]).reshape(bsz, seq, d)
    return x
```
